```python
import jax, jax.numpy as jnp
from jax import lax
import numpy as np

D_MODEL = 1024
BATCH = 32
SEQ = 2048
DEPTH = 1

HEAD_DIM = 64
NSA_HEADS = 8
NSA_KV_GROUPS = 2
NSA_HPG = NSA_HEADS // NSA_KV_GROUPS
NSA_WIDTH = NSA_HEADS * HEAD_DIM
NSA_KV_COLS = NSA_KV_GROUPS * HEAD_DIM
FOX_HEADS = 8
FOX_WIDTH = FOX_HEADS * HEAD_DIM
CMP_BLOCK = 32
CMP_STRIDE = 16
CMP_HIDDEN = 256
SEL_BLOCK = 64
SEL_TOPN = 16
SEL_Q_BLOCK = 32
WINDOW = 512
Q_BLOCK = 128
N_EXPERTS = 32
TOP_K = 4
D_FF = D_MODEL
SWIGLU_LIMIT = 7.0
SWIGLU_ALPHA = 1.702
RMS_EPS = 1e-6
NEG_INF = -1e30
FORCED_SCORE = 1e9
S1 = NSA_WIDTH
S2 = S1 + 6 * NSA_KV_COLS
S3 = S2 + 3 * NSA_HEADS
S4 = S3 + 3 * FOX_WIDTH
S5 = S4 + FOX_HEADS
D_IN = S5 + 2 * D_MODEL

kernel_name = 'hybrid_nsa_fox_moe_block'


def rms_norm(x, g):
    xf = x.astype(jnp.float32)
    y = xf * lax.rsqrt(jnp.mean(xf * xf, axis=-1, keepdims=True) + RMS_EPS)
    return (y * g.astype(jnp.float32)).astype(x.dtype)


def alibi_slopes(n):
    return 2.0 ** (-8.0 * jnp.arange(1, n + 1, dtype=jnp.float32) / n)


def masked_softmax(s, mask):
    p = jax.nn.softmax(jnp.where(mask, s, NEG_INF), axis=-1)
    return jnp.where(mask, p, 0.0)


def compress(kv, pos, w1, b1, w2, b2):
    B, T, G, Dh = kv.shape
    n_cmp = (T - CMP_BLOCK) // CMP_STRIDE + 1
    idx = CMP_STRIDE * jnp.arange(n_cmp)[:, None] + jnp.arange(CMP_BLOCK)[None, :]
    blk = kv[:, idx] + pos[None, None, :, None, :]
    blk = blk.transpose(0, 1, 3, 2, 4).reshape(B, n_cmp, G, CMP_BLOCK * Dh)
    hid = jax.nn.gelu(blk @ w1 + b1)
    return hid @ w2 + b2


def nsa_mixer(q, k_c, v_c, k_s, v_s, k_w, v_w, gates):
    B, T = q.shape[:2]
    G, HPG, Dh = NSA_KV_GROUPS, NSA_HPG, HEAD_DIM
    scale = Dh ** -0.5
    qg = q.reshape(B, T, G, HPG, Dh)
    slopes = alibi_slopes(NSA_HEADS).reshape(G, HPG)
    pos = jnp.arange(T)

    n_cmp = k_c.shape[1]
    c_start = CMP_STRIDE * jnp.arange(n_cmp)
    c_end = c_start + CMP_BLOCK - 1
    dist_c = pos[:, None] - c_end[None, :]
    mask_c = dist_c >= 0
    s_c = jnp.einsum('btghd,bcgd->bghtc', qg, k_c).astype(jnp.float32) * scale \
        - slopes[:, :, None, None] * jnp.abs(dist_c)
    p_c = masked_softmax(s_c, mask_c)
    o_c = jnp.einsum('bghtc,bcgd->btghd', p_c.astype(v_c.dtype), v_c)

    n_sel = T // SEL_BLOCK
    top_n = min(SEL_TOPN, n_sel)
    j = jnp.arange(n_sel)
    lo = c_start // SEL_BLOCK
    hi = c_end // SEL_BLOCK
    overlap = ((j[None, :] == lo[:, None]) | (j[None, :] == hi[:, None])).astype(jnp.float32)
    imp = jnp.einsum('bghtc,cj->bgtj', p_c, overlap)
    t_blk = pos // SEL_BLOCK
    valid = j[None, :] <= t_blk[:, None]
    forced = (j[None, :] == 0) | (j[None, :] == t_blk[:, None]) | (j[None, :] == t_blk[:, None] - 1)
    imp = jnp.where(forced & valid, FORCED_SCORE, imp)
    imp = jnp.where(valid, imp, -1.0)
    _, sel_idx = lax.top_k(imp, top_n)

    k_sb = k_s.reshape(B, n_sel, SEL_BLOCK, G, Dh).transpose(0, 3, 1, 2, 4)
    v_sb = v_s.reshape(B, n_sel, SEL_BLOCK, G, Dh).transpose(0, 3, 1, 2, 4)
    nqs = T // SEL_Q_BLOCK
    q_sblocks = qg.reshape(B, nqs, SEL_Q_BLOCK, G, HPG, Dh).transpose(1, 0, 3, 2, 4, 5)
    idx_blocks = sel_idx.reshape(B, G, nqs, SEL_Q_BLOCK, top_n).transpose(2, 0, 1, 3, 4)
    b_ar = jnp.arange(B)[:, None, None, None]
    g_ar = jnp.arange(G)[None, :, None, None]
    n_keys = top_n * SEL_BLOCK

    def sel_block(args):
        i, q_b, idx_b = args
        kb = k_sb[b_ar, g_ar, idx_b].reshape(B, G, SEL_Q_BLOCK, n_keys, Dh)
        vb = v_sb[b_ar, g_ar, idx_b].reshape(B, G, SEL_Q_BLOCK, n_keys, Dh)
        t = i * SEL_Q_BLOCK + jnp.arange(SEL_Q_BLOCK)
        s_pos = (idx_b[..., None] * SEL_BLOCK + jnp.arange(SEL_BLOCK)).reshape(B, G, SEL_Q_BLOCK, n_keys)
        dist = t[None, None, :, None] - s_pos
        s = jnp.einsum('bgqhd,bgqkd->bghqk', q_b, kb).astype(jnp.float32) * scale \
            - slopes[None, :, :, None, None] * jnp.abs(dist)[:, :, None]
        p = masked_softmax(s, (dist >= 0)[:, :, None])
        return jnp.einsum('bghqk,bgqkd->bgqhd', p.astype(vb.dtype), vb)

    o_s = lax.map(sel_block, (jnp.arange(nqs), q_sblocks, idx_blocks))
    o_s = o_s.transpose(1, 0, 3, 2, 4, 5).reshape(B, T, G, HPG, Dh)

    nwb = T // Q_BLOCK
    span = WINDOW + Q_BLOCK
    k_pad = jnp.pad(k_w, ((0, 0), (WINDOW, 0), (0, 0), (0, 0)))
    v_pad = jnp.pad(v_w, ((0, 0), (WINDOW, 0), (0, 0), (0, 0)))
    q_wblocks = qg.reshape(B, nwb, Q_BLOCK, G, HPG, Dh).transpose(1, 0, 3, 2, 4, 5)

    def win_block(args):
        i, q_b = args
        start = i * Q_BLOCK
        kb = lax.dynamic_slice_in_dim(k_pad, start, span, axis=1)
        vb = lax.dynamic_slice_in_dim(v_pad, start, span, axis=1)
        t = start + jnp.arange(Q_BLOCK)
        s_pos = start - WINDOW + jnp.arange(span)
        dist = t[:, None] - s_pos[None, :]
        mask = (dist >= 0) & (dist < WINDOW) & (s_pos[None, :] >= 0)
        s = jnp.einsum('bgqhd,bkgd->bghqk', q_b, kb).astype(jnp.float32) * scale \
            - slopes[:, :, None, None] * jnp.abs(dist)
        p = masked_softmax(s, mask)
        return jnp.einsum('bghqk,bkgd->bgqhd', p.astype(vb.dtype), vb)

    o_w = lax.map(win_block, (jnp.arange(nwb), q_wblocks))
    o_w = o_w.transpose(1, 0, 3, 2, 4, 5).reshape(B, T, G, HPG, Dh)

    gt = gates.reshape(B, T, 3, G, HPG, 1)
    o = gt[:, :, 0] * o_c + gt[:, :, 1] * o_s + gt[:, :, 2] * o_w
    return o.reshape(B, T, NSA_WIDTH)


def fox_mixer(q, k, v, f_logit, b_forget):
    B, T, H, Dh = q.shape
    scale = Dh ** -0.5
    log_f = jax.nn.log_sigmoid((f_logit + b_forget).astype(jnp.float32))
    c_t = jnp.cumsum(log_f, axis=1).transpose(0, 2, 1)
    nqb = T // Q_BLOCK
    q_blocks = q.reshape(B, nqb, Q_BLOCK, H, Dh).transpose(1, 0, 2, 3, 4)
    c_blocks = c_t.reshape(B, H, nqb, Q_BLOCK).transpose(2, 0, 1, 3)
    s_pos = jnp.arange(T)

    def fox_block(args):
        i, q_b, c_b = args
        t = i * Q_BLOCK + jnp.arange(Q_BLOCK)
        mask = s_pos[None, :] <= t[:, None]
        s = jnp.einsum('bqhd,bkhd->bhqk', q_b, k).astype(jnp.float32) * scale \
            + (c_b[..., None] - c_t[:, :, None, :])
        p = masked_softmax(s, mask)
        return jnp.einsum('bhqk,bkhd->bqhd', p.astype(v.dtype), v)

    o = lax.map(fox_block, (jnp.arange(nqb), q_blocks, c_blocks))
    return o.transpose(1, 0, 2, 3, 4).reshape(B, T, H * Dh)


def moe_ffn(h, w_router, b_router, w_e_gate, b_e_gate, w_e_up, b_e_up, w_e_down, b_e_down):
    B, T, D = h.shape
    xt = h.reshape(B * T, D)
    logits = (xt @ w_router + b_router).astype(jnp.float32)
    top_val, top_idx = lax.top_k(logits, TOP_K)
    top_w = jax.nn.softmax(top_val, axis=-1)
    combine = jnp.sum(jax.nn.one_hot(top_idx, N_EXPERTS, dtype=jnp.float32) * top_w[..., None], axis=1)

    def expert_step(acc, args):
        wg, bg, wu, bu, wd, bd, cw = args
        g = jnp.minimum(xt @ wg + bg, SWIGLU_LIMIT)
        u = jnp.clip(xt @ wu + bu, -SWIGLU_LIMIT, SWIGLU_LIMIT)
        y = ((u + 1.0) * g * jax.nn.sigmoid(SWIGLU_ALPHA * g)) @ wd + bd
        return acc + cw[:, None].astype(y.dtype) * y, None

    acc, _ = lax.scan(expert_step, jnp.zeros_like(xt),
                      (w_e_gate, b_e_gate, w_e_up, b_e_up, w_e_down, b_e_down, combine.T))
    return acc.reshape(B, T, D)


def hybrid_layer(x, g_attn, w_in, qn_nsa, kn_cmp, kn_sel, kn_win,
                 pos_cmp_k, w_ck1, b_ck1, w_ck2, b_ck2,
                 pos_cmp_v, w_cv1, b_cv1, w_cv2, b_cv2,
                 b_nsa_gate, qn_fox, kn_fox, b_forget,
                 w_up_nsa, w_up_fox, b_merge, w_out, g_mlp,
                 w_router, b_router, w_e_gate, b_e_gate, w_e_up, b_e_up, w_e_down, b_e_down):
    B, T, _ = x.shape
    h = rms_norm(x, g_attn)
    proj = h @ w_in
    q_n, kv_n, gate_n, qkv_f, f_logit, merge_logit = jnp.split(proj, [S1, S2, S3, S4, S5], axis=-1)

    q_n = rms_norm(q_n.reshape(B, T, NSA_HEADS, HEAD_DIM), qn_nsa)
    kv_n = kv_n.reshape(B, T, 6, NSA_KV_GROUPS, HEAD_DIM)
    k_c = rms_norm(compress(kv_n[:, :, 0], pos_cmp_k, w_ck1, b_ck1, w_ck2, b_ck2), kn_cmp)
    v_c = compress(kv_n[:, :, 1], pos_cmp_v, w_cv1, b_cv1, w_cv2, b_cv2)
    k_s = rms_norm(kv_n[:, :, 2], kn_sel)
    k_w = rms_norm(kv_n[:, :, 4], kn_win)
    nsa_gates = jax.nn.sigmoid(gate_n + b_nsa_gate)
    o_nsa = nsa_mixer(q_n, k_c, v_c, k_s, kv_n[:, :, 3], k_w, kv_n[:, :, 5], nsa_gates)

    qkv_f = qkv_f.reshape(B, T, 3, FOX_HEADS, HEAD_DIM)
    o_fox = fox_mixer(rms_norm(qkv_f[:, :, 0], qn_fox), rms_norm(qkv_f[:, :, 1], kn_fox),
                      qkv_f[:, :, 2], f_logit, b_forget)

    merge = jax.nn.sigmoid(merge_logit + b_merge).reshape(B, T, 2, D_MODEL)
    y = merge[:, :, 0] * (o_nsa @ w_up_nsa) + merge[:, :, 1] * (o_fox @ w_up_fox)
    x = x + y @ w_out

    x = x + moe_ffn(rms_norm(x, g_mlp), w_router, b_router, w_e_gate, b_e_gate,
                    w_e_up, b_e_up, w_e_down, b_e_down)
    return x


def setup_inputs(seed: int = 0) -> dict:
    key = jax.random.key(seed)
    ks = iter(jax.random.split(key, 40))

    def nrm(shape, scale):
        return scale * jax.random.normal(next(ks), (DEPTH,) + shape, dtype=jnp.float32)

    def gain(n):
        return 1.0 + nrm((n,), 0.02)

    x = jax.random.normal(next(ks), (BATCH, SEQ, D_MODEL), dtype=jnp.float32)
    cmp_in = CMP_BLOCK * HEAD_DIM
    return {
        'x': x,
        'g_attn': gain(D_MODEL),
        'w_in': nrm((D_MODEL, D_IN), D_MODEL ** -0.5),
        'qn_nsa': gain(HEAD_DIM),
        'kn_cmp': gain(HEAD_DIM),
        'kn_sel': gain(HEAD_DIM),
        'kn_win': gain(HEAD_DIM),
        'pos_cmp_k': nrm((CMP_BLOCK, HEAD_DIM), 0.1),
        'w_ck1': nrm((cmp_in, CMP_HIDDEN), cmp_in ** -0.5),
        'b_ck1': nrm((CMP_HIDDEN,), 0.02),
        'w_ck2': nrm((CMP_HIDDEN, HEAD_DIM), CMP_HIDDEN ** -0.5),
        'b_ck2': nrm((HEAD_DIM,), 0.02),
        'pos_cmp_v': nrm((CMP_BLOCK, HEAD_DIM), 0.1),
        'w_cv1': nrm((cmp_in, CMP_HIDDEN), cmp_in ** -0.5),
        'b_cv1': nrm((CMP_HIDDEN,), 0.02),
        'w_cv2': nrm((CMP_HIDDEN, HEAD_DIM), CMP_HIDDEN ** -0.5),
        'b_cv2': nrm((HEAD_DIM,), 0.02),
        'b_nsa_gate': nrm((3 * NSA_HEADS,), 0.1),
        'qn_fox': gain(HEAD_DIM),
        'kn_fox': gain(HEAD_DIM),
        'b_forget': 3.0 + nrm((FOX_HEADS,), 0.5),
        'w_up_nsa': nrm((NSA_WIDTH, D_MODEL), NSA_WIDTH ** -0.5),
        'w_up_fox': nrm((FOX_WIDTH, D_MODEL), FOX_WIDTH ** -0.5),
        'b_merge': nrm((2 * D_MODEL,), 0.1),
        'w_out': nrm((D_MODEL, D_MODEL), D_MODEL ** -0.5),
        'g_mlp': gain(D_MODEL),
        'w_router': nrm((D_MODEL, N_EXPERTS), D_MODEL ** -0.5),
        'b_router': nrm((N_EXPERTS,), 0.01),
        'w_e_gate': nrm((N_EXPERTS, D_MODEL, D_FF), D_MODEL ** -0.5),
        'b_e_gate': nrm((N_EXPERTS, D_FF), 0.02),
        'w_e_up': nrm((N_EXPERTS, D_MODEL, D_FF), D_MODEL ** -0.5),
        'b_e_up': nrm((N_EXPERTS, D_FF), 0.02),
        'w_e_down': nrm((N_EXPERTS, D_FF, D_MODEL), D_FF ** -0.5),
        'b_e_down': nrm((N_EXPERTS, D_MODEL), 0.02),
    }


def reference(x, g_attn, w_in, qn_nsa, kn_cmp, kn_sel, kn_win,
              pos_cmp_k, w_ck1, b_ck1, w_ck2, b_ck2,
              pos_cmp_v, w_cv1, b_cv1, w_cv2, b_cv2,
              b_nsa_gate, qn_fox, kn_fox, b_forget,
              w_up_nsa, w_up_fox, b_merge, w_out, g_mlp,
              w_router, b_router, w_e_gate, b_e_gate, w_e_up, b_e_up, w_e_down, b_e_down):
    for l in range(DEPTH):
        x = hybrid_layer(x, g_attn[l], w_in[l], qn_nsa[l], kn_cmp[l], kn_sel[l], kn_win[l],
                         pos_cmp_k[l], w_ck1[l], b_ck1[l], w_ck2[l], b_ck2[l],
                         pos_cmp_v[l], w_cv1[l], b_cv1[l], w_cv2[l], b_cv2[l],
                         b_nsa_gate[l], qn_fox[l], kn_fox[l], b_forget[l],
                         w_up_nsa[l], w_up_fox[l], b_merge[l], w_out[l], g_mlp[l],
                         w_router[l], b_router[l], w_e_gate[l], b_e_gate[l], w_e_up[l], b_e_up[l],
                         w_e_down[l], b_e_down[l])
    return x
```

```python
import functools

import jax
import jax.numpy as jnp
from jax import lax
from jax.experimental import pallas as pl
from jax.experimental.pallas import tpu as pltpu

F32 = jnp.float32
BF16 = jnp.bfloat16
I32 = jnp.int32

D_MODEL = 1024
HEAD_DIM = 64
HEAD_PAD = 128
NSA_HEADS = 8
NSA_GROUPS = 2
NSA_HPG = NSA_HEADS // NSA_GROUPS
FOX_HEADS = 8
CMP_BLOCK = 32
CMP_STRIDE = 16
CMP_HIDDEN = 256
N_CMP_PAD = 128
SEL_BLOCK = 64
SEL_TOPN = 16
N_SEL_BLOCKS = 32
WINDOW = 512
N_EXPERTS = 32
TOP_K = 4
SWIGLU_LIMIT = 7.0
SWIGLU_ALPHA = 1.702
RMS_EPS = 1e-6
NEG_BIG = -1e30
FORCED_SCORE = 1e9
QK_SCALE = HEAD_DIM ** -0.5

LANE_SEL = 64
LANE_POS = 96
LANE_FOX = 64

OFF_QN = 0
OFF_KS = OFF_QN + NSA_HEADS * HEAD_PAD
OFF_VS = OFF_KS + NSA_GROUPS * HEAD_PAD
OFF_KW = OFF_VS + NSA_GROUPS * HEAD_PAD
OFF_VW = OFF_KW + NSA_GROUPS * HEAD_PAD
OFF_KC = OFF_VW + NSA_GROUPS * HEAD_PAD
OFF_VC = OFF_KC + NSA_GROUPS * HEAD_DIM
OFF_QF = OFF_VC + NSA_GROUPS * HEAD_DIM
OFF_KF = OFF_QF + FOX_HEADS * HEAD_PAD
OFF_VF = OFF_KF + FOX_HEADS * HEAD_PAD
OFF_MISC = OFF_VF + FOX_HEADS * HEAD_PAD
W1_COLS = OFF_MISC + 128
MISC_GATES = 3 * NSA_HEADS

TM_PROJ = 256
TQ_NSA = 128
TK_SEL = 512
TQ_FOX = 256
TM_OUT = 256
TM_ROUTE = 256
TM_EXPERT = 512
TM_MOVE = 256
VMEM_LIMIT = 56 * 1024 * 1024


def _dot(a, b):
    return jnp.dot(a, b, preferred_element_type=F32)


def _dot_nt(a, b):
    return lax.dot_general(a, b, (((1,), (1,)), ((), ())), preferred_element_type=F32)


def _split3(v):
    hi = v.astype(BF16)
    r1 = v - hi.astype(F32)
    mid = r1.astype(BF16)
    lo = (r1 - mid.astype(F32)).astype(BF16)
    return hi, mid, lo


def _dot_exact_rhs(a_f32, b_bf16):
    hi, mid, lo = _split3(a_f32)
    return _dot(hi, b_bf16) + _dot(mid, b_bf16) + _dot(lo, b_bf16)


def _dot_exact_lhs(a_bf16, b_f32):
    hi, mid, lo = _split3(b_f32)
    return _dot(a_bf16, hi) + _dot(a_bf16, mid) + _dot(a_bf16, lo)


def _rms(x):
    return x * lax.rsqrt(jnp.mean(x * x, axis=-1, keepdims=True) + RMS_EPS)


def _head_rms(a, gain):
    ms = jnp.sum(a * a, axis=-1, keepdims=True) * (1.0 / HEAD_DIM)
    return a * lax.rsqrt(ms + RMS_EPS) * gain


def _pos_feat_k(pos, lane):
    hi = (256 * (pos >> 8)).astype(F32)
    lo = (pos & 255).astype(F32)
    return jnp.where(lane < LANE_POS + 2, 1.0, jnp.where(lane == LANE_POS + 2, hi, lo)) * (
        (lane >= LANE_POS) & (lane < LANE_POS + 4)).astype(F32)


def _in_proj_kernel(x_ref, g_ref, w_ref, gains_ref, bmisc_ref, pq_ref, pk_ref,
                    qn_ref, ks_ref, vs_ref, kw_ref, vw_ref, kc_ref, vc_ref,
                    qf_ref, kf_ref, vf_ref, gate_ref, carry_ref):
    tm = x_ref.shape[0]
    i = pl.program_id(1)
    h = (_rms(x_ref[...]) * g_ref[...]).astype(BF16)
    pos = i * tm + lax.broadcasted_iota(I32, (tm, 1), 0)
    lane = lax.broadcasted_iota(I32, (1, HEAD_PAD), 1)
    pos_hi = (256 * (pos >> 8)).astype(F32)
    pos_lo = (pos & 255).astype(F32)
    kfeat = _pos_feat_k(pos, lane)
    blk_onehot = ((lane - LANE_SEL) == (pos >> 6)).astype(F32)

    def seg(off, width):
        return _dot(h, w_ref[:, off:off + width])

    a = seg(OFF_QN, NSA_HEADS * HEAD_PAD)
    gq = gains_ref[0:1, :] * QK_SCALE
    for hd in range(NSA_HEADS):
        slope = 2.0 ** (-(hd + 1))
        qfeat = jnp.where(lane == LANE_POS, -slope * pos_hi,
                          jnp.where(lane == LANE_POS + 1, -slope * pos_lo,
                                    jnp.where((lane == LANE_POS + 2) | (lane == LANE_POS + 3), slope, 0.0)))
        sl = slice(hd * HEAD_PAD, (hd + 1) * HEAD_PAD)
        qn_ref[:, sl] = (_head_rms(a[:, sl], gq) + qfeat).astype(BF16)

    a = seg(OFF_KS, NSA_GROUPS * HEAD_PAD)
    for g in range(NSA_GROUPS):
        sl = slice(g * HEAD_PAD, (g + 1) * HEAD_PAD)
        ks_ref[:, sl] = (_head_rms(a[:, sl], gains_ref[1:2, :]) + kfeat + blk_onehot).astype(BF16)
    vs_ref[...] = seg(OFF_VS, NSA_GROUPS * HEAD_PAD).astype(BF16)
    a = seg(OFF_KW, NSA_GROUPS * HEAD_PAD)
    for g in range(NSA_GROUPS):
        sl = slice(g * HEAD_PAD, (g + 1) * HEAD_PAD)
        kw_ref[:, sl] = (_head_rms(a[:, sl], gains_ref[2:3, :]) + kfeat).astype(BF16)
    vw_ref[...] = seg(OFF_VW, NSA_GROUPS * HEAD_PAD).astype(BF16)
    kc_ref[...] = seg(OFF_KC, NSA_GROUPS * HEAD_DIM).astype(BF16)
    vc_ref[...] = seg(OFF_VC, NSA_GROUPS * HEAD_DIM).astype(BF16)

    misc = seg(OFF_MISC, 128) + bmisc_ref[...]
    is_gate = lane < MISC_GATES
    gate_ref[...] = jnp.where(is_gate, jax.nn.sigmoid(misc), 0.0)
    is_f = (lane >= MISC_GATES) & (lane < MISC_GATES + 3 * FOX_HEADS)
    logf = jnp.where(is_f, jax.nn.log_sigmoid(misc), 0.0)

    @pl.when(i == 0)
    def _():
        carry_ref[...] = jnp.zeros_like(carry_ref)

    r = lax.broadcasted_iota(I32, (tm, tm), 0)
    c = lax.broadcasted_iota(I32, (tm, tm), 1)
    tri = (r >= c).astype(BF16)
    csum = _dot_exact_lhs(tri, logf) + carry_ref[...]
    carry_ref[...] = csum[tm - 1:tm, :]
    c_hi, c_mid, c_lo = _split3(csum)
    c3 = jnp.where(lane < MISC_GATES + FOX_HEADS, c_hi,
                   jnp.where(lane < MISC_GATES + 2 * FOX_HEADS, c_mid, c_lo))
    c3 = jnp.where(is_f, c3, jnp.zeros_like(c3))
    ones_lo = ((lane >= LANE_FOX) & (lane < LANE_FOX + 3)).astype(F32)
    ones_hi = ((lane >= LANE_FOX + 3) & (lane < LANE_FOX + 6)).astype(F32)

    a = seg(OFF_QF, FOX_HEADS * HEAD_PAD)
    fq = _dot(c3, pq_ref[...])
    gq = gains_ref[3:4, :] * QK_SCALE
    for hd in range(FOX_HEADS):
        sl = slice(hd * HEAD_PAD, (hd + 1) * HEAD_PAD)
        qf_ref[:, sl] = (_head_rms(a[:, sl], gq) + fq[:, sl] + ones_hi).astype(BF16)
    a = seg(OFF_KF, FOX_HEADS * HEAD_PAD)
    fk = _dot(c3, pk_ref[...])
    for hd in range(FOX_HEADS):
        sl = slice(hd * HEAD_PAD, (hd + 1) * HEAD_PAD)
        kf_ref[:, sl] = (_head_rms(a[:, sl], gains_ref[4:5, :]) + fk[:, sl] + ones_lo).astype(BF16)
    vf_ref[...] = seg(OFF_VF, FOX_HEADS * HEAD_PAD).astype(BF16)


def _pad_heads(w, n_heads):
    lead = w.shape[:-1]
    w = w.reshape(lead + (n_heads, HEAD_DIM))
    w = jnp.pad(w, [(0, 0)] * len(lead) + [(0, 0), (0, HEAD_PAD - HEAD_DIM)])
    return w.reshape(lead + (n_heads * HEAD_PAD,))


def _pad_gain(g):
    return jnp.pad(g.astype(F32), (0, HEAD_PAD - HEAD_DIM))


def _fox_placement():
    pq = jnp.zeros((128, FOX_HEADS * HEAD_PAD), F32)
    pk = jnp.zeros((128, FOX_HEADS * HEAD_PAD), F32)
    for piece in range(3):
        for hd in range(FOX_HEADS):
            src = MISC_GATES + piece * FOX_HEADS + hd
            pq = pq.at[src, hd * HEAD_PAD + LANE_FOX + piece].set(1.0)
            pk = pk.at[src, hd * HEAD_PAD + LANE_FOX + 3 + piece].set(-1.0)
    return pq.astype(BF16), pk.astype(BF16)


def _in_proj(x, g_attn, w_in, qn_nsa, kn_sel, kn_win, b_nsa_gate, qn_fox, kn_fox, b_forget):
    B, T, D = x.shape
    s1 = NSA_HEADS * HEAD_DIM
    kv_w = NSA_GROUPS * HEAD_DIM
    s2 = s1 + 6 * kv_w
    s3 = s2 + 3 * NSA_HEADS
    fw = FOX_HEADS * HEAD_DIM
    s4 = s3 + 3 * fw
    s5 = s4 + FOX_HEADS
    kv = [w_in[:, s1 + j * kv_w: s1 + (j + 1) * kv_w] for j in range(6)]
    f_cols = w_in[:, s4:s5]
    misc = jnp.concatenate([w_in[:, s2:s3], f_cols, f_cols, f_cols,
                            jnp.zeros((D, 128 - MISC_GATES - 3 * FOX_HEADS), F32)], axis=1)
    w1 = jnp.concatenate([
        _pad_heads(w_in[:, :s1], NSA_HEADS),
        _pad_heads(kv[2], NSA_GROUPS), _pad_heads(kv[3], NSA_GROUPS),
        _pad_heads(kv[4], NSA_GROUPS), _pad_heads(kv[5], NSA_GROUPS),
        kv[0], kv[1],
        _pad_heads(w_in[:, s3:s3 + fw], FOX_HEADS),
        _pad_heads(w_in[:, s3 + fw:s3 + 2 * fw], FOX_HEADS),
        _pad_heads(w_in[:, s3 + 2 * fw:s4], FOX_HEADS),
        misc], axis=1).astype(BF16)
    assert w1.shape[1] == W1_COLS
    gains = jnp.stack([_pad_gain(qn_nsa), _pad_gain(kn_sel), _pad_gain(kn_win),
                       _pad_gain(qn_fox), _pad_gain(kn_fox),
                       jnp.zeros(HEAD_PAD, F32), jnp.zeros(HEAD_PAD, F32), jnp.zeros(HEAD_PAD, F32)])
    bmisc = jnp.concatenate([b_nsa_gate, b_forget, b_forget, b_forget,
                             jnp.zeros(128 - MISC_GATES - 3 * FOX_HEADS, F32)]).reshape(1, 128)
    pq, pk = _fox_placement()
    tm = TM_PROJ

    def tok(width):
        return pl.BlockSpec((None, tm, width), lambda b, i: (b, i, 0))

    def full(shape):
        return pl.BlockSpec(shape, lambda b, i: (0,) * len(shape))

    widths = [NSA_HEADS * HEAD_PAD] + [NSA_GROUPS * HEAD_PAD] * 4 + [NSA_GROUPS * HEAD_DIM] * 2 + \
             [FOX_HEADS * HEAD_PAD] * 3
    out_shape = [jax.ShapeDtypeStruct((B, T, w), BF16) for w in widths] + \
                [jax.ShapeDtypeStruct((B, T, 128), F32)]
    return pl.pallas_call(
        _in_proj_kernel,
        grid=(B, T // tm),
        in_specs=[tok(D), full((1, D)), full((D, W1_COLS)), full((8, HEAD_PAD)), full((1, 128)),
                  full((128, FOX_HEADS * HEAD_PAD)), full((128, FOX_HEADS * HEAD_PAD))],
        out_specs=[tok(w) for w in widths] + [tok(128)],
        out_shape=out_shape,
        scratch_shapes=[pltpu.VMEM((1, 128), F32)],
        compiler_params=pltpu.CompilerParams(
            dimension_semantics=("arbitrary", "arbitrary"), vmem_limit_bytes=VMEM_LIMIT),
        name="in_proj",
    )(x, g_attn.reshape(1, D), w1, gains, bmisc, pq, pk)


def _gelu_tanh(x):
    return 0.5 * x * (1.0 + jnp.tanh(0.7978845608028654 * (x + 0.044715 * (x * x * x))))


def _compress_kernel(ak_ref, av_ref, posk_ref, posv_ref, wk1_ref, bk1_ref, wk2_ref, bk2_ref,
                     wv1_ref, bv1_ref, wv2_ref, bv2_ref, gain_ref, kc_ref, vc_ref):
    half = (CMP_BLOCK // 2) * HEAD_DIM
    row = lax.broadcasted_iota(I32, (N_CMP_PAD, 1), 0)
    lane = lax.broadcasted_iota(I32, (1, HEAD_PAD), 1)
    kfeat = _pos_feat_k(CMP_STRIDE * row + CMP_BLOCK - 1, lane)

    def mlp(a, pos_ref, w1_ref, b1_ref, w2_ref, b2_ref):
        a = a.astype(F32)
        lo = (a + pos_ref[0:1, :]).astype(BF16)
        hi = (a + pos_ref[1:2, :]).astype(BF16)
        p_lo = _dot(lo, w1_ref[0:half, :])
        p_hi = _dot(hi, w1_ref[half:2 * half, :])
        hid = p_lo + pltpu.roll(p_hi, N_CMP_PAD - 1, 0) + b1_ref[...]
        return _dot(_gelu_tanh(hid).astype(BF16), w2_ref[...]) + b2_ref[...]

    for g in range(NSA_GROUPS):
        k = mlp(ak_ref[g], posk_ref, wk1_ref, bk1_ref, wk2_ref, bk2_ref)
        kc_ref[g] = (_head_rms(k, gain_ref[...]) + kfeat).astype(BF16)
        vc_ref[g] = mlp(av_ref[g], posv_ref, wv1_ref, bv1_ref, wv2_ref, bv2_ref).astype(BF16)


def _compress(kc_raw, vc_raw, pos_k, w_ck1, b_ck1, w_ck2, b_ck2, pos_v, w_cv1, b_cv1, w_cv2, b_cv2, kn_cmp):
    B, T, _ = kc_raw.shape
    n_chunks = T // CMP_STRIDE
    assert n_chunks == N_CMP_PAD
    chunk_w = CMP_STRIDE * HEAD_DIM

    def chunks(a):
        a = a.reshape(B, n_chunks, CMP_STRIDE, NSA_GROUPS, HEAD_DIM)
        return a.transpose(0, 3, 1, 2, 4).reshape(B, NSA_GROUPS, n_chunks, chunk_w)

    def w2pad(w, b):
        return (jnp.pad(w, ((0, 0), (0, HEAD_PAD - HEAD_DIM))).astype(BF16),
                jnp.pad(b, (0, HEAD_PAD - HEAD_DIM)).reshape(1, HEAD_PAD))

    wk2, bk2 = w2pad(w_ck2, b_ck2)
    wv2, bv2 = w2pad(w_cv2, b_cv2)
    a_spec = pl.BlockSpec((None, NSA_GROUPS, n_chunks, chunk_w), lambda b: (b, 0, 0, 0))
    o_spec = pl.BlockSpec((None, NSA_GROUPS, N_CMP_PAD, HEAD_PAD), lambda b: (b, 0, 0, 0))

    def full(shape):
        return pl.BlockSpec(shape, lambda b: (0,) * len(shape))

    hid2 = CMP_BLOCK * HEAD_DIM
    return pl.pallas_call(
        _compress_kernel,
        grid=(B,),
        in_specs=[a_spec, a_spec, full((2, chunk_w)), full((2, chunk_w)),
                  full((hid2, CMP_HIDDEN)), full((1, CMP_HIDDEN)), full((CMP_HIDDEN, HEAD_PAD)), full((1, HEAD_PAD)),
                  full((hid2, CMP_HIDDEN)), full((1, CMP_HIDDEN)), full((CMP_HIDDEN, HEAD_PAD)), full((1, HEAD_PAD)),
                  full((1, HEAD_PAD))],
        out_specs=[o_spec, o_spec],
        out_shape=[jax.ShapeDtypeStruct((B, NSA_GROUPS, N_CMP_PAD, HEAD_PAD), BF16)] * 2,
        compiler_params=pltpu.CompilerParams(dimension_semantics=("arbitrary",), vmem_limit_bytes=VMEM_LIMIT),
        name="compress",
    )(chunks(kc_raw), chunks(vc_raw), pos_k.reshape(2, chunk_w), pos_v.reshape(2, chunk_w),
      w_ck1.astype(BF16), b_ck1.reshape(1, -1), wk2, bk2,
      w_cv1.astype(BF16), b_cv1.reshape(1, -1), wv2, bv2, _pad_gain(kn_cmp).reshape(1, HEAD_PAD))


def _nsa_kernel(q_ref, ks_ref, vs_ref, kw_ref, vw_ref, kc_ref, vc_ref, gate_ref, ovl_ref, o_ref):
    tq = q_ref.shape[0]
    rows = NSA_HPG * tq
    g = pl.program_id(1)
    t0 = pl.program_id(2) * tq
    q4 = jnp.concatenate([q_ref[:, hh * HEAD_PAD:(hh + 1) * HEAD_PAD] for hh in range(NSA_HPG)], axis=0)
    t_row = t0 + (lax.broadcasted_iota(I32, (rows, 1), 0) & (tq - 1))
    t_q = t0 + lax.broadcasted_iota(I32, (tq, 1), 0)
    lane = lax.broadcasted_iota(I32, (1, HEAD_PAD), 1)

    s = _dot_nt(q4, kc_ref[...])
    mask = t_row >= CMP_STRIDE * lane + (CMP_BLOCK - 1)
    s = jnp.where(mask, s, NEG_BIG)
    p = jnp.where(mask, jnp.exp(s - jnp.max(s, axis=-1, keepdims=True)), 0.0)
    l = jnp.sum(p, axis=-1, keepdims=True)
    p = p / jnp.where(l > 0.0, l, 1.0)
    o_c = _dot(p.astype(BF16), vc_ref[...])

    p_sum = p[0:tq]
    for hh in range(1, NSA_HPG):
        p_sum = p_sum + p[hh * tq:(hh + 1) * tq]
    imp = _dot_exact_rhs(p_sum, ovl_ref[...])
    jl = lane - LANE_SEL
    tb = t_q >> 6
    in_range = (jl >= 0) & (jl < N_SEL_BLOCKS)
    valid = in_range & (jl <= tb)
    forced = (jl == 0) | (jl == tb) | (jl == tb - 1)
    score = jnp.where(forced & valid, FORCED_SCORE, imp)
    score = jnp.where(valid, score, -1.0)
    score = jnp.where(in_range, score, -2.0)
    n_above = jnp.zeros((tq, HEAD_PAD), F32)
    for j in range(N_SEL_BLOCKS):
        cj = score[:, LANE_SEL + j:LANE_SEL + j + 1]
        above = (cj > score) | ((cj == score) & (jl > j))
        n_above = n_above + above.astype(F32)
    dropped = in_range & (n_above >= float(SEL_TOPN))
    sel_feat = jnp.where(dropped, NEG_BIG, 0.0).astype(BF16)
    q4s = q4 + jnp.concatenate([sel_feat] * NSA_HPG, axis=0)

    def sel_tile(kt, carry, causal):
        m, l, acc = carry
        start = pl.multiple_of(kt * TK_SEL, TK_SEL)
        s = _dot_nt(q4s, ks_ref[pl.ds(start, TK_SEL), :])
        if causal:
            s_pos = start + lax.broadcasted_iota(I32, (1, TK_SEL), 1)
            s = jnp.where(t_row >= s_pos, s, NEG_BIG)
        m_new = jnp.maximum(m, jnp.max(s, axis=-1, keepdims=True))
        alpha = jnp.exp(m - m_new)
        p = jnp.exp(s - m_new)
        l = alpha * l + jnp.sum(p, axis=-1, keepdims=True)
        acc = alpha * acc + _dot(p.astype(BF16), vs_ref[pl.ds(start, TK_SEL), :])
        return m_new, l, acc

    n_full = t0 // TK_SEL
    init = (jnp.full((rows, 1), NEG_BIG, F32), jnp.zeros((rows, 1), F32), jnp.zeros((rows, HEAD_PAD), F32))
    carry = lax.fori_loop(0, n_full, lambda kt, c: sel_tile(kt, c, False), init)
    _, l, acc = sel_tile(n_full, carry, True)
    o_s = acc / l

    span = WINDOW + tq
    ws = pl.multiple_of(jnp.maximum(t0 - WINDOW, 0), tq)
    s = _dot_nt(q4, kw_ref[pl.ds(ws, span), :])
    dist = t_row - (ws + lax.broadcasted_iota(I32, (1, span), 1))
    mask = (dist >= 0) & (dist < WINDOW)
    s = jnp.where(mask, s, NEG_BIG)
    p = jnp.exp(s - jnp.max(s, axis=-1, keepdims=True))
    o_w = _dot(p.astype(BF16), vw_ref[pl.ds(ws, span), :]) / jnp.sum(p, axis=-1, keepdims=True)

    gates = gate_ref[...]
    for hh in range(NSA_HPG):
        rs = slice(hh * tq, (hh + 1) * tq)
        gc = jnp.where(g == 0, gates[:, hh:hh + 1], gates[:, NSA_HPG + hh:NSA_HPG + hh + 1])
        gs = jnp.where(g == 0, gates[:, 8 + hh:9 + hh], gates[:, 12 + hh:13 + hh])
        gw = jnp.where(g == 0, gates[:, 16 + hh:17 + hh], gates[:, 20 + hh:21 + hh])
        o = gc * o_c[rs] + gs * o_s[rs] + gw * o_w[rs]
        o_ref[:, hh * HEAD_PAD:(hh + 1) * HEAD_PAD] = o.astype(BF16)


def _overlap_matrix():
    c = jnp.arange(N_CMP_PAD)[:, None]
    lane = jnp.arange(HEAD_PAD)[None, :]
    j = lane - LANE_SEL
    lo = (CMP_STRIDE * c) // SEL_BLOCK
    hi = (CMP_STRIDE * c + CMP_BLOCK - 1) // SEL_BLOCK
    return (((j == lo) | (j == hi)) & (j >= 0) & (j < N_SEL_BLOCKS)).astype(BF16)


def _nsa(qn, ks, vs, kw, vw, kc, vc, gates):
    B, T, _ = qn.shape
    tq = TQ_NSA
    gw = NSA_HPG * HEAD_PAD
    q_spec = pl.BlockSpec((None, tq, gw), lambda b, g, i: (b, i, g))
    kv_spec = pl.BlockSpec((None, T, HEAD_PAD), lambda b, g, i: (b, 0, g))
    c_spec = pl.BlockSpec((None, None, N_CMP_PAD, HEAD_PAD), lambda b, g, i: (b, g, 0, 0))
    return pl.pallas_call(
        _nsa_kernel,
        grid=(B, NSA_GROUPS, T // tq),
        in_specs=[q_spec, kv_spec, kv_spec, kv_spec, kv_spec, c_spec, c_spec,
                  pl.BlockSpec((None, tq, 128), lambda b, g, i: (b, i, 0)),
                  pl.BlockSpec((N_CMP_PAD, HEAD_PAD), lambda b, g, i: (0, 0))],
        out_specs=q_spec,
        out_shape=jax.ShapeDtypeStruct((B, T, NSA_HEADS * HEAD_PAD), BF16),
        compiler_params=pltpu.CompilerParams(
            dimension_semantics=("arbitrary", "arbitrary", "arbitrary"), vmem_limit_bytes=VMEM_LIMIT),
        name="nsa",
    )(qn, ks, vs, kw, vw, kc, vc, gates, _overlap_matrix())


def _fox_kernel(q_ref, k_ref, v_ref, o_ref):
    tq = q_ref.shape[0]
    i = pl.program_id(2)
    q = q_ref[...]
    t_row = lax.broadcasted_iota(I32, (tq, 1), 0)
    s_col = lax.broadcasted_iota(I32, (1, tq), 1)

    def tile(kt, carry, causal):
        m, l, acc = carry
        start = pl.multiple_of(kt * tq, tq)
        s = _dot_nt(q, k_ref[pl.ds(start, tq), :])
        if causal:
            s = jnp.where(t_row >= s_col, s, NEG_BIG)
        m_new = jnp.maximum(m, jnp.max(s, axis=-1, keepdims=True))
        alpha = jnp.exp(m - m_new)
        p = jnp.exp(s - m_new)
        l = alpha * l + jnp.sum(p, axis=-1, keepdims=True)
        acc = alpha * acc + _dot(p.astype(BF16), v_ref[pl.ds(start, tq), :])
        return m_new, l, acc

    init = (jnp.full((tq, 1), NEG_BIG, F32), jnp.zeros((tq, 1), F32), jnp.zeros((tq, HEAD_PAD), F32))
    carry = lax.fori_loop(0, i, lambda kt, c: tile(kt, c, False), init)
    _, l, acc = tile(i, carry, True)
    o_ref[...] = (acc / l).astype(BF16)


def _fox(qf, kf, vf):
    B, T, _ = qf.shape
    tq = TQ_FOX
    q_spec = pl.BlockSpec((None, tq, HEAD_PAD), lambda b, h, i: (b, i, h))
    kv_spec = pl.BlockSpec((None, T, HEAD_PAD), lambda b, h, i: (b, 0, h))
    return pl.pallas_call(
        _fox_kernel,
        grid=(B, FOX_HEADS, T // tq),
        in_specs=[q_spec, kv_spec, kv_spec],
        out_specs=q_spec,
        out_shape=jax.ShapeDtypeStruct((B, T, FOX_HEADS * HEAD_PAD), BF16),
        compiler_params=pltpu.CompilerParams(
            dimension_semantics=("arbitrary", "arbitrary", "arbitrary"), vmem_limit_bytes=VMEM_LIMIT),
        name="fox",
    )(qf, kf, vf)


def _merge_out_kernel(x_ref, g_ref, on_ref, of_ref, wm_ref, bm_ref, wun_ref, wuf_ref, wo_ref, o_ref):
    x = x_ref[...]
    h = (_rms(x) * g_ref[...]).astype(BF16)
    merge = jax.nn.sigmoid(_dot(h, wm_ref[...]) + bm_ref[...])
    y = merge[:, :D_MODEL] * _dot(on_ref[...], wun_ref[...]) + merge[:, D_MODEL:] * _dot(of_ref[...], wuf_ref[...])
    o_ref[...] = x + _dot(y.astype(BF16), wo_ref[...])


def _merge_out(x2, g_attn, o_nsa, o_fox, w_merge, b_merge, w_up_nsa, w_up_fox, w_out):
    N, D = x2.shape
    tm = TM_OUT

    def pad_rows(w, n_heads):
        return _pad_heads(w.T, n_heads).T.astype(BF16)

    def tok(width):
        return pl.BlockSpec((tm, width), lambda i: (i, 0))

    def full(shape):
        return pl.BlockSpec(shape, lambda i: (0,) * len(shape))

    hw = NSA_HEADS * HEAD_PAD
    return pl.pallas_call(
        _merge_out_kernel,
        grid=(N // tm,),
        in_specs=[tok(D), full((1, D)), tok(hw), tok(hw), full((D, 2 * D)), full((1, 2 * D)),
                  full((hw, D)), full((hw, D)), full((D, D))],
        out_specs=tok(D),
        out_shape=jax.ShapeDtypeStruct((N, D), F32),
        compiler_params=pltpu.CompilerParams(dimension_semantics=("arbitrary",), vmem_limit_bytes=VMEM_LIMIT),
        name="merge_out",
    )(x2, g_attn.reshape(1, D), o_nsa, o_fox, w_merge.astype(BF16), b_merge.reshape(1, 2 * D),
      pad_rows(w_up_nsa, NSA_HEADS), pad_rows(w_up_fox, FOX_HEADS), w_out.astype(BF16))


def _router_kernel(x_ref, g_ref, wr_ref, br_ref, idx_ref, wgt_ref, rank_ref, cnt_ref, carry_ref):
    tm = x_ref.shape[0]

    @pl.when(pl.program_id(0) == 0)
    def _():
        carry_ref[...] = jnp.zeros_like(carry_ref)

    h = _rms(x_ref[...]) * g_ref[...]
    logits = jnp.dot(h, wr_ref[...], preferred_element_type=F32, precision=lax.Precision.HIGHEST) + br_ref[...]
    lane = lax.broadcasted_iota(I32, (1, 128), 1)
    lane_f = lane.astype(F32)
    work = jnp.where(lane < N_EXPERTS, logits, -jnp.inf)
    vals, hots = [], []
    idx_out = jnp.zeros((tm, 128), F32)
    for k in range(TOP_K):
        m = jnp.max(work, axis=-1, keepdims=True)
        idx = jnp.min(jnp.where(work == m, lane_f, 128.0), axis=-1, keepdims=True)
        hot = lane_f == idx
        work = jnp.where(hot, -jnp.inf, work)
        vals.append(m)
        hots.append(hot)
        idx_out = idx_out + jnp.where(lane == k, idx, 0.0)
    e = [jnp.exp(v - vals[0]) for v in vals]
    denom = e[0] + e[1] + e[2] + e[3]
    wgt = jnp.zeros((tm, 128), F32)
    for k in range(TOP_K):
        wgt = wgt + jnp.where(lane == k, e[k] / denom, 0.0)

    multi = (hots[0] | hots[1] | hots[2] | hots[3]).astype(F32)
    r = lax.broadcasted_iota(I32, (tm, tm), 0)
    c = lax.broadcasted_iota(I32, (tm, tm), 1)
    before = _dot((r > c).astype(BF16), multi.astype(BF16)) + carry_ref[...]
    rank = jnp.zeros((tm, 128), F32)
    for k in range(TOP_K):
        rk = jnp.sum(jnp.where(hots[k], before, 0.0), axis=-1, keepdims=True)
        rank = rank + jnp.where(lane == k, rk, 0.0)
    total = carry_ref[...] + jnp.sum(multi, axis=0, keepdims=True)
    carry_ref[...] = total
    idx_ref[...] = idx_out.astype(I32)
    wgt_ref[...] = wgt
    rank_ref[...] = rank.astype(I32)
    cnt_ref[...] = total.astype(I32)


def _router(x1, g_mlp, w_router, b_router):
    N, D = x1.shape
    tm = TM_ROUTE
    wr = jnp.pad(w_router, ((0, 0), (0, 128 - N_EXPERTS)))
    br = jnp.pad(b_router, (0, 128 - N_EXPERTS)).reshape(1, 128)
    tok = pl.BlockSpec((tm, 128), lambda i: (i, 0))
    one = pl.BlockSpec((1, 128), lambda i: (0, 0))
    return pl.pallas_call(
        _router_kernel,
        grid=(N // tm,),
        in_specs=[pl.BlockSpec((tm, D), lambda i: (i, 0)), pl.BlockSpec((1, D), lambda i: (0, 0)),
                  pl.BlockSpec((D, 128), lambda i: (0, 0)), one],
        out_specs=[tok, tok, tok, one],
        out_shape=[jax.ShapeDtypeStruct((N, 128), I32), jax.ShapeDtypeStruct((N, 128), F32),
                   jax.ShapeDtypeStruct((N, 128), I32), jax.ShapeDtypeStruct((1, 128), I32)],
        scratch_shapes=[pltpu.VMEM((1, 128), F32)],
        compiler_params=pltpu.CompilerParams(dimension_semantics=("arbitrary",), vmem_limit_bytes=VMEM_LIMIT),
        name="router",
    )(x1, g_mlp.reshape(1, D), wr, br)


def _row_copy(src_ref, src_row, dst_ref, dst_row, sem):
    return pltpu.make_async_copy(src_ref.at[pl.ds(src_row, 1), :], dst_ref.at[pl.ds(dst_row, 1), :], sem)


def _dispatch_kernel(pos_ref, x_ref, xs_in_ref, xs_ref, sem):
    del xs_in_ref
    tm = x_ref.shape[0]

    def issue(r, carry):
        for k in range(TOP_K):
            _row_copy(x_ref, r, xs_ref, pos_ref[r * TOP_K + k], sem).start()
        return carry

    lax.fori_loop(0, tm, issue, 0)

    def drain(r, carry):
        for k in range(TOP_K):
            _row_copy(x_ref, 0, xs_ref, 0, sem).wait()
        return carry

    lax.fori_loop(0, tm, drain, 0)


def _dispatch(x1, pos_flat, n_rows):
    N, D = x1.shape
    tm = TM_MOVE
    return pl.pallas_call(
        _dispatch_kernel,
        grid=(N // tm,),
        in_specs=[pl.BlockSpec((tm * TOP_K,), lambda i: (i,), memory_space=pltpu.SMEM),
                  pl.BlockSpec((tm, D), lambda i: (i, 0)),
                  pl.BlockSpec(memory_space=pl.ANY)],
        out_specs=pl.BlockSpec(memory_space=pl.ANY),
        out_shape=jax.ShapeDtypeStruct((n_rows, D), F32),
        scratch_shapes=[pltpu.SemaphoreType.DMA],
        input_output_aliases={2: 0},
        compiler_params=pltpu.CompilerParams(dimension_semantics=("arbitrary",), vmem_limit_bytes=VMEM_LIMIT),
        name="dispatch",
    )(pos_flat, x1, jnp.zeros((n_rows, D), F32))


def _experts_kernel(te_ref, nv_ref, xs_ref, g_ref, wg_ref, bg_ref, wu_ref, bu_ref, wd_ref, bd_ref, ys_ref):
    i = pl.program_id(0)

    @pl.when(i < nv_ref[0])
    def _():
        h = (_rms(xs_ref[...]) * g_ref[...]).astype(BF16)
        gate = jnp.minimum(_dot(h, wg_ref[...]) + bg_ref[...], SWIGLU_LIMIT)
        up = jnp.clip(_dot(h, wu_ref[...]) + bu_ref[...], -SWIGLU_LIMIT, SWIGLU_LIMIT)
        act = (up + 1.0) * gate * jax.nn.sigmoid(SWIGLU_ALPHA * gate)
        ys_ref[...] = _dot(act.astype(BF16), wd_ref[...]) + bd_ref[...]

    @pl.when(i >= nv_ref[0])
    def _():
        ys_ref[...] = jnp.zeros_like(ys_ref)


def _experts(xs, tile_expert, n_valid, g_mlp, w_gate, b_gate, w_up, b_up, w_down, b_down):
    P, D = xs.shape
    tm = TM_EXPERT
    E, _, F = w_gate.shape
    w_spec = pl.BlockSpec((None, D, F), lambda i, te, nv: (te[i], 0, 0))
    wd_spec = pl.BlockSpec((None, F, D), lambda i, te, nv: (te[i], 0, 0))
    b_spec = pl.BlockSpec((None, 1, F), lambda i, te, nv: (te[i], 0, 0))
    bd_spec = pl.BlockSpec((None, 1, D), lambda i, te, nv: (te[i], 0, 0))
    tok = pl.BlockSpec((tm, D), lambda i, te, nv: (i, 0))
    return pl.pallas_call(
        _experts_kernel,
        grid_spec=pltpu.PrefetchScalarGridSpec(
            num_scalar_prefetch=2,
            grid=(P // tm,),
            in_specs=[tok, pl.BlockSpec((1, D), lambda i, te, nv: (0, 0)),
                      w_spec, b_spec, w_spec, b_spec, wd_spec, bd_spec],
            out_specs=tok),
        out_shape=jax.ShapeDtypeStruct((P, D), F32),
        compiler_params=pltpu.CompilerParams(dimension_semantics=("arbitrary",), vmem_limit_bytes=VMEM_LIMIT),
        name="experts",
    )(tile_expert, n_valid, xs, g_mlp.reshape(1, D),
      w_gate.astype(BF16), b_gate.reshape(E, 1, F), w_up.astype(BF16), b_up.reshape(E, 1, F),
      w_down.astype(BF16), b_down.reshape(E, 1, D))


def _combine_kernel(pos_ref, x_ref, wgt_ref, ys_ref, o_ref, buf_ref, sem):
    tm = x_ref.shape[0]

    def issue(r, carry):
        for k in range(TOP_K):
            _row_copy(ys_ref, pos_ref[r * TOP_K + k], buf_ref.at[k], r, sem).start()
        return carry

    lax.fori_loop(0, tm, issue, 0)

    def drain(r, carry):
        for k in range(TOP_K):
            _row_copy(ys_ref, 0, buf_ref.at[k], 0, sem).wait()
        return carry

    lax.fori_loop(0, tm, drain, 0)
    wgt = wgt_ref[...]
    acc = x_ref[...]
    for k in range(TOP_K):
        acc = acc + wgt[:, k:k + 1] * buf_ref[k]
    o_ref[...] = acc


def _combine(x1, wgt, pos_flat, ys):
    N, D = x1.shape
    tm = TM_MOVE
    return pl.pallas_call(
        _combine_kernel,
        grid=(N // tm,),
        in_specs=[pl.BlockSpec((tm * TOP_K,), lambda i: (i,), memory_space=pltpu.SMEM),
                  pl.BlockSpec((tm, D), lambda i: (i, 0)),
                  pl.BlockSpec((tm, 128), lambda i: (i, 0)),
                  pl.BlockSpec(memory_space=pl.ANY)],
        out_specs=pl.BlockSpec((tm, D), lambda i: (i, 0)),
        out_shape=jax.ShapeDtypeStruct((N, D), F32),
        scratch_shapes=[pltpu.VMEM((TOP_K, tm, D), F32), pltpu.SemaphoreType.DMA],
        compiler_params=pltpu.CompilerParams(dimension_semantics=("arbitrary",), vmem_limit_bytes=VMEM_LIMIT),
        name="combine",
    )(pos_flat, x1, wgt, ys)


def _moe(x1, g_mlp, w_router, b_router, w_gate, b_gate, w_up, b_up, w_down, b_down):
    N, D = x1.shape
    idx, wgt, rank, cnt = _router(x1, g_mlp, w_router, b_router)
    tm = TM_EXPERT
    n_tiles = -(-(N * TOP_K + N_EXPERTS * (tm - 1)) // tm)
    counts = cnt[0, :N_EXPERTS]
    padded = ((counts + tm - 1) // tm) * tm
    ends = jnp.cumsum(padded)
    starts = ends - padded
    pos = (starts[idx[:, :TOP_K]] + rank[:, :TOP_K]).reshape(-1).astype(I32)
    tile_start = jnp.arange(n_tiles, dtype=I32) * tm
    tile_expert = jnp.minimum(jnp.searchsorted(ends, tile_start, side="right"), N_EXPERTS - 1).astype(I32)
    n_valid = (ends[-1] // tm).astype(I32).reshape(1)
    xs = _dispatch(x1, pos, n_tiles * tm)
    ys = _experts(xs, tile_expert, n_valid, g_mlp, w_gate, b_gate, w_up, b_up, w_down, b_down)
    return _combine(x1, wgt, pos, ys)


def _layer(x, g_attn, w_in, qn_nsa, kn_cmp, kn_sel, kn_win,
           pos_cmp_k, w_ck1, b_ck1, w_ck2, b_ck2, pos_cmp_v, w_cv1, b_cv1, w_cv2, b_cv2,
           b_nsa_gate, qn_fox, kn_fox, b_forget, w_up_nsa, w_up_fox, b_merge, w_out, g_mlp,
           w_router, b_router, w_e_gate, b_e_gate, w_e_up, b_e_up, w_e_down, b_e_down):
    B, T, D = x.shape
    qn, ks, vs, kw, vw, kc_raw, vc_raw, qf, kf, vf, gates = _in_proj(
        x, g_attn, w_in, qn_nsa, kn_sel, kn_win, b_nsa_gate, qn_fox, kn_fox, b_forget)
    kc, vc = _compress(kc_raw, vc_raw, pos_cmp_k, w_ck1, b_ck1, w_ck2, b_ck2,
                       pos_cmp_v, w_cv1, b_cv1, w_cv2, b_cv2, kn_cmp)
    o_nsa = _nsa(qn, ks, vs, kw, vw, kc, vc, gates)
    o_fox = _fox(qf, kf, vf)
    hw = NSA_HEADS * HEAD_PAD
    w_merge = w_in[:, w_in.shape[1] - 2 * D:]
    x1 = _merge_out(x.reshape(B * T, D), g_attn, o_nsa.reshape(B * T, hw), o_fox.reshape(B * T, hw),
                    w_merge, b_merge, w_up_nsa, w_up_fox, w_out)
    out = _moe(x1, g_mlp, w_router, b_router, w_e_gate, b_e_gate, w_e_up, b_e_up, w_e_down, b_e_down)
    return out.reshape(B, T, D)


def kernel(x, g_attn, w_in, qn_nsa, kn_cmp, kn_sel, kn_win, pos_cmp_k, w_ck1, b_ck1, w_ck2, b_ck2, pos_cmp_v, w_cv1, b_cv1, w_cv2, b_cv2, b_nsa_gate, qn_fox, kn_fox, b_forget, w_up_nsa, w_up_fox, b_merge, w_out, g_mlp, w_router, b_router, w_e_gate, b_e_gate, w_e_up, b_e_up, w_e_down, b_e_down):
    params = (g_attn, w_in, qn_nsa, kn_cmp, kn_sel, kn_win, pos_cmp_k, w_ck1, b_ck1, w_ck2, b_ck2,
              pos_cmp_v, w_cv1, b_cv1, w_cv2, b_cv2, b_nsa_gate, qn_fox, kn_fox, b_forget,
              w_up_nsa, w_up_fox, b_merge, w_out, g_mlp, w_router, b_router,
              w_e_gate, b_e_gate, w_e_up, b_e_up, w_e_down, b_e_down)
    for layer in range(g_attn.shape[0]):
        x = _layer(x, *[p[layer] for p in params])
    return x
```

```python
import functools

import jax
import jax.numpy as jnp
from jax import lax
from jax.experimental import pallas as pl
from jax.experimental.pallas import tpu as pltpu

F32 = jnp.float32
BF16 = jnp.bfloat16
I32 = jnp.int32

D_MODEL = 1024
HEAD_DIM = 64
HEAD_PAD = 128
NSA_HEADS = 8
NSA_GROUPS = 2
NSA_HPG = NSA_HEADS // NSA_GROUPS
FOX_HEADS = 8
CMP_BLOCK = 32
CMP_STRIDE = 16
CMP_HIDDEN = 256
N_CMP_PAD = 128
SEL_BLOCK = 64
SEL_TOPN = 16
N_SEL_BLOCKS = 32
WINDOW = 512
N_EXPERTS = 32
TOP_K = 4
SWIGLU_LIMIT = 7.0
SWIGLU_ALPHA = 1.702
RMS_EPS = 1e-6
NEG_BIG = -1e30
FORCED_SCORE = 1e9
QK_SCALE = HEAD_DIM ** -0.5

LANE_SEL = 64
LANE_POS = 96
LANE_FOX = 64

OFF_QN = 0
OFF_KS = OFF_QN + NSA_HEADS * HEAD_PAD
OFF_VS = OFF_KS + NSA_GROUPS * HEAD_PAD
OFF_KW = OFF_VS + NSA_GROUPS * HEAD_PAD
OFF_VW = OFF_KW + NSA_GROUPS * HEAD_PAD
OFF_KC = OFF_VW + NSA_GROUPS * HEAD_PAD
OFF_VC = OFF_KC + NSA_GROUPS * HEAD_DIM
OFF_QF = OFF_VC + NSA_GROUPS * HEAD_DIM
OFF_KF = OFF_QF + FOX_HEADS * HEAD_PAD
OFF_VF = OFF_KF + FOX_HEADS * HEAD_PAD
OFF_MISC = OFF_VF + FOX_HEADS * HEAD_PAD
W1_COLS = OFF_MISC + 128
MISC_GATES = 3 * NSA_HEADS

TM_PROJ = 256
TQ_NSA = 128
CK_ATTN = 256
TM_OUT = 256
TM_ROUTE = 256
TM_EXPERT = 512
TM_MOVE = 256
VMEM_LIMIT = 56 * 1024 * 1024


def _dot(a, b):
    return jnp.dot(a, b, preferred_element_type=F32)


def _dot_nt(a, b):
    return lax.dot_general(a, b, (((1,), (1,)), ((), ())), preferred_element_type=F32)


def _split3(v):
    hi = v.astype(BF16)
    r1 = v - hi.astype(F32)
    mid = r1.astype(BF16)
    lo = (r1 - mid.astype(F32)).astype(BF16)
    return hi, mid, lo


def _dot_exact_rhs(a_f32, b_bf16):
    hi, mid, lo = _split3(a_f32)
    return _dot(hi, b_bf16) + _dot(mid, b_bf16) + _dot(lo, b_bf16)


def _dot_exact_lhs(a_bf16, b_f32):
    hi, mid, lo = _split3(b_f32)
    return _dot(a_bf16, hi) + _dot(a_bf16, mid) + _dot(a_bf16, lo)


def _rms(x):
    return x * lax.rsqrt(jnp.mean(x * x, axis=-1, keepdims=True) + RMS_EPS)


def _head_rms(a, gain):
    ms = jnp.sum(a * a, axis=-1, keepdims=True) * (1.0 / HEAD_DIM)
    return a * lax.rsqrt(ms + RMS_EPS) * gain


def _pos_feat_k(pos, lane):
    hi = (256 * (pos >> 8)).astype(F32)
    lo = (pos & 255).astype(F32)
    return jnp.where(lane < LANE_POS + 2, 1.0, jnp.where(lane == LANE_POS + 2, hi, lo)) * (
        (lane >= LANE_POS) & (lane < LANE_POS + 4)).astype(F32)


def _in_proj_kernel(x_ref, g_ref, w_ref, gains_ref, bmisc_ref, pq_ref, pk_ref,
                    qn_ref, ks_ref, vs_ref, kw_ref, vw_ref, kc_ref, vc_ref,
                    qf_ref, kf_ref, vf_ref, gate_ref, carry_ref):
    tm = x_ref.shape[0]
    i = pl.program_id(1)
    h = (_rms(x_ref[...]) * g_ref[...]).astype(BF16)
    pos = i * tm + lax.broadcasted_iota(I32, (tm, 1), 0)
    lane = lax.broadcasted_iota(I32, (1, HEAD_PAD), 1)
    pos_hi = (256 * (pos >> 8)).astype(F32)
    pos_lo = (pos & 255).astype(F32)
    kfeat = _pos_feat_k(pos, lane)
    blk_onehot = ((lane - LANE_SEL) == (pos >> 6)).astype(F32)

    def seg(off, width):
        return _dot(h, w_ref[:, off:off + width])

    a = seg(OFF_QN, NSA_HEADS * HEAD_PAD)
    gq = gains_ref[0:1, :] * QK_SCALE
    for hd in range(NSA_HEADS):
        slope = 2.0 ** (-(hd + 1))
        qfeat = jnp.where(lane == LANE_POS, -slope * pos_hi,
                          jnp.where(lane == LANE_POS + 1, -slope * pos_lo,
                                    jnp.where((lane == LANE_POS + 2) | (lane == LANE_POS + 3), slope, 0.0)))
        sl = slice(hd * HEAD_PAD, (hd + 1) * HEAD_PAD)
        qn_ref[:, sl] = (_head_rms(a[:, sl], gq) + qfeat).astype(BF16)

    a = seg(OFF_KS, NSA_GROUPS * HEAD_PAD)
    for g in range(NSA_GROUPS):
        sl = slice(g * HEAD_PAD, (g + 1) * HEAD_PAD)
        ks_ref[:, sl] = (_head_rms(a[:, sl], gains_ref[1:2, :]) + kfeat + blk_onehot).astype(BF16)
    vs_ref[...] = seg(OFF_VS, NSA_GROUPS * HEAD_PAD).astype(BF16)
    a = seg(OFF_KW, NSA_GROUPS * HEAD_PAD)
    for g in range(NSA_GROUPS):
        sl = slice(g * HEAD_PAD, (g + 1) * HEAD_PAD)
        kw_ref[:, sl] = (_head_rms(a[:, sl], gains_ref[2:3, :]) + kfeat).astype(BF16)
    vw_ref[...] = seg(OFF_VW, NSA_GROUPS * HEAD_PAD).astype(BF16)
    kc_ref[...] = seg(OFF_KC, NSA_GROUPS * HEAD_DIM).astype(BF16)
    vc_ref[...] = seg(OFF_VC, NSA_GROUPS * HEAD_DIM).astype(BF16)

    misc = seg(OFF_MISC, 128) + bmisc_ref[...]
    is_gate = lane < MISC_GATES
    gate_ref[...] = jnp.where(is_gate, jax.nn.sigmoid(misc), 0.0)
    is_f = (lane >= MISC_GATES) & (lane < MISC_GATES + 3 * FOX_HEADS)
    logf = jnp.where(is_f, jax.nn.log_sigmoid(misc), 0.0)

    @pl.when(i == 0)
    def _():
        carry_ref[...] = jnp.zeros_like(carry_ref)

    r = lax.broadcasted_iota(I32, (tm, tm), 0)
    c = lax.broadcasted_iota(I32, (tm, tm), 1)
    tri = (r >= c).astype(BF16)
    csum = _dot_exact_lhs(tri, logf) + carry_ref[...]
    carry_ref[...] = csum[tm - 1:tm, :]
    c_hi, c_mid, c_lo = _split3(csum)
    c3 = jnp.where(lane < MISC_GATES + FOX_HEADS, c_hi,
                   jnp.where(lane < MISC_GATES + 2 * FOX_HEADS, c_mid, c_lo))
    c3 = jnp.where(is_f, c3, jnp.zeros_like(c3))
    ones_lo = ((lane >= LANE_FOX) & (lane < LANE_FOX + 3)).astype(F32)
    ones_hi = ((lane >= LANE_FOX + 3) & (lane < LANE_FOX + 6)).astype(F32)

    a = seg(OFF_QF, FOX_HEADS * HEAD_PAD)
    fq = _dot(c3, pq_ref[...])
    gq = gains_ref[3:4, :] * QK_SCALE
    for hd in range(FOX_HEADS):
        sl = slice(hd * HEAD_PAD, (hd + 1) * HEAD_PAD)
        qf_ref[:, sl] = (_head_rms(a[:, sl], gq) + fq[:, sl] + ones_hi).astype(BF16)
    a = seg(OFF_KF, FOX_HEADS * HEAD_PAD)
    fk = _dot(c3, pk_ref[...])
    for hd in range(FOX_HEADS):
        sl = slice(hd * HEAD_PAD, (hd + 1) * HEAD_PAD)
        kf_ref[:, sl] = (_head_rms(a[:, sl], gains_ref[4:5, :]) + fk[:, sl] + ones_lo).astype(BF16)
    vf_ref[...] = seg(OFF_VF, FOX_HEADS * HEAD_PAD).astype(BF16)


def _pad_heads(w, n_heads):
    lead = w.shape[:-1]
    w = w.reshape(lead + (n_heads, HEAD_DIM))
    w = jnp.pad(w, [(0, 0)] * len(lead) + [(0, 0), (0, HEAD_PAD - HEAD_DIM)])
    return w.reshape(lead + (n_heads * HEAD_PAD,))


def _pad_gain(g):
    return jnp.pad(g.astype(F32), (0, HEAD_PAD - HEAD_DIM))


def _fox_placement():
    pq = jnp.zeros((128, FOX_HEADS * HEAD_PAD), F32)
    pk = jnp.zeros((128, FOX_HEADS * HEAD_PAD), F32)
    for piece in range(3):
        for hd in range(FOX_HEADS):
            src = MISC_GATES + piece * FOX_HEADS + hd
            pq = pq.at[src, hd * HEAD_PAD + LANE_FOX + piece].set(1.0)
            pk = pk.at[src, hd * HEAD_PAD + LANE_FOX + 3 + piece].set(-1.0)
    return pq.astype(BF16), pk.astype(BF16)


def _in_proj(x, g_attn, w_in, qn_nsa, kn_sel, kn_win, b_nsa_gate, qn_fox, kn_fox, b_forget):
    B, T, D = x.shape
    s1 = NSA_HEADS * HEAD_DIM
    kv_w = NSA_GROUPS * HEAD_DIM
    s2 = s1 + 6 * kv_w
    s3 = s2 + 3 * NSA_HEADS
    fw = FOX_HEADS * HEAD_DIM
    s4 = s3 + 3 * fw
    s5 = s4 + FOX_HEADS
    kv = [w_in[:, s1 + j * kv_w: s1 + (j + 1) * kv_w] for j in range(6)]
    f_cols = w_in[:, s4:s5]
    misc = jnp.concatenate([w_in[:, s2:s3], f_cols, f_cols, f_cols,
                            jnp.zeros((D, 128 - MISC_GATES - 3 * FOX_HEADS), F32)], axis=1)
    w1 = jnp.concatenate([
        _pad_heads(w_in[:, :s1], NSA_HEADS),
        _pad_heads(kv[2], NSA_GROUPS), _pad_heads(kv[3], NSA_GROUPS),
        _pad_heads(kv[4], NSA_GROUPS), _pad_heads(kv[5], NSA_GROUPS),
        kv[0], kv[1],
        _pad_heads(w_in[:, s3:s3 + fw], FOX_HEADS),
        _pad_heads(w_in[:, s3 + fw:s3 + 2 * fw], FOX_HEADS),
        _pad_heads(w_in[:, s3 + 2 * fw:s4], FOX_HEADS),
        misc], axis=1).astype(BF16)
    assert w1.shape[1] == W1_COLS
    gains = jnp.stack([_pad_gain(qn_nsa), _pad_gain(kn_sel), _pad_gain(kn_win),
                       _pad_gain(qn_fox), _pad_gain(kn_fox),
                       jnp.zeros(HEAD_PAD, F32), jnp.zeros(HEAD_PAD, F32), jnp.zeros(HEAD_PAD, F32)])
    bmisc = jnp.concatenate([b_nsa_gate, b_forget, b_forget, b_forget,
                             jnp.zeros(128 - MISC_GATES - 3 * FOX_HEADS, F32)]).reshape(1, 128)
    pq, pk = _fox_placement()
    tm = TM_PROJ

    def tok(width):
        return pl.BlockSpec((None, tm, width), lambda b, i: (b, i, 0))

    def full(shape):
        return pl.BlockSpec(shape, lambda b, i: (0,) * len(shape))

    widths = [NSA_HEADS * HEAD_PAD] + [NSA_GROUPS * HEAD_PAD] * 4 + [NSA_GROUPS * HEAD_DIM] * 2 + \
             [FOX_HEADS * HEAD_PAD] * 3
    out_shape = [jax.ShapeDtypeStruct((B, T, w), BF16) for w in widths] + \
                [jax.ShapeDtypeStruct((B, T, 128), F32)]
    return pl.pallas_call(
        _in_proj_kernel,
        grid=(B, T // tm),
        in_specs=[tok(D), full((1, D)), full((D, W1_COLS)), full((8, HEAD_PAD)), full((1, 128)),
                  full((128, FOX_HEADS * HEAD_PAD)), full((128, FOX_HEADS * HEAD_PAD))],
        out_specs=[tok(w) for w in widths] + [tok(128)],
        out_shape=out_shape,
        scratch_shapes=[pltpu.VMEM((1, 128), F32)],
        compiler_params=pltpu.CompilerParams(
            dimension_semantics=("arbitrary", "arbitrary"), vmem_limit_bytes=VMEM_LIMIT),
        name="in_proj",
    )(x, g_attn.reshape(1, D), w1, gains, bmisc, pq, pk)


def _gelu_tanh(x):
    return 0.5 * x * (1.0 + jnp.tanh(0.7978845608028654 * (x + 0.044715 * (x * x * x))))


def _compress_kernel(ak_ref, av_ref, posk_ref, posv_ref, wk1_ref, bk1_ref, wk2_ref, bk2_ref,
                     wv1_ref, bv1_ref, wv2_ref, bv2_ref, gain_ref, kc_ref, vc_ref):
    half = (CMP_BLOCK // 2) * HEAD_DIM
    row = lax.broadcasted_iota(I32, (N_CMP_PAD, 1), 0)
    lane = lax.broadcasted_iota(I32, (1, HEAD_PAD), 1)
    kfeat = _pos_feat_k(CMP_STRIDE * row + CMP_BLOCK - 1, lane)

    def mlp(a, pos_ref, w1_ref, b1_ref, w2_ref, b2_ref):
        a = a.astype(F32)
        lo = (a + pos_ref[0:1, :]).astype(BF16)
        hi = (a + pos_ref[1:2, :]).astype(BF16)
        p_lo = _dot(lo, w1_ref[0:half, :])
        p_hi = _dot(hi, w1_ref[half:2 * half, :])
        hid = p_lo + pltpu.roll(p_hi, N_CMP_PAD - 1, 0) + b1_ref[...]
        return _dot(_gelu_tanh(hid).astype(BF16), w2_ref[...]) + b2_ref[...]

    for g in range(NSA_GROUPS):
        k = mlp(ak_ref[g], posk_ref, wk1_ref, bk1_ref, wk2_ref, bk2_ref)
        kc_ref[g] = (_head_rms(k, gain_ref[...]) + kfeat).astype(BF16)
        vc_ref[g] = mlp(av_ref[g], posv_ref, wv1_ref, bv1_ref, wv2_ref, bv2_ref).astype(BF16)


def _compress(kc_raw, vc_raw, pos_k, w_ck1, b_ck1, w_ck2, b_ck2, pos_v, w_cv1, b_cv1, w_cv2, b_cv2, kn_cmp):
    B, T, _ = kc_raw.shape
    n_chunks = T // CMP_STRIDE
    assert n_chunks == N_CMP_PAD
    chunk_w = CMP_STRIDE * HEAD_DIM

    def chunks(a):
        a = a.reshape(B, n_chunks, CMP_STRIDE, NSA_GROUPS, HEAD_DIM)
        return a.transpose(0, 3, 1, 2, 4).reshape(B, NSA_GROUPS, n_chunks, chunk_w)

    def w2pad(w, b):
        return (jnp.pad(w, ((0, 0), (0, HEAD_PAD - HEAD_DIM))).astype(BF16),
                jnp.pad(b, (0, HEAD_PAD - HEAD_DIM)).reshape(1, HEAD_PAD))

    wk2, bk2 = w2pad(w_ck2, b_ck2)
    wv2, bv2 = w2pad(w_cv2, b_cv2)
    a_spec = pl.BlockSpec((None, NSA_GROUPS, n_chunks, chunk_w), lambda b: (b, 0, 0, 0))
    o_spec = pl.BlockSpec((None, NSA_GROUPS, N_CMP_PAD, HEAD_PAD), lambda b: (b, 0, 0, 0))

    def full(shape):
        return pl.BlockSpec(shape, lambda b: (0,) * len(shape))

    hid2 = CMP_BLOCK * HEAD_DIM
    return pl.pallas_call(
        _compress_kernel,
        grid=(B,),
        in_specs=[a_spec, a_spec, full((2, chunk_w)), full((2, chunk_w)),
                  full((hid2, CMP_HIDDEN)), full((1, CMP_HIDDEN)), full((CMP_HIDDEN, HEAD_PAD)), full((1, HEAD_PAD)),
                  full((hid2, CMP_HIDDEN)), full((1, CMP_HIDDEN)), full((CMP_HIDDEN, HEAD_PAD)), full((1, HEAD_PAD)),
                  full((1, HEAD_PAD))],
        out_specs=[o_spec, o_spec],
        out_shape=[jax.ShapeDtypeStruct((B, NSA_GROUPS, N_CMP_PAD, HEAD_PAD), BF16)] * 2,
        compiler_params=pltpu.CompilerParams(dimension_semantics=("arbitrary",), vmem_limit_bytes=VMEM_LIMIT),
        name="compress",
    )(chunks(kc_raw), chunks(vc_raw), pos_k.reshape(2, chunk_w), pos_v.reshape(2, chunk_w),
      w_ck1.astype(BF16), b_ck1.reshape(1, -1), wk2, bk2,
      w_cv1.astype(BF16), b_cv1.reshape(1, -1), wv2, bv2, _pad_gain(kn_cmp).reshape(1, HEAD_PAD))


def _lane_blocks(a):
    return [a[:, i * 128:(i + 1) * 128] for i in range(a.shape[1] // 128)]


def _fold_blocks(op, acc, a):
    for blk in _lane_blocks(a):
        acc = blk if acc is None else op(acc, blk)
    return acc


def _nsa_kernel(q_ref, ks_ref, vs_ref, kw_ref, vw_ref, kc_ref, vc_ref, gate_ref, ovl_ref, o_ref):
    seq = q_ref.shape[0]
    tq = TQ_NSA
    ck = CK_ATTN
    rows = NSA_HPG * tq
    span = WINDOW + tq
    g = pl.program_id(1)
    row_l = lax.broadcasted_iota(I32, (rows, 1), 0) & (tq - 1)
    col_l = lax.broadcasted_iota(I32, (1, 128), 1)
    c_end_lane = CMP_STRIDE * col_l + (CMP_BLOCK - 1)
    c_end_sub = CMP_STRIDE * lax.broadcasted_iota(I32, (N_CMP_PAD, 1), 0) + (CMP_BLOCK - 1)
    blk_id = lax.broadcasted_iota(I32, (N_SEL_BLOCKS, 1), 0)

    def q_tile(t0, n_chunks, early):
        qt = q_ref[pl.ds(t0, tq), :]
        heads = [qt[:, hh * HEAD_PAD:(hh + 1) * HEAD_PAD] for hh in range(NSA_HPG)]
        q4 = jnp.concatenate(heads, axis=0)
        t_row = t0 + row_l
        t_lane = t0 + lax.broadcasted_iota(I32, (1, tq), 1)

        s = _dot_nt(q4, kc_ref[...])
        mask = t_row >= c_end_lane
        s = jnp.where(mask, s, NEG_BIG)
        p = jnp.where(mask, jnp.exp(s - jnp.max(s, axis=-1, keepdims=True)), 0.0)
        l = jnp.sum(p, axis=-1, keepdims=True)
        o_c = _dot((p / jnp.where(l > 0.0, l, 1.0)).astype(BF16), vc_ref[...])

        mask_t = t_lane >= c_end_sub
        p_sum = None
        for hh in range(NSA_HPG):
            s_t = jnp.where(mask_t, _dot_nt(kc_ref[...], heads[hh]), NEG_BIG)
            e_t = jnp.where(mask_t, jnp.exp(s_t - jnp.max(s_t, axis=0, keepdims=True)), 0.0)
            l_t = jnp.sum(e_t, axis=0, keepdims=True)
            p_t = e_t / jnp.where(l_t > 0.0, l_t, 1.0)
            p_sum = p_t if p_sum is None else p_sum + p_t
        imp = _dot_exact_lhs(ovl_ref[...], p_sum)
        tb = t_lane >> 6
        valid = blk_id <= tb
        forced = (blk_id == 0) | (blk_id == tb) | (blk_id == tb - 1)
        score = jnp.where(forced & valid, FORCED_SCORE, imp)
        score = jnp.where(valid, score, -1.0)
        n_above = jnp.zeros((N_SEL_BLOCKS, tq), F32)
        for j in range(N_SEL_BLOCKS):
            cj = score[j:j + 1, :]
            gt = jnp.where(cj > score, 1.0, 0.0)
            eq = jnp.where(cj == score, 1.0, 0.0)
            n_above = n_above + gt + eq * (blk_id > j).astype(F32)
        dropped = jnp.where(n_above >= float(SEL_TOPN), NEG_BIG, 0.0)
        drop_t = jnp.concatenate([jnp.zeros((LANE_SEL, tq), F32), dropped,
                                  jnp.zeros((HEAD_PAD - LANE_SEL - N_SEL_BLOCKS, tq), F32)], axis=0)
        sel_feat = drop_t.T.astype(BF16)
        q4s = q4 + jnp.concatenate([sel_feat] * NSA_HPG, axis=0)

        s_chunks = []
        mx = None
        for c in range(n_chunks):
            s = _dot_nt(q4s, ks_ref[c * ck:(c + 1) * ck, :])
            if c == n_chunks - 1:
                s = jnp.where(t_row >= c * ck + lax.broadcasted_iota(I32, (1, ck), 1), s, NEG_BIG)
            s_chunks.append(s)
            mx = _fold_blocks(jnp.maximum, mx, s)
        m = jnp.max(mx, axis=-1, keepdims=True)
        ls = None
        acc = None
        for c in range(n_chunks):
            p = jnp.exp(s_chunks[c] - m)
            ls = _fold_blocks(jnp.add, ls, p)
            pv = _dot(p.astype(BF16), vs_ref[c * ck:(c + 1) * ck, :])
            acc = pv if acc is None else acc + pv
        o_s = acc / jnp.sum(ls, axis=-1, keepdims=True)

        if early:
            kw = kw_ref[0:span, :]
            vw = vw_ref[0:span, :]
            s = _dot_nt(q4, kw)
            s = jnp.where(t_row >= lax.broadcasted_iota(I32, (1, span), 1), s, NEG_BIG)
        else:
            ws = pl.multiple_of(t0 - WINDOW, tq)
            kw = kw_ref[pl.ds(ws, span), :]
            vw = vw_ref[pl.ds(ws, span), :]
            blocks = _lane_blocks(_dot_nt(q4, kw))
            blocks[0] = jnp.where(col_l > row_l, blocks[0], NEG_BIG)
            blocks[-1] = jnp.where(col_l <= row_l, blocks[-1], NEG_BIG)
            s = jnp.concatenate(blocks, axis=1)
        m = jnp.max(_fold_blocks(jnp.maximum, None, s), axis=-1, keepdims=True)
        p = jnp.exp(s - m)
        l = jnp.sum(_fold_blocks(jnp.add, None, p), axis=-1, keepdims=True)
        o_w = _dot(p.astype(BF16), vw) / l

        gates = gate_ref[pl.ds(t0, tq), :]
        outs = []
        for hh in range(NSA_HPG):
            rs = slice(hh * tq, (hh + 1) * tq)
            gc = jnp.where(g == 0, gates[:, hh:hh + 1], gates[:, NSA_HPG + hh:NSA_HPG + hh + 1])
            gs = jnp.where(g == 0, gates[:, 8 + hh:9 + hh], gates[:, 12 + hh:13 + hh])
            gw = jnp.where(g == 0, gates[:, 16 + hh:17 + hh], gates[:, 20 + hh:21 + hh])
            outs.append((gc * o_c[rs] + gs * o_s[rs] + gw * o_w[rs]).astype(BF16))
        o_ref[pl.ds(t0, tq), :] = jnp.concatenate(outs, axis=1)

    tiles_per_chunk = ck // tq
    for n in range(1, seq // ck + 1):
        def body(par, carry, n=n):
            t0 = pl.multiple_of(((n - 1) * tiles_per_chunk + par) * tq, tq)
            q_tile(t0, n, early=(n * ck <= WINDOW))
            return carry
        lax.fori_loop(0, tiles_per_chunk, body, 0)


def _overlap_matrix():
    c = jnp.arange(N_CMP_PAD)[None, :]
    j = jnp.arange(N_SEL_BLOCKS)[:, None]
    lo = (CMP_STRIDE * c) // SEL_BLOCK
    hi = (CMP_STRIDE * c + CMP_BLOCK - 1) // SEL_BLOCK
    return ((j == lo) | (j == hi)).astype(BF16)


def _nsa(qn, ks, vs, kw, vw, kc, vc, gates):
    B, T, _ = qn.shape
    gw = NSA_HPG * HEAD_PAD
    q_spec = pl.BlockSpec((None, T, gw), lambda b, g: (b, 0, g))
    kv_spec = pl.BlockSpec((None, T, HEAD_PAD), lambda b, g: (b, 0, g))
    c_spec = pl.BlockSpec((None, None, N_CMP_PAD, HEAD_PAD), lambda b, g: (b, g, 0, 0))
    return pl.pallas_call(
        _nsa_kernel,
        grid=(B, NSA_GROUPS),
        in_specs=[q_spec, kv_spec, kv_spec, kv_spec, kv_spec, c_spec, c_spec,
                  pl.BlockSpec((None, T, 128), lambda b, g: (b, 0, 0)),
                  pl.BlockSpec((N_SEL_BLOCKS, N_CMP_PAD), lambda b, g: (0, 0))],
        out_specs=q_spec,
        out_shape=jax.ShapeDtypeStruct((B, T, NSA_HEADS * HEAD_PAD), BF16),
        compiler_params=pltpu.CompilerParams(
            dimension_semantics=("arbitrary", "arbitrary"), vmem_limit_bytes=VMEM_LIMIT),
        name="nsa",
    )(qn, ks, vs, kw, vw, kc, vc, gates, _overlap_matrix())


def _fox_kernel(q_ref, k_ref, v_ref, o_ref):
    seq = q_ref.shape[0]
    ck = CK_ATTN
    row = lax.broadcasted_iota(I32, (ck, 1), 0)
    col = lax.broadcasted_iota(I32, (1, ck), 1)
    for qi in range(seq // ck):
        q = q_ref[qi * ck:(qi + 1) * ck, :]
        s_chunks = []
        mx = None
        for c in range(qi + 1):
            s = _dot_nt(q, k_ref[c * ck:(c + 1) * ck, :])
            if c == qi:
                s = jnp.where(row >= col, s, NEG_BIG)
            s_chunks.append(s)
            mx = _fold_blocks(jnp.maximum, mx, s)
        m = jnp.max(mx, axis=-1, keepdims=True)
        ls = None
        acc = None
        for c in range(qi + 1):
            p = jnp.exp(s_chunks[c] - m)
            ls = _fold_blocks(jnp.add, ls, p)
            pv = _dot(p.astype(BF16), v_ref[c * ck:(c + 1) * ck, :])
            acc = pv if acc is None else acc + pv
        o_ref[qi * ck:(qi + 1) * ck, :] = (acc / jnp.sum(ls, axis=-1, keepdims=True)).astype(BF16)


def _fox(qf, kf, vf):
    B, T, _ = qf.shape
    spec = pl.BlockSpec((None, T, HEAD_PAD), lambda b, h: (b, 0, h))
    return pl.pallas_call(
        _fox_kernel,
        grid=(B, FOX_HEADS),
        in_specs=[spec, spec, spec],
        out_specs=spec,
        out_shape=jax.ShapeDtypeStruct((B, T, FOX_HEADS * HEAD_PAD), BF16),
        compiler_params=pltpu.CompilerParams(
            dimension_semantics=("arbitrary", "arbitrary"), vmem_limit_bytes=VMEM_LIMIT),
        name="fox",
    )(qf, kf, vf)


def _merge_out_kernel(x_ref, g_ref, on_ref, of_ref, wm_ref, bm_ref, wun_ref, wuf_ref, wo_ref, o_ref):
    x = x_ref[...]
    h = (_rms(x) * g_ref[...]).astype(BF16)
    merge = jax.nn.sigmoid(_dot(h, wm_ref[...]) + bm_ref[...])
    y = merge[:, :D_MODEL] * _dot(on_ref[...], wun_ref[...]) + merge[:, D_MODEL:] * _dot(of_ref[...], wuf_ref[...])
    o_ref[...] = x + _dot(y.astype(BF16), wo_ref[...])


def _merge_out(x2, g_attn, o_nsa, o_fox, w_merge, b_merge, w_up_nsa, w_up_fox, w_out):
    N, D = x2.shape
    tm = TM_OUT

    def pad_rows(w, n_heads):
        return _pad_heads(w.T, n_heads).T.astype(BF16)

    def tok(width):
        return pl.BlockSpec((tm, width), lambda i: (i, 0))

    def full(shape):
        return pl.BlockSpec(shape, lambda i: (0,) * len(shape))

    hw = NSA_HEADS * HEAD_PAD
    return pl.pallas_call(
        _merge_out_kernel,
        grid=(N // tm,),
        in_specs=[tok(D), full((1, D)), tok(hw), tok(hw), full((D, 2 * D)), full((1, 2 * D)),
                  full((hw, D)), full((hw, D)), full((D, D))],
        out_specs=tok(D),
        out_shape=jax.ShapeDtypeStruct((N, D), F32),
        compiler_params=pltpu.CompilerParams(dimension_semantics=("arbitrary",), vmem_limit_bytes=VMEM_LIMIT),
        name="merge_out",
    )(x2, g_attn.reshape(1, D), o_nsa, o_fox, w_merge.astype(BF16), b_merge.reshape(1, 2 * D),
      pad_rows(w_up_nsa, NSA_HEADS), pad_rows(w_up_fox, FOX_HEADS), w_out.astype(BF16))


def _router_kernel(x_ref, g_ref, wr_ref, br_ref, idx_ref, wgt_ref, rank_ref, cnt_ref, carry_ref):
    tm = x_ref.shape[0]

    @pl.when(pl.program_id(0) == 0)
    def _():
        carry_ref[...] = jnp.zeros_like(carry_ref)

    h = _rms(x_ref[...]) * g_ref[...]
    logits = jnp.dot(h, wr_ref[...], preferred_element_type=F32, precision=lax.Precision.HIGHEST) + br_ref[...]
    lane = lax.broadcasted_iota(I32, (1, 128), 1)
    lane_f = lane.astype(F32)
    work = jnp.where(lane < N_EXPERTS, logits, -jnp.inf)
    vals, hots = [], []
    idx_out = jnp.zeros((tm, 128), F32)
    for k in range(TOP_K):
        m = jnp.max(work, axis=-1, keepdims=True)
        idx = jnp.min(jnp.where(work == m, lane_f, 128.0), axis=-1, keepdims=True)
        hot = lane_f == idx
        work = jnp.where(hot, -jnp.inf, work)
        vals.append(m)
        hots.append(hot)
        idx_out = idx_out + jnp.where(lane == k, idx, 0.0)
    e = [jnp.exp(v - vals[0]) for v in vals]
    denom = e[0] + e[1] + e[2] + e[3]
    wgt = jnp.zeros((tm, 128), F32)
    for k in range(TOP_K):
        wgt = wgt + jnp.where(lane == k, e[k] / denom, 0.0)

    multi = (hots[0] | hots[1] | hots[2] | hots[3]).astype(F32)
    r = lax.broadcasted_iota(I32, (tm, tm), 0)
    c = lax.broadcasted_iota(I32, (tm, tm), 1)
    before = _dot((r > c).astype(BF16), multi.astype(BF16)) + carry_ref[...]
    rank = jnp.zeros((tm, 128), F32)
    for k in range(TOP_K):
        rk = jnp.sum(jnp.where(hots[k], before, 0.0), axis=-1, keepdims=True)
        rank = rank + jnp.where(lane == k, rk, 0.0)
    total = carry_ref[...] + jnp.sum(multi, axis=0, keepdims=True)
    carry_ref[...] = total
    idx_ref[...] = idx_out.astype(I32)
    wgt_ref[...] = wgt
    rank_ref[...] = rank.astype(I32)
    cnt_ref[...] = total.astype(I32)


def _router(x1, g_mlp, w_router, b_router):
    N, D = x1.shape
    tm = TM_ROUTE
    wr = jnp.pad(w_router, ((0, 0), (0, 128 - N_EXPERTS)))
    br = jnp.pad(b_router, (0, 128 - N_EXPERTS)).reshape(1, 128)
    tok = pl.BlockSpec((tm, 128), lambda i: (i, 0))
    one = pl.BlockSpec((1, 128), lambda i: (0, 0))
    return pl.pallas_call(
        _router_kernel,
        grid=(N // tm,),
        in_specs=[pl.BlockSpec((tm, D), lambda i: (i, 0)), pl.BlockSpec((1, D), lambda i: (0, 0)),
                  pl.BlockSpec((D, 128), lambda i: (0, 0)), one],
        out_specs=[tok, tok, tok, one],
        out_shape=[jax.ShapeDtypeStruct((N, 128), I32), jax.ShapeDtypeStruct((N, 128), F32),
                   jax.ShapeDtypeStruct((N, 128), I32), jax.ShapeDtypeStruct((1, 128), I32)],
        scratch_shapes=[pltpu.VMEM((1, 128), F32)],
        compiler_params=pltpu.CompilerParams(dimension_semantics=("arbitrary",), vmem_limit_bytes=VMEM_LIMIT),
        name="router",
    )(x1, g_mlp.reshape(1, D), wr, br)


def _row_copy(src_ref, src_row, dst_ref, dst_row, sem):
    return pltpu.make_async_copy(src_ref.at[pl.ds(src_row, 1), :], dst_ref.at[pl.ds(dst_row, 1), :], sem)


def _dispatch_kernel(pos_ref, x_ref, xs_in_ref, xs_ref, sem):
    del xs_in_ref
    tm = x_ref.shape[0]

    def issue(r, carry):
        for k in range(TOP_K):
            _row_copy(x_ref, r, xs_ref, pos_ref[r * TOP_K + k], sem).start()
        return carry

    lax.fori_loop(0, tm, issue, 0)

    def drain(r, carry):
        for k in range(TOP_K):
            _row_copy(x_ref, 0, xs_ref, 0, sem).wait()
        return carry

    lax.fori_loop(0, tm, drain, 0)


def _dispatch(x1, pos_flat, n_rows):
    N, D = x1.shape
    tm = TM_MOVE
    return pl.pallas_call(
        _dispatch_kernel,
        grid=(N // tm,),
        in_specs=[pl.BlockSpec((tm * TOP_K,), lambda i: (i,), memory_space=pltpu.SMEM),
                  pl.BlockSpec((tm, D), lambda i: (i, 0)),
                  pl.BlockSpec(memory_space=pl.ANY)],
        out_specs=pl.BlockSpec(memory_space=pl.ANY),
        out_shape=jax.ShapeDtypeStruct((n_rows, D), F32),
        scratch_shapes=[pltpu.SemaphoreType.DMA],
        input_output_aliases={2: 0},
        compiler_params=pltpu.CompilerParams(dimension_semantics=("arbitrary",), vmem_limit_bytes=VMEM_LIMIT),
        name="dispatch",
    )(pos_flat, x1, jnp.zeros((n_rows, D), F32))


def _experts_kernel(te_ref, nv_ref, xs_ref, g_ref, wg_ref, bg_ref, wu_ref, bu_ref, wd_ref, bd_ref, ys_ref):
    i = pl.program_id(0)

    @pl.when(i < nv_ref[0])
    def _():
        h = (_rms(xs_ref[...]) * g_ref[...]).astype(BF16)
        gate = jnp.minimum(_dot(h, wg_ref[...]) + bg_ref[...], SWIGLU_LIMIT)
        up = jnp.clip(_dot(h, wu_ref[...]) + bu_ref[...], -SWIGLU_LIMIT, SWIGLU_LIMIT)
        act = (up + 1.0) * gate * jax.nn.sigmoid(SWIGLU_ALPHA * gate)
        ys_ref[...] = _dot(act.astype(BF16), wd_ref[...]) + bd_ref[...]

    @pl.when(i >= nv_ref[0])
    def _():
        ys_ref[...] = jnp.zeros_like(ys_ref)


def _experts(xs, tile_expert, n_valid, g_mlp, w_gate, b_gate, w_up, b_up, w_down, b_down):
    P, D = xs.shape
    tm = TM_EXPERT
    E, _, F = w_gate.shape
    w_spec = pl.BlockSpec((None, D, F), lambda i, te, nv: (te[i], 0, 0))
    wd_spec = pl.BlockSpec((None, F, D), lambda i, te, nv: (te[i], 0, 0))
    b_spec = pl.BlockSpec((None, 1, F), lambda i, te, nv: (te[i], 0, 0))
    bd_spec = pl.BlockSpec((None, 1, D), lambda i, te, nv: (te[i], 0, 0))
    tok = pl.BlockSpec((tm, D), lambda i, te, nv: (i, 0))
    return pl.pallas_call(
        _experts_kernel,
        grid_spec=pltpu.PrefetchScalarGridSpec(
            num_scalar_prefetch=2,
            grid=(P // tm,),
            in_specs=[tok, pl.BlockSpec((1, D), lambda i, te, nv: (0, 0)),
                      w_spec, b_spec, w_spec, b_spec, wd_spec, bd_spec],
            out_specs=tok),
        out_shape=jax.ShapeDtypeStruct((P, D), F32),
        compiler_params=pltpu.CompilerParams(dimension_semantics=("arbitrary",), vmem_limit_bytes=VMEM_LIMIT),
        name="experts",
    )(tile_expert, n_valid, xs, g_mlp.reshape(1, D),
      w_gate.astype(BF16), b_gate.reshape(E, 1, F), w_up.astype(BF16), b_up.reshape(E, 1, F),
      w_down.astype(BF16), b_down.reshape(E, 1, D))


def _combine_kernel(pos_ref, x_ref, wgt_ref, ys_ref, o_ref, buf_ref, sem):
    tm = x_ref.shape[0]

    def issue(r, carry):
        for k in range(TOP_K):
            _row_copy(ys_ref, pos_ref[r * TOP_K + k], buf_ref.at[k], r, sem).start()
        return carry

    lax.fori_loop(0, tm, issue, 0)

    def drain(r, carry):
        for k in range(TOP_K):
            _row_copy(ys_ref, 0, buf_ref.at[k], 0, sem).wait()
        return carry

    lax.fori_loop(0, tm, drain, 0)
    wgt = wgt_ref[...]
    acc = x_ref[...]
    for k in range(TOP_K):
        acc = acc + wgt[:, k:k + 1] * buf_ref[k]
    o_ref[...] = acc


def _combine(x1, wgt, pos_flat, ys):
    N, D = x1.shape
    tm = TM_MOVE
    return pl.pallas_call(
        _combine_kernel,
        grid=(N // tm,),
        in_specs=[pl.BlockSpec((tm * TOP_K,), lambda i: (i,), memory_space=pltpu.SMEM),
                  pl.BlockSpec((tm, D), lambda i: (i, 0)),
                  pl.BlockSpec((tm, 128), lambda i: (i, 0)),
                  pl.BlockSpec(memory_space=pl.ANY)],
        out_specs=pl.BlockSpec((tm, D), lambda i: (i, 0)),
        out_shape=jax.ShapeDtypeStruct((N, D), F32),
        scratch_shapes=[pltpu.VMEM((TOP_K, tm, D), F32), pltpu.SemaphoreType.DMA],
        compiler_params=pltpu.CompilerParams(dimension_semantics=("arbitrary",), vmem_limit_bytes=VMEM_LIMIT),
        name="combine",
    )(pos_flat, x1, wgt, ys)


def _moe(x1, g_mlp, w_router, b_router, w_gate, b_gate, w_up, b_up, w_down, b_down):
    N, D = x1.shape
    idx, wgt, rank, cnt = _router(x1, g_mlp, w_router, b_router)
    tm = TM_EXPERT
    n_tiles = -(-(N * TOP_K + N_EXPERTS * (tm - 1)) // tm)
    counts = cnt[0, :N_EXPERTS]
    padded = ((counts + tm - 1) // tm) * tm
    ends = jnp.cumsum(padded)
    starts = ends - padded
    pos = (starts[idx[:, :TOP_K]] + rank[:, :TOP_K]).reshape(-1).astype(I32)
    tile_start = jnp.arange(n_tiles, dtype=I32) * tm
    tile_expert = jnp.minimum(jnp.sum(ends[None, :] <= tile_start[:, None], axis=1), N_EXPERTS - 1).astype(I32)
    n_valid = (ends[-1] // tm).astype(I32).reshape(1)
    xs = _dispatch(x1, pos, n_tiles * tm)
    ys = _experts(xs, tile_expert, n_valid, g_mlp, w_gate, b_gate, w_up, b_up, w_down, b_down)
    return _combine(x1, wgt, pos, ys)


def _layer(x, g_attn, w_in, qn_nsa, kn_cmp, kn_sel, kn_win,
           pos_cmp_k, w_ck1, b_ck1, w_ck2, b_ck2, pos_cmp_v, w_cv1, b_cv1, w_cv2, b_cv2,
           b_nsa_gate, qn_fox, kn_fox, b_forget, w_up_nsa, w_up_fox, b_merge, w_out, g_mlp,
           w_router, b_router, w_e_gate, b_e_gate, w_e_up, b_e_up, w_e_down, b_e_down):
    B, T, D = x.shape
    qn, ks, vs, kw, vw, kc_raw, vc_raw, qf, kf, vf, gates = _in_proj(
        x, g_attn, w_in, qn_nsa, kn_sel, kn_win, b_nsa_gate, qn_fox, kn_fox, b_forget)
    kc, vc = _compress(kc_raw, vc_raw, pos_cmp_k, w_ck1, b_ck1, w_ck2, b_ck2,
                       pos_cmp_v, w_cv1, b_cv1, w_cv2, b_cv2, kn_cmp)
    o_nsa = _nsa(qn, ks, vs, kw, vw, kc, vc, gates)
    o_fox = _fox(qf, kf, vf)
    hw = NSA_HEADS * HEAD_PAD
    w_merge = w_in[:, w_in.shape[1] - 2 * D:]
    x1 = _merge_out(x.reshape(B * T, D), g_attn, o_nsa.reshape(B * T, hw), o_fox.reshape(B * T, hw),
                    w_merge, b_merge, w_up_nsa, w_up_fox, w_out)
    out = _moe(x1, g_mlp, w_router, b_router, w_e_gate, b_e_gate, w_e_up, b_e_up, w_e_down, b_e_down)
    return out.reshape(B, T, D)


def kernel(x, g_attn, w_in, qn_nsa, kn_cmp, kn_sel, kn_win, pos_cmp_k, w_ck1, b_ck1, w_ck2, b_ck2, pos_cmp_v, w_cv1, b_cv1, w_cv2, b_cv2, b_nsa_gate, qn_fox, kn_fox, b_forget, w_up_nsa, w_up_fox, b_merge, w_out, g_mlp, w_router, b_router, w_e_gate, b_e_gate, w_e_up, b_e_up, w_e_down, b_e_down):
    params = (g_attn, w_in, qn_nsa, kn_cmp, kn_sel, kn_win, pos_cmp_k, w_ck1, b_ck1, w_ck2, b_ck2,
              pos_cmp_v, w_cv1, b_cv1, w_cv2, b_cv2, b_nsa_gate, qn_fox, kn_fox, b_forget,
              w_up_nsa, w_up_fox, b_merge, w_out, g_mlp, w_router, b_router,
              w_e_gate, b_e_gate, w_e_up, b_e_up, w_e_down, b_e_down)
    for layer in range(g_attn.shape[0]):
        x = _layer(x, *[p[layer] for p in params])
    return x
```

```python
import functools

import jax
import jax.numpy as jnp
from jax import lax
from jax.experimental import pallas as pl
from jax.experimental.pallas import tpu as pltpu

F32 = jnp.float32
BF16 = jnp.bfloat16
I32 = jnp.int32

D_MODEL = 1024
HEAD_DIM = 64
HEAD_PAD = 128
NSA_HEADS = 8
NSA_GROUPS = 2
NSA_HPG = NSA_HEADS // NSA_GROUPS
FOX_HEADS = 8
CMP_BLOCK = 32
CMP_STRIDE = 16
CMP_HIDDEN = 256
N_CMP_PAD = 128
SEL_BLOCK = 64
SEL_TOPN = 16
N_SEL_BLOCKS = 32
WINDOW = 512
N_EXPERTS = 32
TOP_K = 4
SWIGLU_LIMIT = 7.0
SWIGLU_ALPHA = 1.702
RMS_EPS = 1e-6
NEG_BIG = -1e30
FORCED_SCORE = 1e9
QK_SCALE = HEAD_DIM ** -0.5

LANE_SEL = 64
LANE_POS = 96
LANE_FOX = 64

OFF_QN = 0
OFF_KS = OFF_QN + NSA_HEADS * HEAD_PAD
OFF_VS = OFF_KS + NSA_GROUPS * HEAD_PAD
OFF_KW = OFF_VS + NSA_GROUPS * HEAD_PAD
OFF_VW = OFF_KW + NSA_GROUPS * HEAD_PAD
OFF_KC = OFF_VW + NSA_GROUPS * HEAD_PAD
OFF_VC = OFF_KC + NSA_GROUPS * HEAD_DIM
OFF_QF = OFF_VC + NSA_GROUPS * HEAD_DIM
OFF_KF = OFF_QF + FOX_HEADS * HEAD_PAD
OFF_VF = OFF_KF + FOX_HEADS * HEAD_PAD
OFF_MISC = OFF_VF + FOX_HEADS * HEAD_PAD
W1_COLS = OFF_MISC + 128
MISC_GATES = 3 * NSA_HEADS

TM_PROJ = 256
TQ_NSA = 128
TQ_SELECT = 512
TILES_PER_REGION = 2
CK_ATTN = 256
TM_OUT = 256
TM_ROUTE = 256
TM_EXPERT = 512
TM_DISPATCH = 1024
TM_COMBINE = 512
ROWS_PER_ISSUE = 8
RANK_RADIX = 1 << 16
VMEM_LIMIT = 56 * 1024 * 1024


def _dot(a, b):
    return jnp.dot(a, b, preferred_element_type=F32)


def _dot_nt(a, b):
    return lax.dot_general(a, b, (((1,), (1,)), ((), ())), preferred_element_type=F32)


def _split3(v):
    hi = v.astype(BF16)
    r1 = v - hi.astype(F32)
    mid = r1.astype(BF16)
    lo = (r1 - mid.astype(F32)).astype(BF16)
    return hi, mid, lo


def _dot_exact_rhs(a_f32, b_bf16):
    hi, mid, lo = _split3(a_f32)
    return _dot(hi, b_bf16) + _dot(mid, b_bf16) + _dot(lo, b_bf16)


def _dot_exact_lhs(a_bf16, b_f32):
    hi, mid, lo = _split3(b_f32)
    return _dot(a_bf16, hi) + _dot(a_bf16, mid) + _dot(a_bf16, lo)


def _rms(x):
    return x * lax.rsqrt(jnp.mean(x * x, axis=-1, keepdims=True) + RMS_EPS)


def _head_rms(a, gain):
    ms = jnp.sum(a * a, axis=-1, keepdims=True) * (1.0 / HEAD_DIM)
    return a * lax.rsqrt(ms + RMS_EPS) * gain


def _pos_feat_k(pos, lane):
    hi = (256 * (pos >> 8)).astype(F32)
    lo = (pos & 255).astype(F32)
    return jnp.where(lane < LANE_POS + 2, 1.0, jnp.where(lane == LANE_POS + 2, hi, lo)) * (
        (lane >= LANE_POS) & (lane < LANE_POS + 4)).astype(F32)


def _in_proj_kernel(x_ref, g_ref, w_ref, gains_ref, bmisc_ref, pq_ref, pk_ref,
                    qn_ref, ks_ref, vs_ref, kw_ref, vw_ref, kc_ref, vc_ref,
                    qf_ref, kf_ref, vf_ref, gate_ref, carry_ref):
    tm = x_ref.shape[0]
    i = pl.program_id(1)
    h = (_rms(x_ref[...]) * g_ref[...]).astype(BF16)
    pos = i * tm + lax.broadcasted_iota(I32, (tm, 1), 0)
    lane = lax.broadcasted_iota(I32, (1, HEAD_PAD), 1)
    pos_hi = (256 * (pos >> 8)).astype(F32)
    pos_lo = (pos & 255).astype(F32)
    kfeat = _pos_feat_k(pos, lane)
    blk_onehot = ((lane - LANE_SEL) == (pos >> 6)).astype(F32)

    def seg(off, width):
        return _dot(h, w_ref[:, off:off + width])

    a = seg(OFF_QN, NSA_HEADS * HEAD_PAD)
    gq = gains_ref[0:1, :] * QK_SCALE
    for hd in range(NSA_HEADS):
        slope = 2.0 ** (-(hd + 1))
        qfeat = jnp.where(lane == LANE_POS, -slope * pos_hi,
                          jnp.where(lane == LANE_POS + 1, -slope * pos_lo,
                                    jnp.where((lane == LANE_POS + 2) | (lane == LANE_POS + 3), slope, 0.0)))
        sl = slice(hd * HEAD_PAD, (hd + 1) * HEAD_PAD)
        qn_ref[:, sl] = (_head_rms(a[:, sl], gq) + qfeat).astype(BF16)

    a = seg(OFF_KS, NSA_GROUPS * HEAD_PAD)
    for g in range(NSA_GROUPS):
        sl = slice(g * HEAD_PAD, (g + 1) * HEAD_PAD)
        ks_ref[:, sl] = (_head_rms(a[:, sl], gains_ref[1:2, :]) + kfeat + blk_onehot).astype(BF16)
    vs_ref[...] = seg(OFF_VS, NSA_GROUPS * HEAD_PAD).astype(BF16)
    a = seg(OFF_KW, NSA_GROUPS * HEAD_PAD)
    for g in range(NSA_GROUPS):
        sl = slice(g * HEAD_PAD, (g + 1) * HEAD_PAD)
        kw_ref[:, sl] = (_head_rms(a[:, sl], gains_ref[2:3, :]) + kfeat).astype(BF16)
    vw_ref[...] = seg(OFF_VW, NSA_GROUPS * HEAD_PAD).astype(BF16)
    kc_ref[...] = seg(OFF_KC, NSA_GROUPS * HEAD_DIM).astype(BF16)
    vc_ref[...] = seg(OFF_VC, NSA_GROUPS * HEAD_DIM).astype(BF16)

    misc = seg(OFF_MISC, 128) + bmisc_ref[...]
    is_gate = lane < MISC_GATES
    gate_ref[...] = jnp.where(is_gate, jax.nn.sigmoid(misc), 0.0)
    is_f = (lane >= MISC_GATES) & (lane < MISC_GATES + 3 * FOX_HEADS)
    logf = jnp.where(is_f, jax.nn.log_sigmoid(misc), 0.0)

    @pl.when(i == 0)
    def _():
        carry_ref[...] = jnp.zeros_like(carry_ref)

    r = lax.broadcasted_iota(I32, (tm, tm), 0)
    c = lax.broadcasted_iota(I32, (tm, tm), 1)
    tri = (r >= c).astype(BF16)
    csum = _dot_exact_lhs(tri, logf) + carry_ref[...]
    carry_ref[...] = csum[tm - 1:tm, :]
    c_hi, c_mid, c_lo = _split3(csum)
    c3 = jnp.where(lane < MISC_GATES + FOX_HEADS, c_hi,
                   jnp.where(lane < MISC_GATES + 2 * FOX_HEADS, c_mid, c_lo))
    c3 = jnp.where(is_f, c3, jnp.zeros_like(c3))
    ones_lo = ((lane >= LANE_FOX) & (lane < LANE_FOX + 3)).astype(F32)
    ones_hi = ((lane >= LANE_FOX + 3) & (lane < LANE_FOX + 6)).astype(F32)

    a = seg(OFF_QF, FOX_HEADS * HEAD_PAD)
    fq = _dot(c3, pq_ref[...])
    gq = gains_ref[3:4, :] * QK_SCALE
    for hd in range(FOX_HEADS):
        sl = slice(hd * HEAD_PAD, (hd + 1) * HEAD_PAD)
        qf_ref[:, sl] = (_head_rms(a[:, sl], gq) + fq[:, sl] + ones_hi).astype(BF16)
    a = seg(OFF_KF, FOX_HEADS * HEAD_PAD)
    fk = _dot(c3, pk_ref[...])
    for hd in range(FOX_HEADS):
        sl = slice(hd * HEAD_PAD, (hd + 1) * HEAD_PAD)
        kf_ref[:, sl] = (_head_rms(a[:, sl], gains_ref[4:5, :]) + fk[:, sl] + ones_lo).astype(BF16)
    vf_ref[...] = seg(OFF_VF, FOX_HEADS * HEAD_PAD).astype(BF16)


def _pad_heads(w, n_heads):
    lead = w.shape[:-1]
    w = w.reshape(lead + (n_heads, HEAD_DIM))
    w = jnp.pad(w, [(0, 0)] * len(lead) + [(0, 0), (0, HEAD_PAD - HEAD_DIM)])
    return w.reshape(lead + (n_heads * HEAD_PAD,))


def _pad_gain(g):
    return jnp.pad(g.astype(F32), (0, HEAD_PAD - HEAD_DIM))


def _fox_placement():
    pq = jnp.zeros((128, FOX_HEADS * HEAD_PAD), F32)
    pk = jnp.zeros((128, FOX_HEADS * HEAD_PAD), F32)
    for piece in range(3):
        for hd in range(FOX_HEADS):
            src = MISC_GATES + piece * FOX_HEADS + hd
            pq = pq.at[src, hd * HEAD_PAD + LANE_FOX + piece].set(1.0)
            pk = pk.at[src, hd * HEAD_PAD + LANE_FOX + 3 + piece].set(-1.0)
    return pq.astype(BF16), pk.astype(BF16)


def _in_proj(x, g_attn, w_in, qn_nsa, kn_sel, kn_win, b_nsa_gate, qn_fox, kn_fox, b_forget):
    B, T, D = x.shape
    s1 = NSA_HEADS * HEAD_DIM
    kv_w = NSA_GROUPS * HEAD_DIM
    s2 = s1 + 6 * kv_w
    s3 = s2 + 3 * NSA_HEADS
    fw = FOX_HEADS * HEAD_DIM
    s4 = s3 + 3 * fw
    s5 = s4 + FOX_HEADS
    kv = [w_in[:, s1 + j * kv_w: s1 + (j + 1) * kv_w] for j in range(6)]
    f_cols = w_in[:, s4:s5]
    misc = jnp.concatenate([w_in[:, s2:s3], f_cols, f_cols, f_cols,
                            jnp.zeros((D, 128 - MISC_GATES - 3 * FOX_HEADS), F32)], axis=1)
    w1 = jnp.concatenate([
        _pad_heads(w_in[:, :s1], NSA_HEADS),
        _pad_heads(kv[2], NSA_GROUPS), _pad_heads(kv[3], NSA_GROUPS),
        _pad_heads(kv[4], NSA_GROUPS), _pad_heads(kv[5], NSA_GROUPS),
        kv[0], kv[1],
        _pad_heads(w_in[:, s3:s3 + fw], FOX_HEADS),
        _pad_heads(w_in[:, s3 + fw:s3 + 2 * fw], FOX_HEADS),
        _pad_heads(w_in[:, s3 + 2 * fw:s4], FOX_HEADS),
        misc], axis=1).astype(BF16)
    assert w1.shape[1] == W1_COLS
    gains = jnp.stack([_pad_gain(qn_nsa), _pad_gain(kn_sel), _pad_gain(kn_win),
                       _pad_gain(qn_fox), _pad_gain(kn_fox),
                       jnp.zeros(HEAD_PAD, F32), jnp.zeros(HEAD_PAD, F32), jnp.zeros(HEAD_PAD, F32)])
    bmisc = jnp.concatenate([b_nsa_gate, b_forget, b_forget, b_forget,
                             jnp.zeros(128 - MISC_GATES - 3 * FOX_HEADS, F32)]).reshape(1, 128)
    pq, pk = _fox_placement()
    tm = TM_PROJ

    def tok(width):
        return pl.BlockSpec((None, tm, width), lambda b, i: (b, i, 0))

    def full(shape):
        return pl.BlockSpec(shape, lambda b, i: (0,) * len(shape))

    widths = [NSA_HEADS * HEAD_PAD] + [NSA_GROUPS * HEAD_PAD] * 4 + [NSA_GROUPS * HEAD_DIM] * 2 + \
             [FOX_HEADS * HEAD_PAD] * 3
    out_shape = [jax.ShapeDtypeStruct((B, T, w), BF16) for w in widths] + \
                [jax.ShapeDtypeStruct((B, T, 128), F32)]
    return pl.pallas_call(
        _in_proj_kernel,
        grid=(B, T // tm),
        in_specs=[tok(D), full((1, D)), full((D, W1_COLS)), full((8, HEAD_PAD)), full((1, 128)),
                  full((128, FOX_HEADS * HEAD_PAD)), full((128, FOX_HEADS * HEAD_PAD))],
        out_specs=[tok(w) for w in widths] + [tok(128)],
        out_shape=out_shape,
        scratch_shapes=[pltpu.VMEM((1, 128), F32)],
        compiler_params=pltpu.CompilerParams(
            dimension_semantics=("arbitrary", "arbitrary"), vmem_limit_bytes=VMEM_LIMIT),
        name="in_proj",
    )(x, g_attn.reshape(1, D), w1, gains, bmisc, pq, pk)


def _gelu_tanh(x):
    return 0.5 * x * (1.0 + jnp.tanh(0.7978845608028654 * (x + 0.044715 * (x * x * x))))


def _compress_kernel(ak_ref, av_ref, posk_ref, posv_ref, wk1_ref, bk1_ref, wk2_ref, bk2_ref,
                     wv1_ref, bv1_ref, wv2_ref, bv2_ref, gain_ref, kc_ref, vc_ref):
    half = (CMP_BLOCK // 2) * HEAD_DIM
    row = lax.broadcasted_iota(I32, (N_CMP_PAD, 1), 0)
    lane = lax.broadcasted_iota(I32, (1, HEAD_PAD), 1)
    kfeat = _pos_feat_k(CMP_STRIDE * row + CMP_BLOCK - 1, lane)

    def mlp(a, pos_ref, w1_ref, b1_ref, w2_ref, b2_ref):
        a = a.astype(F32)
        lo = (a + pos_ref[0:1, :]).astype(BF16)
        hi = (a + pos_ref[1:2, :]).astype(BF16)
        p_lo = _dot(lo, w1_ref[0:half, :])
        p_hi = _dot(hi, w1_ref[half:2 * half, :])
        hid = p_lo + pltpu.roll(p_hi, N_CMP_PAD - 1, 0) + b1_ref[...]
        return _dot(_gelu_tanh(hid).astype(BF16), w2_ref[...]) + b2_ref[...]

    for g in range(NSA_GROUPS):
        k = mlp(ak_ref[g], posk_ref, wk1_ref, bk1_ref, wk2_ref, bk2_ref)
        kc_ref[g] = (_head_rms(k, gain_ref[...]) + kfeat).astype(BF16)
        vc_ref[g] = mlp(av_ref[g], posv_ref, wv1_ref, bv1_ref, wv2_ref, bv2_ref).astype(BF16)


def _compress(kc_raw, vc_raw, pos_k, w_ck1, b_ck1, w_ck2, b_ck2, pos_v, w_cv1, b_cv1, w_cv2, b_cv2, kn_cmp):
    B, T, _ = kc_raw.shape
    n_chunks = T // CMP_STRIDE
    assert n_chunks == N_CMP_PAD
    chunk_w = CMP_STRIDE * HEAD_DIM

    def chunks(a):
        a = a.reshape(B, n_chunks, CMP_STRIDE, NSA_GROUPS, HEAD_DIM)
        return a.transpose(0, 3, 1, 2, 4).reshape(B, NSA_GROUPS, n_chunks, chunk_w)

    def w2pad(w, b):
        return (jnp.pad(w, ((0, 0), (0, HEAD_PAD - HEAD_DIM))).astype(BF16),
                jnp.pad(b, (0, HEAD_PAD - HEAD_DIM)).reshape(1, HEAD_PAD))

    wk2, bk2 = w2pad(w_ck2, b_ck2)
    wv2, bv2 = w2pad(w_cv2, b_cv2)
    a_spec = pl.BlockSpec((None, NSA_GROUPS, n_chunks, chunk_w), lambda b: (b, 0, 0, 0))
    o_spec = pl.BlockSpec((None, NSA_GROUPS, N_CMP_PAD, HEAD_PAD), lambda b: (b, 0, 0, 0))

    def full(shape):
        return pl.BlockSpec(shape, lambda b: (0,) * len(shape))

    hid2 = CMP_BLOCK * HEAD_DIM
    return pl.pallas_call(
        _compress_kernel,
        grid=(B,),
        in_specs=[a_spec, a_spec, full((2, chunk_w)), full((2, chunk_w)),
                  full((hid2, CMP_HIDDEN)), full((1, CMP_HIDDEN)), full((CMP_HIDDEN, HEAD_PAD)), full((1, HEAD_PAD)),
                  full((hid2, CMP_HIDDEN)), full((1, CMP_HIDDEN)), full((CMP_HIDDEN, HEAD_PAD)), full((1, HEAD_PAD)),
                  full((1, HEAD_PAD))],
        out_specs=[o_spec, o_spec],
        out_shape=[jax.ShapeDtypeStruct((B, NSA_GROUPS, N_CMP_PAD, HEAD_PAD), BF16)] * 2,
        compiler_params=pltpu.CompilerParams(dimension_semantics=("arbitrary",), vmem_limit_bytes=VMEM_LIMIT),
        name="compress",
    )(chunks(kc_raw), chunks(vc_raw), pos_k.reshape(2, chunk_w), pos_v.reshape(2, chunk_w),
      w_ck1.astype(BF16), b_ck1.reshape(1, -1), wk2, bk2,
      w_cv1.astype(BF16), b_cv1.reshape(1, -1), wv2, bv2, _pad_gain(kn_cmp).reshape(1, HEAD_PAD))


def _lane_blocks(a):
    return [a[:, i * 128:(i + 1) * 128] for i in range(a.shape[1] // 128)]


def _fold_blocks(op, acc, a):
    for blk in _lane_blocks(a):
        acc = blk if acc is None else op(acc, blk)
    return acc


def _nsa_kernel(q_ref, ks_ref, vs_ref, kw_ref, vw_ref, kc_ref, vc_ref, gate_ref, ovl_ref, o_ref,
                sel_ref, oc_ref, s_ref):
    seq = q_ref.shape[0]
    tq = TQ_NSA
    ta = TQ_SELECT
    ck = CK_ATTN
    span = WINDOW + tq
    g = pl.program_id(1)
    col_l = lax.broadcasted_iota(I32, (1, 128), 1)
    c_end_lane = CMP_STRIDE * col_l + (CMP_BLOCK - 1)
    c_end_sub = CMP_STRIDE * lax.broadcasted_iota(I32, (N_CMP_PAD, 1), 0) + (CMP_BLOCK - 1)
    blk_id = lax.broadcasted_iota(I32, (N_SEL_BLOCKS, 1), 0)
    sub_id = lax.broadcasted_iota(I32, (8, 1), 0)

    def gate_col(gates, branch, hh):
        lo = branch * NSA_HEADS + hh
        return jnp.where(g == 0, gates[:, lo:lo + 1], gates[:, lo + NSA_HPG:lo + NSA_HPG + 1])

    def select(a):
        a0 = a * ta
        heads = [q_ref[a0:a0 + ta, hh * HEAD_PAD:(hh + 1) * HEAD_PAD] for hh in range(NSA_HPG)]
        q4 = jnp.concatenate(heads, axis=0)
        t_row = a0 + (lax.broadcasted_iota(I32, (NSA_HPG * ta, 1), 0) & (ta - 1))
        t_lane = a0 + lax.broadcasted_iota(I32, (1, ta), 1)

        s = _dot_nt(q4, kc_ref[...])
        mask = t_row >= c_end_lane
        s = jnp.where(mask, s, NEG_BIG)
        p = jnp.where(mask, jnp.exp(s - jnp.max(s, axis=-1, keepdims=True)), 0.0)
        l = jnp.sum(p, axis=-1, keepdims=True)
        o_c = _dot((p / jnp.where(l > 0.0, l, 1.0)).astype(BF16), vc_ref[...])
        gates = gate_ref[a0:a0 + ta, :]
        for hh in range(NSA_HPG):
            oc_ref[a0:a0 + ta, hh * HEAD_PAD:(hh + 1) * HEAD_PAD] = gate_col(gates, 0, hh) * o_c[hh * ta:(hh + 1) * ta]

        mask_t = t_lane >= c_end_sub
        p_sum = None
        for hh in range(NSA_HPG):
            s_t = jnp.where(mask_t, _dot_nt(kc_ref[...], heads[hh]), NEG_BIG)
            e_t = jnp.where(mask_t, jnp.exp(s_t - jnp.max(s_t, axis=0, keepdims=True)), 0.0)
            l_t = jnp.sum(e_t, axis=0, keepdims=True)
            p_t = e_t / jnp.where(l_t > 0.0, l_t, 1.0)
            p_sum = p_t if p_sum is None else p_sum + p_t
        imp = _dot_exact_lhs(ovl_ref[...], p_sum)
        tb = t_lane >> 6
        valid = blk_id <= tb
        forced = (blk_id == 0) | (blk_id == tb) | (blk_id == tb - 1)
        score = jnp.where(forced & valid, FORCED_SCORE, imp)
        score = jnp.where(valid, score, -1.0)
        groups = [score[8 * b:8 * b + 8, :] for b in range(N_SEL_BLOCKS // 8)]
        n_above = [jnp.zeros((8, ta), F32) for _ in groups]
        for j in range(N_SEL_BLOCKS):
            cj = score[j:j + 1, :]
            for b, grp in enumerate(groups):
                if 8 * b > j:
                    n_above[b] = n_above[b] + jnp.where(cj >= grp, 1.0, 0.0)
                elif 8 * b + 7 < j:
                    n_above[b] = n_above[b] + jnp.where(cj > grp, 1.0, 0.0)
                else:
                    tie = jnp.where(sub_id + 8 * b > j, 1.0, 0.0)
                    n_above[b] = n_above[b] + jnp.where(cj > grp, 1.0, 0.0) + jnp.where(cj == grp, tie, 0.0)
        dropped = jnp.where(jnp.concatenate(n_above, axis=0) >= float(SEL_TOPN), NEG_BIG, 0.0)
        drop_t = jnp.concatenate([jnp.zeros((LANE_SEL, ta), F32), dropped,
                                  jnp.zeros((HEAD_PAD - LANE_SEL - N_SEL_BLOCKS, ta), F32)], axis=0)
        for blk in range(ta // 128):
            sel_ref[a0 + blk * 128:a0 + (blk + 1) * 128, :] = drop_t[:, blk * 128:(blk + 1) * 128].T.astype(BF16)

    for a in range(seq // ta):
        pl.when(g >= -a)(functools.partial(select, a))

    rows = NSA_HPG * tq
    row_l = lax.broadcasted_iota(I32, (rows, 1), 0) & (tq - 1)

    def attend(i):
        t0 = i * tq
        t_row = t0 + row_l
        q4 = jnp.concatenate([q_ref[t0:t0 + tq, hh * HEAD_PAD:(hh + 1) * HEAD_PAD] for hh in range(NSA_HPG)], axis=0)
        q4s = q4 + jnp.concatenate([sel_ref[t0:t0 + tq, :]] * NSA_HPG, axis=0)

        n_chunks = (t0 + tq - 1) // ck + 1
        slot = i % 2
        mx = None
        for c in range(n_chunks):
            s = _dot_nt(q4s, ks_ref[c * ck:(c + 1) * ck, :])
            if (c + 1) * ck > t0:
                s = jnp.where(t_row >= c * ck + lax.broadcasted_iota(I32, (1, ck), 1), s, NEG_BIG)
            s_ref[slot, :, c * ck:(c + 1) * ck] = s
            mx = _fold_blocks(jnp.maximum, mx, s)
        m = jnp.max(mx, axis=-1, keepdims=True)
        ls = None
        acc = None
        for c in range(n_chunks):
            p = jnp.exp(s_ref[slot, :, c * ck:(c + 1) * ck] - m)
            ls = _fold_blocks(jnp.add, ls, p)
            pv = _dot(p.astype(BF16), vs_ref[c * ck:(c + 1) * ck, :])
            acc = pv if acc is None else acc + pv
        o_s = acc / jnp.sum(ls, axis=-1, keepdims=True)

        ws = max(t0 - WINDOW, 0)
        blocks = _lane_blocks(_dot_nt(q4, kw_ref[ws:ws + span, :]))
        if t0 < WINDOW:
            blocks = [jnp.where(t_row >= ws + jb * 128 + col_l, blk, NEG_BIG) if ws + (jb + 1) * 128 > t0 else blk
                      for jb, blk in enumerate(blocks)]
        else:
            blocks[0] = jnp.where(col_l > row_l, blocks[0], NEG_BIG)
            blocks[-1] = jnp.where(col_l <= row_l, blocks[-1], NEG_BIG)
        s = jnp.concatenate(blocks, axis=1)
        m = jnp.max(_fold_blocks(jnp.maximum, None, s), axis=-1, keepdims=True)
        p = jnp.exp(s - m)
        l = jnp.sum(_fold_blocks(jnp.add, None, p), axis=-1, keepdims=True)
        o_w = _dot(p.astype(BF16), vw_ref[ws:ws + span, :]) / l

        gates = gate_ref[t0:t0 + tq, :]
        outs = []
        for hh in range(NSA_HPG):
            rs = slice(hh * tq, (hh + 1) * tq)
            o = oc_ref[t0:t0 + tq, hh * HEAD_PAD:(hh + 1) * HEAD_PAD]
            outs.append((o + gate_col(gates, 1, hh) * o_s[rs] + gate_col(gates, 2, hh) * o_w[rs]).astype(BF16))
        o_ref[t0:t0 + tq, :] = jnp.concatenate(outs, axis=1)

    for first in range(0, seq // tq, TILES_PER_REGION):
        @pl.when(g > -1 - first)
        def _(first=first):
            for i in range(first, first + TILES_PER_REGION):
                attend(i)


def _overlap_matrix():
    c = jnp.arange(N_CMP_PAD)[None, :]
    j = jnp.arange(N_SEL_BLOCKS)[:, None]
    lo = (CMP_STRIDE * c) // SEL_BLOCK
    hi = (CMP_STRIDE * c + CMP_BLOCK - 1) // SEL_BLOCK
    return ((j == lo) | (j == hi)).astype(BF16)


def _nsa(qn, ks, vs, kw, vw, kc, vc, gates):
    B, T, _ = qn.shape
    gw = NSA_HPG * HEAD_PAD
    q_spec = pl.BlockSpec((None, T, gw), lambda b, g: (b, 0, g))
    kv_spec = pl.BlockSpec((None, T, HEAD_PAD), lambda b, g: (b, 0, g))
    c_spec = pl.BlockSpec((None, None, N_CMP_PAD, HEAD_PAD), lambda b, g: (b, g, 0, 0))
    return pl.pallas_call(
        _nsa_kernel,
        grid=(B, NSA_GROUPS),
        in_specs=[q_spec, kv_spec, kv_spec, kv_spec, kv_spec, c_spec, c_spec,
                  pl.BlockSpec((None, T, 128), lambda b, g: (b, 0, 0)),
                  pl.BlockSpec((N_SEL_BLOCKS, N_CMP_PAD), lambda b, g: (0, 0))],
        out_specs=q_spec,
        out_shape=jax.ShapeDtypeStruct((B, T, NSA_HEADS * HEAD_PAD), BF16),
        scratch_shapes=[pltpu.VMEM((T, HEAD_PAD), BF16), pltpu.VMEM((T, gw), F32),
                        pltpu.VMEM((2, NSA_HPG * TQ_NSA, T), F32)],
        compiler_params=pltpu.CompilerParams(
            dimension_semantics=("arbitrary", "arbitrary"), vmem_limit_bytes=VMEM_LIMIT),
        name="nsa",
    )(qn, ks, vs, kw, vw, kc, vc, gates, _overlap_matrix())


def _fox_kernel(q_ref, k_ref, v_ref, o_ref):
    seq = q_ref.shape[0]
    ck = CK_ATTN
    row = lax.broadcasted_iota(I32, (ck, 1), 0)
    col = lax.broadcasted_iota(I32, (1, ck), 1)
    for qi in range(seq // ck):
        q = q_ref[qi * ck:(qi + 1) * ck, :]
        s_chunks = []
        mx = None
        for c in range(qi + 1):
            s = _dot_nt(q, k_ref[c * ck:(c + 1) * ck, :])
            if c == qi:
                s = jnp.where(row >= col, s, NEG_BIG)
            s_chunks.append(s)
            mx = _fold_blocks(jnp.maximum, mx, s)
        m = jnp.max(mx, axis=-1, keepdims=True)
        ls = None
        acc = None
        for c in range(qi + 1):
            p = jnp.exp(s_chunks[c] - m)
            ls = _fold_blocks(jnp.add, ls, p)
            pv = _dot(p.astype(BF16), v_ref[c * ck:(c + 1) * ck, :])
            acc = pv if acc is None else acc + pv
        o_ref[qi * ck:(qi + 1) * ck, :] = (acc / jnp.sum(ls, axis=-1, keepdims=True)).astype(BF16)


def _fox(qf, kf, vf):
    B, T, _ = qf.shape
    spec = pl.BlockSpec((None, T, HEAD_PAD), lambda b, h: (b, 0, h))
    return pl.pallas_call(
        _fox_kernel,
        grid=(B, FOX_HEADS),
        in_specs=[spec, spec, spec],
        out_specs=spec,
        out_shape=jax.ShapeDtypeStruct((B, T, FOX_HEADS * HEAD_PAD), BF16),
        compiler_params=pltpu.CompilerParams(
            dimension_semantics=("arbitrary", "arbitrary"), vmem_limit_bytes=VMEM_LIMIT),
        name="fox",
    )(qf, kf, vf)


def _merge_out_kernel(x_ref, g_ref, on_ref, of_ref, wm_ref, bm_ref, wun_ref, wuf_ref, wo_ref, o_ref):
    x = x_ref[...]
    h = (_rms(x) * g_ref[...]).astype(BF16)
    merge = jax.nn.sigmoid(_dot(h, wm_ref[...]) + bm_ref[...])
    y = merge[:, :D_MODEL] * _dot(on_ref[...], wun_ref[...]) + merge[:, D_MODEL:] * _dot(of_ref[...], wuf_ref[...])
    o_ref[...] = x + _dot(y.astype(BF16), wo_ref[...])


def _merge_out(x2, g_attn, o_nsa, o_fox, w_merge, b_merge, w_up_nsa, w_up_fox, w_out):
    N, D = x2.shape
    tm = TM_OUT

    def pad_rows(w, n_heads):
        return _pad_heads(w.T, n_heads).T.astype(BF16)

    def tok(width):
        return pl.BlockSpec((tm, width), lambda i: (i, 0))

    def full(shape):
        return pl.BlockSpec(shape, lambda i: (0,) * len(shape))

    hw = NSA_HEADS * HEAD_PAD
    return pl.pallas_call(
        _merge_out_kernel,
        grid=(N // tm,),
        in_specs=[tok(D), full((1, D)), tok(hw), tok(hw), full((D, 2 * D)), full((1, 2 * D)),
                  full((hw, D)), full((hw, D)), full((D, D))],
        out_specs=tok(D),
        out_shape=jax.ShapeDtypeStruct((N, D), F32),
        compiler_params=pltpu.CompilerParams(dimension_semantics=("arbitrary",), vmem_limit_bytes=VMEM_LIMIT),
        name="merge_out",
    )(x2, g_attn.reshape(1, D), o_nsa, o_fox, w_merge.astype(BF16), b_merge.reshape(1, 2 * D),
      pad_rows(w_up_nsa, NSA_HEADS), pad_rows(w_up_fox, FOX_HEADS), w_out.astype(BF16))


def _router_kernel(x_ref, g_ref, wh_ref, wm_ref, br_ref, code_ref, wgt_ref, cnt_ref, carry_ref):
    tm = x_ref.shape[0]

    @pl.when(pl.program_id(0) == 0)
    def _():
        carry_ref[...] = jnp.zeros_like(carry_ref)

    h = _rms(x_ref[...]) * g_ref[...]
    h_hi = h.astype(BF16)
    h_mid = (h - h_hi.astype(F32)).astype(BF16)
    logits = _dot(h_hi, wh_ref[...]) + _dot(h_hi, wm_ref[...]) + _dot(h_mid, wh_ref[...]) + br_ref[...]
    lane = lax.broadcasted_iota(I32, (1, 128), 1)
    lane_f = lane.astype(F32)
    work = jnp.where(lane < N_EXPERTS, logits, -jnp.inf)
    vals, hots = [], []
    idx_out = jnp.zeros((tm, 128), F32)
    for k in range(TOP_K):
        m = jnp.max(work, axis=-1, keepdims=True)
        idx = jnp.min(jnp.where(work == m, lane_f, 128.0), axis=-1, keepdims=True)
        hot = lane_f == idx
        work = jnp.where(hot, -jnp.inf, work)
        vals.append(m)
        hots.append(hot)
        idx_out = idx_out + jnp.where(lane == k, idx, 0.0)
    e = [jnp.exp(v - vals[0]) for v in vals]
    denom = e[0] + e[1] + e[2] + e[3]
    wgt = jnp.zeros((tm, 128), F32)
    for k in range(TOP_K):
        wgt = wgt + jnp.where(lane == k, e[k] / denom, 0.0)

    multi = (hots[0] | hots[1] | hots[2] | hots[3]).astype(F32)
    r = lax.broadcasted_iota(I32, (tm, tm), 0)
    c = lax.broadcasted_iota(I32, (tm, tm), 1)
    before = _dot((r > c).astype(BF16), multi.astype(BF16)) + carry_ref[...]
    rank = jnp.zeros((tm, 128), F32)
    for k in range(TOP_K):
        rk = jnp.sum(jnp.where(hots[k], before, 0.0), axis=-1, keepdims=True)
        rank = rank + jnp.where(lane == k, rk, 0.0)
    total = carry_ref[...] + jnp.sum(multi, axis=0, keepdims=True)
    carry_ref[...] = total
    code_ref[...] = idx_out.astype(I32) * RANK_RADIX + rank.astype(I32)
    wgt_ref[...] = wgt
    cnt_ref[...] = total.astype(I32)


def _router(x1, g_mlp, w_router, b_router):
    N, D = x1.shape
    tm = TM_ROUTE
    assert N <= RANK_RADIX
    wr = jnp.pad(w_router, ((0, 0), (0, 128 - N_EXPERTS)))
    wr_hi = wr.astype(BF16)
    wr_mid = (wr - wr_hi.astype(F32)).astype(BF16)
    br = jnp.pad(b_router, (0, 128 - N_EXPERTS)).reshape(1, 128)
    tok = pl.BlockSpec((tm, 128), lambda i: (i, 0))
    one = pl.BlockSpec((1, 128), lambda i: (0, 0))
    wsp = pl.BlockSpec((D, 128), lambda i: (0, 0))
    return pl.pallas_call(
        _router_kernel,
        grid=(N // tm,),
        in_specs=[pl.BlockSpec((tm, D), lambda i: (i, 0)), pl.BlockSpec((1, D), lambda i: (0, 0)), wsp, wsp, one],
        out_specs=[tok, tok, one],
        out_shape=[jax.ShapeDtypeStruct((N, 128), I32), jax.ShapeDtypeStruct((N, 128), F32),
                   jax.ShapeDtypeStruct((1, 128), I32)],
        scratch_shapes=[pltpu.VMEM((1, 128), F32)],
        compiler_params=pltpu.CompilerParams(dimension_semantics=("arbitrary",), vmem_limit_bytes=VMEM_LIMIT),
        name="router",
    )(x1, g_mlp.reshape(1, D), wr_hi, wr_mid, br)


def _row_copy(src_ref, src_row, dst_ref, dst_row, sem):
    return pltpu.make_async_copy(src_ref.at[pl.ds(src_row, 1), :], dst_ref.at[pl.ds(dst_row, 1), :], sem)


def _rows_copy(src_ref, dst_ref, n_rows, sem):
    return pltpu.make_async_copy(src_ref.at[pl.ds(0, n_rows), :], dst_ref.at[pl.ds(0, n_rows), :], sem)


def _dispatch_kernel(pad_lo_ref, pad_hi_ref, pos_ref, x_ref, xs_ref, zero_ref, sem, zsem):
    tm = x_ref.shape[0]

    def issue(j, carry):
        for u in range(ROWS_PER_ISSUE):
            r = j * ROWS_PER_ISSUE + u
            for k in range(TOP_K):
                _row_copy(x_ref, r, xs_ref, pos_ref[r * TOP_K + k], sem).start(priority=k % 2)
        return carry

    lax.fori_loop(0, tm // ROWS_PER_ISSUE, issue, 0)

    @pl.when(pl.program_id(0) == pl.num_programs(0) - 1)
    def _():
        zero_ref[...] = jnp.zeros_like(zero_ref)
        for e in range(N_EXPERTS):
            def pad(p, carry):
                _row_copy(zero_ref, 0, xs_ref, p, zsem).start()
                return carry
            lax.fori_loop(pad_lo_ref[e], pad_hi_ref[e], pad, 0)
        for e in range(N_EXPERTS):
            def pad_wait(p, carry):
                _row_copy(zero_ref, 0, xs_ref, 0, zsem).wait()
                return carry
            lax.fori_loop(pad_lo_ref[e], pad_hi_ref[e], pad_wait, 0)

    for k in range(TOP_K):
        _rows_copy(x_ref, xs_ref, tm, sem).wait()


def _dispatch(x1, pos_flat, pad_lo, pad_hi, n_rows):
    N, D = x1.shape
    tm = TM_DISPATCH
    return pl.pallas_call(
        _dispatch_kernel,
        grid_spec=pltpu.PrefetchScalarGridSpec(
            num_scalar_prefetch=2,
            grid=(N // tm,),
            in_specs=[pl.BlockSpec((tm * TOP_K,), lambda i, lo, hi: (i,), memory_space=pltpu.SMEM),
                      pl.BlockSpec((tm, D), lambda i, lo, hi: (i, 0))],
            out_specs=pl.BlockSpec(memory_space=pl.ANY),
            scratch_shapes=[pltpu.VMEM((8, D), F32), pltpu.SemaphoreType.DMA, pltpu.SemaphoreType.DMA]),
        out_shape=jax.ShapeDtypeStruct((n_rows, D), F32),
        compiler_params=pltpu.CompilerParams(dimension_semantics=("arbitrary",), vmem_limit_bytes=VMEM_LIMIT),
        name="dispatch",
    )(pad_lo, pad_hi, pos_flat, x1)


def _experts_kernel(te_ref, nv_ref, xs_ref, g_ref, wg_ref, bg_ref, wu_ref, bu_ref, wd_ref, bd_ref, ys_ref):
    i = pl.program_id(0)

    @pl.when(i < nv_ref[0])
    def _():
        h = (_rms(xs_ref[...]) * g_ref[...]).astype(BF16)
        gate = jnp.minimum(_dot(h, wg_ref[...]) + bg_ref[...], SWIGLU_LIMIT)
        up = jnp.clip(_dot(h, wu_ref[...]) + bu_ref[...], -SWIGLU_LIMIT, SWIGLU_LIMIT)
        act = (up + 1.0) * gate * jax.nn.sigmoid(SWIGLU_ALPHA * gate)
        ys_ref[...] = _dot(act.astype(BF16), wd_ref[...]) + bd_ref[...]

    @pl.when(i >= nv_ref[0])
    def _():
        ys_ref[...] = jnp.zeros_like(ys_ref)


def _experts(xs, tile_expert, n_valid, g_mlp, w_gate, b_gate, w_up, b_up, w_down, b_down):
    P, D = xs.shape
    tm = TM_EXPERT
    E, _, F = w_gate.shape
    w_spec = pl.BlockSpec((None, D, F), lambda i, te, nv: (te[i], 0, 0))
    wd_spec = pl.BlockSpec((None, F, D), lambda i, te, nv: (te[i], 0, 0))
    b_spec = pl.BlockSpec((None, 1, F), lambda i, te, nv: (te[i], 0, 0))
    bd_spec = pl.BlockSpec((None, 1, D), lambda i, te, nv: (te[i], 0, 0))
    x_spec = pl.BlockSpec((tm, D), lambda i, te, nv: (jnp.minimum(i, nv[0] - 1), 0))
    return pl.pallas_call(
        _experts_kernel,
        grid_spec=pltpu.PrefetchScalarGridSpec(
            num_scalar_prefetch=2,
            grid=(P // tm,),
            in_specs=[x_spec, pl.BlockSpec((1, D), lambda i, te, nv: (0, 0)),
                      w_spec, b_spec, w_spec, b_spec, wd_spec, bd_spec],
            out_specs=pl.BlockSpec((tm, D), lambda i, te, nv: (i, 0))),
        out_shape=jax.ShapeDtypeStruct((P, D), F32),
        compiler_params=pltpu.CompilerParams(dimension_semantics=("arbitrary",), vmem_limit_bytes=VMEM_LIMIT),
        name="experts",
    )(tile_expert, n_valid, xs, g_mlp.reshape(1, D),
      w_gate.astype(BF16), b_gate.reshape(E, 1, F), w_up.astype(BF16), b_up.reshape(E, 1, F),
      w_down.astype(BF16), b_down.reshape(E, 1, D))


def _combine_kernel(pos_ref, x_ref, wgt_ref, ys_ref, o_ref, buf_ref, sem):
    tm = x_ref.shape[0]

    def issue(j, carry):
        for u in range(ROWS_PER_ISSUE):
            r = j * ROWS_PER_ISSUE + u
            for k in range(TOP_K):
                _row_copy(ys_ref, pos_ref[r * TOP_K + k], buf_ref.at[k], r, sem).start(priority=k % 2)
        return carry

    lax.fori_loop(0, tm // ROWS_PER_ISSUE, issue, 0)
    for k in range(TOP_K):
        _rows_copy(ys_ref, buf_ref.at[k], tm, sem).wait()
    wgt = wgt_ref[...]
    acc = x_ref[...]
    for k in range(TOP_K):
        acc = acc + wgt[:, k:k + 1] * buf_ref[k]
    o_ref[...] = acc


def _combine(x1, wgt, pos_flat, ys):
    N, D = x1.shape
    tm = TM_COMBINE
    return pl.pallas_call(
        _combine_kernel,
        grid=(N // tm,),
        in_specs=[pl.BlockSpec((tm * TOP_K,), lambda i: (i,), memory_space=pltpu.SMEM),
                  pl.BlockSpec((tm, D), lambda i: (i, 0)),
                  pl.BlockSpec((tm, 128), lambda i: (i, 0)),
                  pl.BlockSpec(memory_space=pl.ANY)],
        out_specs=pl.BlockSpec((tm, D), lambda i: (i, 0)),
        out_shape=jax.ShapeDtypeStruct((N, D), F32),
        scratch_shapes=[pltpu.VMEM((TOP_K, tm, D), F32), pltpu.SemaphoreType.DMA],
        compiler_params=pltpu.CompilerParams(dimension_semantics=("arbitrary",), vmem_limit_bytes=VMEM_LIMIT),
        name="combine",
    )(pos_flat, x1, wgt, ys)


def _moe(x1, g_mlp, w_router, b_router, w_gate, b_gate, w_up, b_up, w_down, b_down):
    N, D = x1.shape
    code, wgt, cnt = _router(x1, g_mlp, w_router, b_router)
    tm = TM_EXPERT
    n_tiles = -(-(N * TOP_K + N_EXPERTS * (tm - 1)) // tm)
    counts = cnt[0, :N_EXPERTS]
    padded = ((counts + tm - 1) // tm) * tm
    ends = jnp.cumsum(padded)
    starts = ends - padded
    code = code[:, :TOP_K]
    pos = (starts[code // RANK_RADIX] + code % RANK_RADIX).reshape(-1).astype(I32)
    tile_start = jnp.arange(n_tiles, dtype=I32) * tm
    tile_expert = jnp.minimum(jnp.sum(ends[None, :] <= tile_start[:, None], axis=1), N_EXPERTS - 1).astype(I32)
    n_valid = (ends[-1] // tm).astype(I32).reshape(1)
    xs = _dispatch(x1, pos, (starts + counts).astype(I32), ends.astype(I32), n_tiles * tm)
    ys = _experts(xs, tile_expert, n_valid, g_mlp, w_gate, b_gate, w_up, b_up, w_down, b_down)
    return _combine(x1, wgt, pos, ys)


def _layer(x, g_attn, w_in, qn_nsa, kn_cmp, kn_sel, kn_win,
           pos_cmp_k, w_ck1, b_ck1, w_ck2, b_ck2, pos_cmp_v, w_cv1, b_cv1, w_cv2, b_cv2,
           b_nsa_gate, qn_fox, kn_fox, b_forget, w_up_nsa, w_up_fox, b_merge, w_out, g_mlp,
           w_router, b_router, w_e_gate, b_e_gate, w_e_up, b_e_up, w_e_down, b_e_down):
    B, T, D = x.shape
    qn, ks, vs, kw, vw, kc_raw, vc_raw, qf, kf, vf, gates = _in_proj(
        x, g_attn, w_in, qn_nsa, kn_sel, kn_win, b_nsa_gate, qn_fox, kn_fox, b_forget)
    kc, vc = _compress(kc_raw, vc_raw, pos_cmp_k, w_ck1, b_ck1, w_ck2, b_ck2,
                       pos_cmp_v, w_cv1, b_cv1, w_cv2, b_cv2, kn_cmp)
    o_nsa = _nsa(qn, ks, vs, kw, vw, kc, vc, gates)
    o_fox = _fox(qf, kf, vf)
    hw = NSA_HEADS * HEAD_PAD
    w_merge = w_in[:, w_in.shape[1] - 2 * D:]
    x1 = _merge_out(x.reshape(B * T, D), g_attn, o_nsa.reshape(B * T, hw), o_fox.reshape(B * T, hw),
                    w_merge, b_merge, w_up_nsa, w_up_fox, w_out)
    out = _moe(x1, g_mlp, w_router, b_router, w_e_gate, b_e_gate, w_e_up, b_e_up, w_e_down, b_e_down)
    return out.reshape(B, T, D)


def kernel(x, g_attn, w_in, qn_nsa, kn_cmp, kn_sel, kn_win, pos_cmp_k, w_ck1, b_ck1, w_ck2, b_ck2, pos_cmp_v, w_cv1, b_cv1, w_cv2, b_cv2, b_nsa_gate, qn_fox, kn_fox, b_forget, w_up_nsa, w_up_fox, b_merge, w_out, g_mlp, w_router, b_router, w_e_gate, b_e_gate, w_e_up, b_e_up, w_e_down, b_e_down):
    params = (g_attn, w_in, qn_nsa, kn_cmp, kn_sel, kn_win, pos_cmp_k, w_ck1, b_ck1, w_ck2, b_ck2,
              pos_cmp_v, w_cv1, b_cv1, w_cv2, b_cv2, b_nsa_gate, qn_fox, kn_fox, b_forget,
              w_up_nsa, w_up_fox, b_merge, w_out, g_mlp, w_router, b_router,
              w_e_gate, b_e_gate, w_e_up, b_e_up, w_e_down, b_e_down)
    for layer in range(g_attn.shape[0]):
        x = _layer(x, *[p[layer] for p in params])
    return x
```

```python
import functools

import jax
import jax.numpy as jnp
from jax import lax
from jax.experimental import pallas as pl
from jax.experimental.pallas import tpu as pltpu

F32 = jnp.float32
BF16 = jnp.bfloat16
I32 = jnp.int32

D_MODEL = 1024
HEAD_DIM = 64
HEAD_PAD = 128
NSA_HEADS = 8
NSA_GROUPS = 2
NSA_HPG = NSA_HEADS // NSA_GROUPS
FOX_HEADS = 8
CMP_BLOCK = 32
CMP_STRIDE = 16
CMP_HIDDEN = 256
N_CMP_PAD = 128
SEL_BLOCK = 64
SEL_TOPN = 16
N_SEL_BLOCKS = 32
WINDOW = 512
N_EXPERTS = 32
TOP_K = 4
SWIGLU_LIMIT = 7.0
SWIGLU_ALPHA = 1.702
RMS_EPS = 1e-6
NEG_BIG = -1e30
FORCED_SCORE = 1e9
QK_SCALE = HEAD_DIM ** -0.5

LANE_SEL = 64
LANE_POS = 96
LANE_FOX = 64

OFF_QN = 0
OFF_KS = OFF_QN + NSA_HEADS * HEAD_PAD
OFF_VS = OFF_KS + NSA_GROUPS * HEAD_PAD
OFF_KW = OFF_VS + NSA_GROUPS * HEAD_PAD
OFF_VW = OFF_KW + NSA_GROUPS * HEAD_PAD
OFF_KC = OFF_VW + NSA_GROUPS * HEAD_PAD
OFF_VC = OFF_KC + NSA_GROUPS * HEAD_DIM
OFF_QF = OFF_VC + NSA_GROUPS * HEAD_DIM
OFF_KF = OFF_QF + FOX_HEADS * HEAD_PAD
OFF_VF = OFF_KF + FOX_HEADS * HEAD_PAD
OFF_MISC = OFF_VF + FOX_HEADS * HEAD_PAD
W1_COLS = OFF_MISC + 128
MISC_GATES = 3 * NSA_HEADS

TM_PROJ = 256
TQ_NSA = 128
TQ_SELECT = 512
TILES_PER_REGION = 2
CK_ATTN = 256
TM_OUT = 256
TM_ROUTE = 256
TM_EXPERT = 512
TM_DISPATCH = 1024
TM_COMBINE = 512
ROWS_PER_ISSUE = 8
RANK_RADIX = 1 << 16
VMEM_LIMIT = 56 * 1024 * 1024


def _dot(a, b):
    return jnp.dot(a, b, preferred_element_type=F32)


def _dot_nt(a, b):
    return lax.dot_general(a, b, (((1,), (1,)), ((), ())), preferred_element_type=F32)


def _split3(v):
    hi = v.astype(BF16)
    r1 = v - hi.astype(F32)
    mid = r1.astype(BF16)
    lo = (r1 - mid.astype(F32)).astype(BF16)
    return hi, mid, lo


def _dot_exact_rhs(a_f32, b_bf16):
    hi, mid, lo = _split3(a_f32)
    return _dot(hi, b_bf16) + _dot(mid, b_bf16) + _dot(lo, b_bf16)


def _dot_exact_lhs(a_bf16, b_f32):
    hi, mid, lo = _split3(b_f32)
    return _dot(a_bf16, hi) + _dot(a_bf16, mid) + _dot(a_bf16, lo)


def _rms(x):
    return x * lax.rsqrt(jnp.mean(x * x, axis=-1, keepdims=True) + RMS_EPS)


def _head_rms(a, gain):
    ms = jnp.sum(a * a, axis=-1, keepdims=True) * (1.0 / HEAD_DIM)
    return a * lax.rsqrt(ms + RMS_EPS) * gain


def _pos_feat_k(pos, lane):
    hi = (256 * (pos >> 8)).astype(F32)
    lo = (pos & 255).astype(F32)
    return jnp.where(lane < LANE_POS + 2, 1.0, jnp.where(lane == LANE_POS + 2, hi, lo)) * (
        (lane >= LANE_POS) & (lane < LANE_POS + 4)).astype(F32)


def _in_proj_kernel(x_ref, g_ref, w_ref, gains_ref, bmisc_ref, pq_ref, pk_ref,
                    qn_ref, ks_ref, vs_ref, kw_ref, vw_ref, kc_ref, vc_ref,
                    qf_ref, kf_ref, vf_ref, gate_ref, carry_ref):
    tm = x_ref.shape[0]
    i = pl.program_id(1)
    h = (_rms(x_ref[...]) * g_ref[...]).astype(BF16)
    pos = i * tm + lax.broadcasted_iota(I32, (tm, 1), 0)
    lane = lax.broadcasted_iota(I32, (1, HEAD_PAD), 1)
    pos_hi = (256 * (pos >> 8)).astype(F32)
    pos_lo = (pos & 255).astype(F32)
    kfeat = _pos_feat_k(pos, lane)
    blk_onehot = ((lane - LANE_SEL) == (pos >> 6)).astype(F32)

    def seg(off, width):
        return _dot(h, w_ref[:, off:off + width])

    a = seg(OFF_QN, NSA_HEADS * HEAD_PAD)
    gq = gains_ref[0:1, :] * QK_SCALE
    for hd in range(NSA_HEADS):
        slope = 2.0 ** (-(hd + 1))
        qfeat = jnp.where(lane == LANE_POS, -slope * pos_hi,
                          jnp.where(lane == LANE_POS + 1, -slope * pos_lo,
                                    jnp.where((lane == LANE_POS + 2) | (lane == LANE_POS + 3), slope, 0.0)))
        sl = slice(hd * HEAD_PAD, (hd + 1) * HEAD_PAD)
        qn_ref[:, sl] = (_head_rms(a[:, sl], gq) + qfeat).astype(BF16)

    a = seg(OFF_KS, NSA_GROUPS * HEAD_PAD)
    for g in range(NSA_GROUPS):
        sl = slice(g * HEAD_PAD, (g + 1) * HEAD_PAD)
        ks_ref[:, sl] = (_head_rms(a[:, sl], gains_ref[1:2, :]) + kfeat + blk_onehot).astype(BF16)
    vs_ref[...] = seg(OFF_VS, NSA_GROUPS * HEAD_PAD).astype(BF16)
    a = seg(OFF_KW, NSA_GROUPS * HEAD_PAD)
    for g in range(NSA_GROUPS):
        sl = slice(g * HEAD_PAD, (g + 1) * HEAD_PAD)
        kw_ref[:, sl] = (_head_rms(a[:, sl], gains_ref[2:3, :]) + kfeat).astype(BF16)
    vw_ref[...] = seg(OFF_VW, NSA_GROUPS * HEAD_PAD).astype(BF16)
    kc_ref[...] = seg(OFF_KC, NSA_GROUPS * HEAD_DIM).astype(BF16)
    vc_ref[...] = seg(OFF_VC, NSA_GROUPS * HEAD_DIM).astype(BF16)

    misc = seg(OFF_MISC, 128) + bmisc_ref[...]
    is_gate = lane < MISC_GATES
    gate_ref[...] = jnp.where(is_gate, jax.nn.sigmoid(misc), 0.0)
    is_f = (lane >= MISC_GATES) & (lane < MISC_GATES + 3 * FOX_HEADS)
    logf = jnp.where(is_f, jax.nn.log_sigmoid(misc), 0.0)

    @pl.when(i == 0)
    def _():
        carry_ref[...] = jnp.zeros_like(carry_ref)

    r = lax.broadcasted_iota(I32, (tm, tm), 0)
    c = lax.broadcasted_iota(I32, (tm, tm), 1)
    tri = (r >= c).astype(BF16)
    csum = _dot_exact_lhs(tri, logf) + carry_ref[...]
    carry_ref[...] = csum[tm - 1:tm, :]
    c_hi, c_mid, c_lo = _split3(csum)
    c3 = jnp.where(lane < MISC_GATES + FOX_HEADS, c_hi,
                   jnp.where(lane < MISC_GATES + 2 * FOX_HEADS, c_mid, c_lo))
    c3 = jnp.where(is_f, c3, jnp.zeros_like(c3))
    ones_lo = ((lane >= LANE_FOX) & (lane < LANE_FOX + 3)).astype(F32)
    ones_hi = ((lane >= LANE_FOX + 3) & (lane < LANE_FOX + 6)).astype(F32)

    a = seg(OFF_QF, FOX_HEADS * HEAD_PAD)
    fq = _dot(c3, pq_ref[...])
    gq = gains_ref[3:4, :] * QK_SCALE
    for hd in range(FOX_HEADS):
        sl = slice(hd * HEAD_PAD, (hd + 1) * HEAD_PAD)
        qf_ref[:, sl] = (_head_rms(a[:, sl], gq) + fq[:, sl] + ones_hi).astype(BF16)
    a = seg(OFF_KF, FOX_HEADS * HEAD_PAD)
    fk = _dot(c3, pk_ref[...])
    for hd in range(FOX_HEADS):
        sl = slice(hd * HEAD_PAD, (hd + 1) * HEAD_PAD)
        kf_ref[:, sl] = (_head_rms(a[:, sl], gains_ref[4:5, :]) + fk[:, sl] + ones_lo).astype(BF16)
    vf_ref[...] = seg(OFF_VF, FOX_HEADS * HEAD_PAD).astype(BF16)


def _pad_heads(w, n_heads):
    lead = w.shape[:-1]
    w = w.reshape(lead + (n_heads, HEAD_DIM))
    w = jnp.pad(w, [(0, 0)] * len(lead) + [(0, 0), (0, HEAD_PAD - HEAD_DIM)])
    return w.reshape(lead + (n_heads * HEAD_PAD,))


def _pad_gain(g):
    return jnp.pad(g.astype(F32), (0, HEAD_PAD - HEAD_DIM))


def _fox_placement():
    pq = jnp.zeros((128, FOX_HEADS * HEAD_PAD), F32)
    pk = jnp.zeros((128, FOX_HEADS * HEAD_PAD), F32)
    for piece in range(3):
        for hd in range(FOX_HEADS):
            src = MISC_GATES + piece * FOX_HEADS + hd
            pq = pq.at[src, hd * HEAD_PAD + LANE_FOX + piece].set(1.0)
            pk = pk.at[src, hd * HEAD_PAD + LANE_FOX + 3 + piece].set(-1.0)
    return pq.astype(BF16), pk.astype(BF16)


def _in_proj(x, g_attn, w_in, qn_nsa, kn_sel, kn_win, b_nsa_gate, qn_fox, kn_fox, b_forget):
    B, T, D = x.shape
    s1 = NSA_HEADS * HEAD_DIM
    kv_w = NSA_GROUPS * HEAD_DIM
    s2 = s1 + 6 * kv_w
    s3 = s2 + 3 * NSA_HEADS
    fw = FOX_HEADS * HEAD_DIM
    s4 = s3 + 3 * fw
    s5 = s4 + FOX_HEADS
    kv = [w_in[:, s1 + j * kv_w: s1 + (j + 1) * kv_w] for j in range(6)]
    f_cols = w_in[:, s4:s5]
    misc = jnp.concatenate([w_in[:, s2:s3], f_cols, f_cols, f_cols,
                            jnp.zeros((D, 128 - MISC_GATES - 3 * FOX_HEADS), F32)], axis=1)
    w1 = jnp.concatenate([
        _pad_heads(w_in[:, :s1], NSA_HEADS),
        _pad_heads(kv[2], NSA_GROUPS), _pad_heads(kv[3], NSA_GROUPS),
        _pad_heads(kv[4], NSA_GROUPS), _pad_heads(kv[5], NSA_GROUPS),
        kv[0], kv[1],
        _pad_heads(w_in[:, s3:s3 + fw], FOX_HEADS),
        _pad_heads(w_in[:, s3 + fw:s3 + 2 * fw], FOX_HEADS),
        _pad_heads(w_in[:, s3 + 2 * fw:s4], FOX_HEADS),
        misc], axis=1).astype(BF16)
    assert w1.shape[1] == W1_COLS
    gains = jnp.stack([_pad_gain(qn_nsa), _pad_gain(kn_sel), _pad_gain(kn_win),
                       _pad_gain(qn_fox), _pad_gain(kn_fox),
                       jnp.zeros(HEAD_PAD, F32), jnp.zeros(HEAD_PAD, F32), jnp.zeros(HEAD_PAD, F32)])
    bmisc = jnp.concatenate([b_nsa_gate, b_forget, b_forget, b_forget,
                             jnp.zeros(128 - MISC_GATES - 3 * FOX_HEADS, F32)]).reshape(1, 128)
    pq, pk = _fox_placement()
    tm = TM_PROJ

    def tok(width):
        return pl.BlockSpec((None, tm, width), lambda b, i: (b, i, 0))

    def full(shape):
        return pl.BlockSpec(shape, lambda b, i: (0,) * len(shape))

    widths = [NSA_HEADS * HEAD_PAD] + [NSA_GROUPS * HEAD_PAD] * 4 + [NSA_GROUPS * HEAD_DIM] * 2 + \
             [FOX_HEADS * HEAD_PAD] * 3
    out_shape = [jax.ShapeDtypeStruct((B, T, w), BF16) for w in widths] + \
                [jax.ShapeDtypeStruct((B, T, 128), F32)]
    return pl.pallas_call(
        _in_proj_kernel,
        grid=(B, T // tm),
        in_specs=[tok(D), full((1, D)), full((D, W1_COLS)), full((8, HEAD_PAD)), full((1, 128)),
                  full((128, FOX_HEADS * HEAD_PAD)), full((128, FOX_HEADS * HEAD_PAD))],
        out_specs=[tok(w) for w in widths] + [tok(128)],
        out_shape=out_shape,
        scratch_shapes=[pltpu.VMEM((1, 128), F32)],
        compiler_params=pltpu.CompilerParams(
            dimension_semantics=("arbitrary", "arbitrary"), vmem_limit_bytes=VMEM_LIMIT),
        name="in_proj",
    )(x, g_attn.reshape(1, D), w1, gains, bmisc, pq, pk)


def _gelu_tanh(x):
    return 0.5 * x * (1.0 + jnp.tanh(0.7978845608028654 * (x + 0.044715 * (x * x * x))))


def _compress_kernel(ak_ref, av_ref, posk_ref, posv_ref, wk1_ref, bk1_ref, wk2_ref, bk2_ref,
                     wv1_ref, bv1_ref, wv2_ref, bv2_ref, gain_ref, kc_ref, vc_ref):
    row = lax.broadcasted_iota(I32, (N_CMP_PAD, 1), 0)
    lane = lax.broadcasted_iota(I32, (1, HEAD_PAD), 1)
    kfeat = _pos_feat_k(CMP_STRIDE * row + CMP_BLOCK - 1, lane)

    def mlp(a_ref, pos_ref, w1_ref, b1_ref, w2_ref, b2_ref):
        a = a_ref[...].astype(F32)
        lo = (a + pos_ref[0:1, :]).astype(BF16)
        hi = (a + pos_ref[1:2, :]).astype(BF16)
        outs = []
        for g in range(NSA_GROUPS):
            hid = _dot(lo, w1_ref[0, g]) + pltpu.roll(_dot(hi, w1_ref[1, g]), N_CMP_PAD - 1, 0) + b1_ref[...]
            outs.append(_dot(_gelu_tanh(hid).astype(BF16), w2_ref[...]) + b2_ref[...])
        return outs

    ks = mlp(ak_ref, posk_ref, wk1_ref, bk1_ref, wk2_ref, bk2_ref)
    vs = mlp(av_ref, posv_ref, wv1_ref, bv1_ref, wv2_ref, bv2_ref)
    for g in range(NSA_GROUPS):
        kc_ref[g] = (_head_rms(ks[g], gain_ref[...]) + kfeat).astype(BF16)
        vc_ref[g] = vs[g].astype(BF16)


def _compress(kc_raw, vc_raw, pos_k, w_ck1, b_ck1, w_ck2, b_ck2, pos_v, w_cv1, b_cv1, w_cv2, b_cv2, kn_cmp):
    B, T, _ = kc_raw.shape
    n_chunks = T // CMP_STRIDE
    assert n_chunks == N_CMP_PAD
    chunk_w = CMP_STRIDE * NSA_GROUPS * HEAD_DIM

    def chunks(a):
        return a.reshape(B, n_chunks, chunk_w)

    def w1_rows(w):
        w = w.reshape(2, CMP_STRIDE, 1, HEAD_DIM, CMP_HIDDEN)
        per_group = [jnp.concatenate([w if gg == g else jnp.zeros_like(w) for gg in range(NSA_GROUPS)], axis=2)
                     for g in range(NSA_GROUPS)]
        return jnp.stack(per_group, axis=1).reshape(2, NSA_GROUPS, chunk_w, CMP_HIDDEN).astype(BF16)

    def pos_rows(p):
        p = jnp.broadcast_to(p.reshape(2, CMP_STRIDE, 1, HEAD_DIM), (2, CMP_STRIDE, NSA_GROUPS, HEAD_DIM))
        return p.reshape(2, chunk_w)

    def w2pad(w, b):
        return (jnp.pad(w, ((0, 0), (0, HEAD_PAD - HEAD_DIM))).astype(BF16),
                jnp.pad(b, (0, HEAD_PAD - HEAD_DIM)).reshape(1, HEAD_PAD))

    wk2, bk2 = w2pad(w_ck2, b_ck2)
    wv2, bv2 = w2pad(w_cv2, b_cv2)
    a_spec = pl.BlockSpec((None, n_chunks, chunk_w), lambda b: (b, 0, 0))
    o_spec = pl.BlockSpec((None, NSA_GROUPS, N_CMP_PAD, HEAD_PAD), lambda b: (b, 0, 0, 0))

    def full(shape):
        return pl.BlockSpec(shape, lambda b: (0,) * len(shape))

    w1_shape = (2, NSA_GROUPS, chunk_w, CMP_HIDDEN)
    return pl.pallas_call(
        _compress_kernel,
        grid=(B,),
        in_specs=[a_spec, a_spec, full((2, chunk_w)), full((2, chunk_w)),
                  full(w1_shape), full((1, CMP_HIDDEN)), full((CMP_HIDDEN, HEAD_PAD)), full((1, HEAD_PAD)),
                  full(w1_shape), full((1, CMP_HIDDEN)), full((CMP_HIDDEN, HEAD_PAD)), full((1, HEAD_PAD)),
                  full((1, HEAD_PAD))],
        out_specs=[o_spec, o_spec],
        out_shape=[jax.ShapeDtypeStruct((B, NSA_GROUPS, N_CMP_PAD, HEAD_PAD), BF16)] * 2,
        compiler_params=pltpu.CompilerParams(dimension_semantics=("arbitrary",), vmem_limit_bytes=VMEM_LIMIT),
        name="compress",
    )(chunks(kc_raw), chunks(vc_raw), pos_rows(pos_k), pos_rows(pos_v),
      w1_rows(w_ck1), b_ck1.reshape(1, -1), wk2, bk2,
      w1_rows(w_cv1), b_cv1.reshape(1, -1), wv2, bv2, _pad_gain(kn_cmp).reshape(1, HEAD_PAD))


def _lane_blocks(a):
    return [a[:, i * 128:(i + 1) * 128] for i in range(a.shape[1] // 128)]


def _fold_blocks(op, acc, a):
    for blk in _lane_blocks(a):
        acc = blk if acc is None else op(acc, blk)
    return acc


def _nsa_kernel(q_ref, ks_ref, vs_ref, kw_ref, vw_ref, kc_ref, vc_ref, gate_ref, ovl_ref, o_ref,
                sel_ref, oc_ref, s_ref):
    seq = q_ref.shape[0]
    tq = TQ_NSA
    ta = TQ_SELECT
    ck = CK_ATTN
    span = WINDOW + tq
    g = pl.program_id(1)
    col_l = lax.broadcasted_iota(I32, (1, 128), 1)
    c_end_lane = CMP_STRIDE * col_l + (CMP_BLOCK - 1)
    c_end_sub = CMP_STRIDE * lax.broadcasted_iota(I32, (N_CMP_PAD, 1), 0) + (CMP_BLOCK - 1)
    blk_id = lax.broadcasted_iota(I32, (N_SEL_BLOCKS, 1), 0)
    sub_id = lax.broadcasted_iota(I32, (8, 1), 0)

    def gate_col(gates, branch, hh):
        lo = branch * NSA_HEADS + hh
        return jnp.where(g == 0, gates[:, lo:lo + 1], gates[:, lo + NSA_HPG:lo + NSA_HPG + 1])

    def select(a):
        a0 = a * ta
        heads = [q_ref[a0:a0 + ta, hh * HEAD_PAD:(hh + 1) * HEAD_PAD] for hh in range(NSA_HPG)]
        q4 = jnp.concatenate(heads, axis=0)
        t_row = a0 + (lax.broadcasted_iota(I32, (NSA_HPG * ta, 1), 0) & (ta - 1))
        t_lane = a0 + lax.broadcasted_iota(I32, (1, ta), 1)

        s = _dot_nt(q4, kc_ref[...])
        mask = t_row >= c_end_lane
        s = jnp.where(mask, s, NEG_BIG)
        p = jnp.where(mask, jnp.exp(s - jnp.max(s, axis=-1, keepdims=True)), 0.0)
        l = jnp.sum(p, axis=-1, keepdims=True)
        o_c = _dot((p / jnp.where(l > 0.0, l, 1.0)).astype(BF16), vc_ref[...])
        gates = gate_ref[a0:a0 + ta, :]
        for hh in range(NSA_HPG):
            oc_ref[a0:a0 + ta, hh * HEAD_PAD:(hh + 1) * HEAD_PAD] = gate_col(gates, 0, hh) * o_c[hh * ta:(hh + 1) * ta]

        mask_t = t_lane >= c_end_sub
        p_sum = None
        for hh in range(NSA_HPG):
            s_t = jnp.where(mask_t, _dot_nt(kc_ref[...], heads[hh]), NEG_BIG)
            e_t = jnp.where(mask_t, jnp.exp(s_t - jnp.max(s_t, axis=0, keepdims=True)), 0.0)
            l_t = jnp.sum(e_t, axis=0, keepdims=True)
            p_t = e_t / jnp.where(l_t > 0.0, l_t, 1.0)
            p_sum = p_t if p_sum is None else p_sum + p_t
        imp = _dot_exact_lhs(ovl_ref[...], p_sum)
        tb = t_lane >> 6
        valid = blk_id <= tb
        forced = (blk_id == 0) | (blk_id == tb) | (blk_id == tb - 1)
        score = jnp.where(forced & valid, FORCED_SCORE, imp)
        score = jnp.where(valid, score, -1.0)
        groups = [score[8 * b:8 * b + 8, :] for b in range(N_SEL_BLOCKS // 8)]
        n_above = [jnp.zeros((8, ta), F32) for _ in groups]
        for j in range(N_SEL_BLOCKS):
            cj = score[j:j + 1, :]
            for b, grp in enumerate(groups):
                if 8 * b > j:
                    n_above[b] = n_above[b] + jnp.where(cj >= grp, 1.0, 0.0)
                elif 8 * b + 7 < j:
                    n_above[b] = n_above[b] + jnp.where(cj > grp, 1.0, 0.0)
                else:
                    tie = jnp.where(sub_id + 8 * b > j, 1.0, 0.0)
                    n_above[b] = n_above[b] + jnp.where(cj > grp, 1.0, 0.0) + jnp.where(cj == grp, tie, 0.0)
        dropped = jnp.where(jnp.concatenate(n_above, axis=0) >= float(SEL_TOPN), NEG_BIG, 0.0)
        drop_t = jnp.concatenate([jnp.zeros((LANE_SEL, ta), F32), dropped,
                                  jnp.zeros((HEAD_PAD - LANE_SEL - N_SEL_BLOCKS, ta), F32)], axis=0)
        for blk in range(ta // 128):
            sel_ref[a0 + blk * 128:a0 + (blk + 1) * 128, :] = drop_t[:, blk * 128:(blk + 1) * 128].T.astype(BF16)

    for a in range(seq // ta):
        pl.when(g >= -a)(functools.partial(select, a))

    rows = NSA_HPG * tq
    row_l = lax.broadcasted_iota(I32, (rows, 1), 0) & (tq - 1)

    def attend(i):
        t0 = i * tq
        t_row = t0 + row_l
        q4 = jnp.concatenate([q_ref[t0:t0 + tq, hh * HEAD_PAD:(hh + 1) * HEAD_PAD] for hh in range(NSA_HPG)], axis=0)
        q4s = q4 + jnp.concatenate([sel_ref[t0:t0 + tq, :]] * NSA_HPG, axis=0)

        n_chunks = (t0 + tq - 1) // ck + 1
        slot = i % 2
        mx = None
        for c in range(n_chunks):
            s = _dot_nt(q4s, ks_ref[c * ck:(c + 1) * ck, :])
            if (c + 1) * ck > t0:
                s = jnp.where(t_row >= c * ck + lax.broadcasted_iota(I32, (1, ck), 1), s, NEG_BIG)
            s_ref[slot, :, c * ck:(c + 1) * ck] = s
            mx = _fold_blocks(jnp.maximum, mx, s)
        m = jnp.max(mx, axis=-1, keepdims=True)
        ls = None
        acc = None
        for c in range(n_chunks):
            p = jnp.exp(s_ref[slot, :, c * ck:(c + 1) * ck] - m)
            ls = _fold_blocks(jnp.add, ls, p)
            pv = _dot(p.astype(BF16), vs_ref[c * ck:(c + 1) * ck, :])
            acc = pv if acc is None else acc + pv
        o_s = acc / jnp.sum(ls, axis=-1, keepdims=True)

        ws = max(t0 - WINDOW, 0)
        blocks = _lane_blocks(_dot_nt(q4, kw_ref[ws:ws + span, :]))
        if t0 < WINDOW:
            blocks = [jnp.where(t_row >= ws + jb * 128 + col_l, blk, NEG_BIG) if ws + (jb + 1) * 128 > t0 else blk
                      for jb, blk in enumerate(blocks)]
        else:
            blocks[0] = jnp.where(col_l > row_l, blocks[0], NEG_BIG)
            blocks[-1] = jnp.where(col_l <= row_l, blocks[-1], NEG_BIG)
        s = jnp.concatenate(blocks, axis=1)
        m = jnp.max(_fold_blocks(jnp.maximum, None, s), axis=-1, keepdims=True)
        p = jnp.exp(s - m)
        l = jnp.sum(_fold_blocks(jnp.add, None, p), axis=-1, keepdims=True)
        o_w = _dot(p.astype(BF16), vw_ref[ws:ws + span, :]) / l

        gates = gate_ref[t0:t0 + tq, :]
        outs = []
        for hh in range(NSA_HPG):
            rs = slice(hh * tq, (hh + 1) * tq)
            o = oc_ref[t0:t0 + tq, hh * HEAD_PAD:(hh + 1) * HEAD_PAD]
            outs.append((o + gate_col(gates, 1, hh) * o_s[rs] + gate_col(gates, 2, hh) * o_w[rs]).astype(BF16))
        o_ref[t0:t0 + tq, :] = jnp.concatenate(outs, axis=1)

    for first in range(0, seq // tq, TILES_PER_REGION):
        @pl.when(g > -1 - first)
        def _(first=first):
            for i in range(first, first + TILES_PER_REGION):
                attend(i)


def _overlap_matrix():
    c = jnp.arange(N_CMP_PAD)[None, :]
    j = jnp.arange(N_SEL_BLOCKS)[:, None]
    lo = (CMP_STRIDE * c) // SEL_BLOCK
    hi = (CMP_STRIDE * c + CMP_BLOCK - 1) // SEL_BLOCK
    return ((j == lo) | (j == hi)).astype(BF16)


def _nsa(qn, ks, vs, kw, vw, kc, vc, gates):
    B, T, _ = qn.shape
    gw = NSA_HPG * HEAD_PAD
    q_spec = pl.BlockSpec((None, T, gw), lambda b, g: (b, 0, g))
    kv_spec = pl.BlockSpec((None, T, HEAD_PAD), lambda b, g: (b, 0, g))
    c_spec = pl.BlockSpec((None, None, N_CMP_PAD, HEAD_PAD), lambda b, g: (b, g, 0, 0))
    return pl.pallas_call(
        _nsa_kernel,
        grid=(B, NSA_GROUPS),
        in_specs=[q_spec, kv_spec, kv_spec, kv_spec, kv_spec, c_spec, c_spec,
                  pl.BlockSpec((None, T, 128), lambda b, g: (b, 0, 0)),
                  pl.BlockSpec((N_SEL_BLOCKS, N_CMP_PAD), lambda b, g: (0, 0))],
        out_specs=q_spec,
        out_shape=jax.ShapeDtypeStruct((B, T, NSA_HEADS * HEAD_PAD), BF16),
        scratch_shapes=[pltpu.VMEM((T, HEAD_PAD), BF16), pltpu.VMEM((T, gw), F32),
                        pltpu.VMEM((2, NSA_HPG * TQ_NSA, T), F32)],
        compiler_params=pltpu.CompilerParams(
            dimension_semantics=("arbitrary", "arbitrary"), vmem_limit_bytes=VMEM_LIMIT),
        name="nsa",
    )(qn, ks, vs, kw, vw, kc, vc, gates, _overlap_matrix())


def _fox_kernel(q_ref, k_ref, v_ref, o_ref):
    seq = q_ref.shape[0]
    ck = CK_ATTN
    row = lax.broadcasted_iota(I32, (ck, 1), 0)
    col = lax.broadcasted_iota(I32, (1, ck), 1)
    for qi in range(seq // ck):
        q = q_ref[qi * ck:(qi + 1) * ck, :]
        s_chunks = []
        mx = None
        for c in range(qi + 1):
            s = _dot_nt(q, k_ref[c * ck:(c + 1) * ck, :])
            if c == qi:
                s = jnp.where(row >= col, s, NEG_BIG)
            s_chunks.append(s)
            mx = _fold_blocks(jnp.maximum, mx, s)
        m = jnp.max(mx, axis=-1, keepdims=True)
        ls = None
        acc = None
        for c in range(qi + 1):
            p = jnp.exp(s_chunks[c] - m)
            ls = _fold_blocks(jnp.add, ls, p)
            pv = _dot(p.astype(BF16), v_ref[c * ck:(c + 1) * ck, :])
            acc = pv if acc is None else acc + pv
        o_ref[qi * ck:(qi + 1) * ck, :] = (acc / jnp.sum(ls, axis=-1, keepdims=True)).astype(BF16)


def _fox(qf, kf, vf):
    B, T, _ = qf.shape
    spec = pl.BlockSpec((None, T, HEAD_PAD), lambda b, h: (b, 0, h))
    return pl.pallas_call(
        _fox_kernel,
        grid=(B, FOX_HEADS),
        in_specs=[spec, spec, spec],
        out_specs=spec,
        out_shape=jax.ShapeDtypeStruct((B, T, FOX_HEADS * HEAD_PAD), BF16),
        compiler_params=pltpu.CompilerParams(
            dimension_semantics=("arbitrary", "arbitrary"), vmem_limit_bytes=VMEM_LIMIT),
        name="fox",
    )(qf, kf, vf)


def _merge_out_kernel(x_ref, g_ref, on_ref, of_ref, wm_ref, bm_ref, wun_ref, wuf_ref, wo_ref,
                      gm_ref, wrh_ref, wrm_ref, br_ref, o_ref, code_ref, wgt_ref, cnt_ref, carry_ref):
    x = x_ref[...]
    h = (_rms(x) * g_ref[...]).astype(BF16)
    merge = jax.nn.sigmoid(_dot(h, wm_ref[...]) + bm_ref[...])
    y = merge[:, :D_MODEL] * _dot(on_ref[...], wun_ref[...]) + merge[:, D_MODEL:] * _dot(of_ref[...], wuf_ref[...])
    x1 = x + _dot(y.astype(BF16), wo_ref[...])
    o_ref[...] = x1
    _route_tile(x1, gm_ref, wrh_ref, wrm_ref, br_ref, code_ref, wgt_ref, cnt_ref, carry_ref)


def _merge_out(x2, g_attn, o_nsa, o_fox, w_merge, b_merge, w_up_nsa, w_up_fox, w_out, g_mlp, w_router, b_router):
    N, D = x2.shape
    tm = TM_OUT
    assert N <= RANK_RADIX
    wr = jnp.pad(w_router, ((0, 0), (0, 128 - N_EXPERTS)))
    wr_hi = wr.astype(BF16)
    wr_mid = (wr - wr_hi.astype(F32)).astype(BF16)
    br = jnp.pad(b_router, (0, 128 - N_EXPERTS)).reshape(1, 128)

    def pad_rows(w, n_heads):
        return _pad_heads(w.T, n_heads).T.astype(BF16)

    def tok(width):
        return pl.BlockSpec((tm, width), lambda i: (i, 0))

    def full(shape):
        return pl.BlockSpec(shape, lambda i: (0,) * len(shape))

    hw = NSA_HEADS * HEAD_PAD
    return pl.pallas_call(
        _merge_out_kernel,
        grid=(N // tm,),
        in_specs=[tok(D), full((1, D)), tok(hw), tok(hw), full((D, 2 * D)), full((1, 2 * D)),
                  full((hw, D)), full((hw, D)), full((D, D)),
                  full((1, D)), full((D, 128)), full((D, 128)), full((1, 128))],
        out_specs=[tok(D), tok(128), tok(128), full((1, 128))],
        out_shape=[jax.ShapeDtypeStruct((N, D), F32), jax.ShapeDtypeStruct((N, 128), I32),
                   jax.ShapeDtypeStruct((N, 128), F32), jax.ShapeDtypeStruct((1, 128), I32)],
        scratch_shapes=[pltpu.VMEM((1, 128), F32)],
        compiler_params=pltpu.CompilerParams(dimension_semantics=("arbitrary",), vmem_limit_bytes=VMEM_LIMIT),
        name="merge_out",
    )(x2, g_attn.reshape(1, D), o_nsa, o_fox, w_merge.astype(BF16), b_merge.reshape(1, 2 * D),
      pad_rows(w_up_nsa, NSA_HEADS), pad_rows(w_up_fox, FOX_HEADS), w_out.astype(BF16),
      g_mlp.reshape(1, D), wr_hi, wr_mid, br)


def _route_tile(x, g_ref, wh_ref, wm_ref, br_ref, code_ref, wgt_ref, cnt_ref, carry_ref):
    tm = x.shape[0]

    @pl.when(pl.program_id(0) == 0)
    def _():
        carry_ref[...] = jnp.zeros_like(carry_ref)

    h = _rms(x) * g_ref[...]
    h_hi = h.astype(BF16)
    h_mid = (h - h_hi.astype(F32)).astype(BF16)
    logits = _dot(h_hi, wh_ref[...]) + _dot(h_hi, wm_ref[...]) + _dot(h_mid, wh_ref[...]) + br_ref[...]
    lane = lax.broadcasted_iota(I32, (1, 128), 1)
    lane_f = lane.astype(F32)
    work = jnp.where(lane < N_EXPERTS, logits, -jnp.inf)
    vals, hots = [], []
    idx_out = jnp.zeros((tm, 128), F32)
    for k in range(TOP_K):
        m = jnp.max(work, axis=-1, keepdims=True)
        idx = jnp.min(jnp.where(work == m, lane_f, 128.0), axis=-1, keepdims=True)
        hot = lane_f == idx
        work = jnp.where(hot, -jnp.inf, work)
        vals.append(m)
        hots.append(hot)
        idx_out = idx_out + jnp.where(lane == k, idx, 0.0)
    e = [jnp.exp(v - vals[0]) for v in vals]
    denom = e[0] + e[1] + e[2] + e[3]
    wgt = jnp.zeros((tm, 128), F32)
    for k in range(TOP_K):
        wgt = wgt + jnp.where(lane == k, e[k] / denom, 0.0)

    multi = (hots[0] | hots[1] | hots[2] | hots[3]).astype(F32)
    r = lax.broadcasted_iota(I32, (tm, tm), 0)
    c = lax.broadcasted_iota(I32, (tm, tm), 1)
    before = _dot((r > c).astype(BF16), multi.astype(BF16)) + carry_ref[...]
    rank = jnp.zeros((tm, 128), F32)
    for k in range(TOP_K):
        rk = jnp.sum(jnp.where(hots[k], before, 0.0), axis=-1, keepdims=True)
        rank = rank + jnp.where(lane == k, rk, 0.0)
    total = carry_ref[...] + jnp.sum(multi, axis=0, keepdims=True)
    carry_ref[...] = total
    code_ref[...] = idx_out.astype(I32) * RANK_RADIX + rank.astype(I32)
    wgt_ref[...] = wgt
    cnt_ref[...] = total.astype(I32)


def _row_copy(src_ref, src_row, dst_ref, dst_row, sem):
    return pltpu.make_async_copy(src_ref.at[pl.ds(src_row, 1), :], dst_ref.at[pl.ds(dst_row, 1), :], sem)


def _rows_copy(src_ref, dst_ref, n_rows, sem):
    return pltpu.make_async_copy(src_ref.at[pl.ds(0, n_rows), :], dst_ref.at[pl.ds(0, n_rows), :], sem)


def _dispatch_kernel(pad_lo_ref, pad_hi_ref, pos_ref, x_ref, xs_ref, zero_ref, sem, zsem):
    tm = x_ref.shape[0]

    def issue(j, carry):
        for u in range(ROWS_PER_ISSUE):
            r = j * ROWS_PER_ISSUE + u
            for k in range(TOP_K):
                _row_copy(x_ref, r, xs_ref, pos_ref[r * TOP_K + k], sem).start(priority=k % 2)
        return carry

    lax.fori_loop(0, tm // ROWS_PER_ISSUE, issue, 0)

    @pl.when(pl.program_id(0) == pl.num_programs(0) - 1)
    def _():
        zero_ref[...] = jnp.zeros_like(zero_ref)
        for e in range(N_EXPERTS):
            def pad(p, carry):
                _row_copy(zero_ref, 0, xs_ref, p, zsem).start()
                return carry
            lax.fori_loop(pad_lo_ref[e], pad_hi_ref[e], pad, 0)
        for e in range(N_EXPERTS):
            def pad_wait(p, carry):
                _row_copy(zero_ref, 0, xs_ref, 0, zsem).wait()
                return carry
            lax.fori_loop(pad_lo_ref[e], pad_hi_ref[e], pad_wait, 0)

    for k in range(TOP_K):
        _rows_copy(x_ref, xs_ref, tm, sem).wait()


def _dispatch(x1, pos_flat, pad_lo, pad_hi, n_rows):
    N, D = x1.shape
    tm = TM_DISPATCH
    return pl.pallas_call(
        _dispatch_kernel,
        grid_spec=pltpu.PrefetchScalarGridSpec(
            num_scalar_prefetch=2,
            grid=(N // tm,),
            in_specs=[pl.BlockSpec((tm * TOP_K,), lambda i, lo, hi: (i,), memory_space=pltpu.SMEM),
                      pl.BlockSpec((tm, D), lambda i, lo, hi: (i, 0))],
            out_specs=pl.BlockSpec(memory_space=pl.ANY),
            scratch_shapes=[pltpu.VMEM((8, D), F32), pltpu.SemaphoreType.DMA, pltpu.SemaphoreType.DMA]),
        out_shape=jax.ShapeDtypeStruct((n_rows, D), F32),
        compiler_params=pltpu.CompilerParams(dimension_semantics=("arbitrary",), vmem_limit_bytes=VMEM_LIMIT),
        name="dispatch",
    )(pad_lo, pad_hi, pos_flat, x1)


def _experts_kernel(te_ref, nv_ref, xs_ref, g_ref, wg_ref, bg_ref, wu_ref, bu_ref, wd_ref, bd_ref, ys_ref,
                    wg_s, wu_s, wd_s):
    i = pl.program_id(0)

    @pl.when((i == 0) | (te_ref[i] != te_ref[jnp.maximum(i - 1, 0)]))
    def _():
        wg_s[...] = wg_ref[...].astype(BF16)
        wu_s[...] = wu_ref[...].astype(BF16)
        wd_s[...] = wd_ref[...].astype(BF16)

    @pl.when(i < nv_ref[0])
    def _():
        h = (_rms(xs_ref[...]) * g_ref[...]).astype(BF16)
        gate = jnp.minimum(_dot(h, wg_s[...]) + bg_ref[...], SWIGLU_LIMIT)
        up = jnp.clip(_dot(h, wu_s[...]) + bu_ref[...], -SWIGLU_LIMIT, SWIGLU_LIMIT)
        act = (up + 1.0) * gate * jax.nn.sigmoid(SWIGLU_ALPHA * gate)
        ys_ref[...] = _dot(act.astype(BF16), wd_s[...]) + bd_ref[...]

    @pl.when(i >= nv_ref[0])
    def _():
        ys_ref[...] = jnp.zeros_like(ys_ref)


def _experts(xs, tile_expert, n_valid, g_mlp, w_gate, b_gate, w_up, b_up, w_down, b_down):
    P, D = xs.shape
    tm = TM_EXPERT
    E, _, F = w_gate.shape
    w_spec = pl.BlockSpec((None, D, F), lambda i, te, nv: (te[i], 0, 0))
    wd_spec = pl.BlockSpec((None, F, D), lambda i, te, nv: (te[i], 0, 0))
    b_spec = pl.BlockSpec((None, 1, F), lambda i, te, nv: (te[i], 0, 0))
    bd_spec = pl.BlockSpec((None, 1, D), lambda i, te, nv: (te[i], 0, 0))
    x_spec = pl.BlockSpec((tm, D), lambda i, te, nv: (jnp.minimum(i, nv[0] - 1), 0))
    return pl.pallas_call(
        _experts_kernel,
        grid_spec=pltpu.PrefetchScalarGridSpec(
            num_scalar_prefetch=2,
            grid=(P // tm,),
            in_specs=[x_spec, pl.BlockSpec((1, D), lambda i, te, nv: (0, 0)),
                      w_spec, b_spec, w_spec, b_spec, wd_spec, bd_spec],
            out_specs=pl.BlockSpec((tm, D), lambda i, te, nv: (i, 0)),
            scratch_shapes=[pltpu.VMEM((D, F), BF16), pltpu.VMEM((D, F), BF16), pltpu.VMEM((F, D), BF16)]),
        out_shape=jax.ShapeDtypeStruct((P, D), F32),
        compiler_params=pltpu.CompilerParams(dimension_semantics=("arbitrary",), vmem_limit_bytes=VMEM_LIMIT),
        name="experts",
    )(tile_expert, n_valid, xs, g_mlp.reshape(1, D),
      w_gate, b_gate.reshape(E, 1, F), w_up, b_up.reshape(E, 1, F), w_down, b_down.reshape(E, 1, D))


def _combine_kernel(pos_ref, x_ref, wgt_ref, ys_ref, o_ref, buf_ref, sem):
    tm = x_ref.shape[0]

    def issue(j, carry):
        for u in range(ROWS_PER_ISSUE):
            r = j * ROWS_PER_ISSUE + u
            for k in range(TOP_K):
                _row_copy(ys_ref, pos_ref[r * TOP_K + k], buf_ref.at[k], r, sem).start(priority=k % 2)
        return carry

    lax.fori_loop(0, tm // ROWS_PER_ISSUE, issue, 0)
    for k in range(TOP_K):
        _rows_copy(ys_ref, buf_ref.at[k], tm, sem).wait()
    wgt = wgt_ref[...]
    acc = x_ref[...]
    for k in range(TOP_K):
        acc = acc + wgt[:, k:k + 1] * buf_ref[k]
    o_ref[...] = acc


def _combine(x1, wgt, pos_flat, ys):
    N, D = x1.shape
    tm = TM_COMBINE
    return pl.pallas_call(
        _combine_kernel,
        grid=(N // tm,),
        in_specs=[pl.BlockSpec((tm * TOP_K,), lambda i: (i,), memory_space=pltpu.SMEM),
                  pl.BlockSpec((tm, D), lambda i: (i, 0)),
                  pl.BlockSpec((tm, 128), lambda i: (i, 0)),
                  pl.BlockSpec(memory_space=pl.ANY)],
        out_specs=pl.BlockSpec((tm, D), lambda i: (i, 0)),
        out_shape=jax.ShapeDtypeStruct((N, D), F32),
        scratch_shapes=[pltpu.VMEM((TOP_K, tm, D), F32), pltpu.SemaphoreType.DMA],
        compiler_params=pltpu.CompilerParams(dimension_semantics=("arbitrary",), vmem_limit_bytes=VMEM_LIMIT),
        name="combine",
    )(pos_flat, x1, wgt, ys)


def _moe(x1, code, wgt, cnt, g_mlp, w_gate, b_gate, w_up, b_up, w_down, b_down):
    N, D = x1.shape
    tm = TM_EXPERT
    n_tiles = -(-(N * TOP_K + N_EXPERTS * (tm - 1)) // tm)
    counts = cnt[0, :N_EXPERTS]
    padded = ((counts + tm - 1) // tm) * tm
    ends = jnp.cumsum(padded)
    starts = ends - padded
    code = code[:, :TOP_K]
    pos = (starts[code // RANK_RADIX] + code % RANK_RADIX).reshape(-1).astype(I32)
    tile_start = jnp.arange(n_tiles, dtype=I32) * tm
    tile_expert = jnp.minimum(jnp.sum(ends[None, :] <= tile_start[:, None], axis=1), N_EXPERTS - 1).astype(I32)
    n_valid = (ends[-1] // tm).astype(I32).reshape(1)
    xs = _dispatch(x1, pos, (starts + counts).astype(I32), ends.astype(I32), n_tiles * tm)
    ys = _experts(xs, tile_expert, n_valid, g_mlp, w_gate, b_gate, w_up, b_up, w_down, b_down)
    return _combine(x1, wgt, pos, ys)


def _layer(x, g_attn, w_in, qn_nsa, kn_cmp, kn_sel, kn_win,
           pos_cmp_k, w_ck1, b_ck1, w_ck2, b_ck2, pos_cmp_v, w_cv1, b_cv1, w_cv2, b_cv2,
           b_nsa_gate, qn_fox, kn_fox, b_forget, w_up_nsa, w_up_fox, b_merge, w_out, g_mlp,
           w_router, b_router, w_e_gate, b_e_gate, w_e_up, b_e_up, w_e_down, b_e_down):
    B, T, D = x.shape
    qn, ks, vs, kw, vw, kc_raw, vc_raw, qf, kf, vf, gates = _in_proj(
        x, g_attn, w_in, qn_nsa, kn_sel, kn_win, b_nsa_gate, qn_fox, kn_fox, b_forget)
    kc, vc = _compress(kc_raw, vc_raw, pos_cmp_k, w_ck1, b_ck1, w_ck2, b_ck2,
                       pos_cmp_v, w_cv1, b_cv1, w_cv2, b_cv2, kn_cmp)
    o_nsa = _nsa(qn, ks, vs, kw, vw, kc, vc, gates)
    o_fox = _fox(qf, kf, vf)
    hw = NSA_HEADS * HEAD_PAD
    w_merge = w_in[:, w_in.shape[1] - 2 * D:]
    x1, code, wgt, cnt = _merge_out(x.reshape(B * T, D), g_attn, o_nsa.reshape(B * T, hw), o_fox.reshape(B * T, hw),
                                    w_merge, b_merge, w_up_nsa, w_up_fox, w_out, g_mlp, w_router, b_router)
    out = _moe(x1, code, wgt, cnt, g_mlp, w_e_gate, b_e_gate, w_e_up, b_e_up, w_e_down, b_e_down)
    return out.reshape(B, T, D)


def kernel(x, g_attn, w_in, qn_nsa, kn_cmp, kn_sel, kn_win, pos_cmp_k, w_ck1, b_ck1, w_ck2, b_ck2, pos_cmp_v, w_cv1, b_cv1, w_cv2, b_cv2, b_nsa_gate, qn_fox, kn_fox, b_forget, w_up_nsa, w_up_fox, b_merge, w_out, g_mlp, w_router, b_router, w_e_gate, b_e_gate, w_e_up, b_e_up, w_e_down, b_e_down):
    params = (g_attn, w_in, qn_nsa, kn_cmp, kn_sel, kn_win, pos_cmp_k, w_ck1, b_ck1, w_ck2, b_ck2,
              pos_cmp_v, w_cv1, b_cv1, w_cv2, b_cv2, b_nsa_gate, qn_fox, kn_fox, b_forget,
              w_up_nsa, w_up_fox, b_merge, w_out, g_mlp, w_router, b_router,
              w_e_gate, b_e_gate, w_e_up, b_e_up, w_e_down, b_e_down)
    for layer in range(g_attn.shape[0]):
        x = _layer(x, *[p[layer] for p in params])
    return x
```

```python
import functools

import jax
import jax.numpy as jnp
from jax import lax
from jax.experimental import pallas as pl
from jax.experimental.pallas import tpu as pltpu

F32 = jnp.float32
BF16 = jnp.bfloat16
I32 = jnp.int32

D_MODEL = 1024
HEAD_DIM = 64
HEAD_PAD = 128
NSA_HEADS = 8
NSA_GROUPS = 2
NSA_HPG = NSA_HEADS // NSA_GROUPS
FOX_HEADS = 8
CMP_BLOCK = 32
CMP_STRIDE = 16
CMP_HIDDEN = 256
N_CMP_PAD = 128
SEL_BLOCK = 64
SEL_TOPN = 16
N_SEL_BLOCKS = 32
WINDOW = 512
N_EXPERTS = 32
TOP_K = 4
SWIGLU_LIMIT = 7.0
SWIGLU_ALPHA = 1.702
RMS_EPS = 1e-6
NEG_BIG = -1e30
FORCED_SCORE = 1e9
QK_SCALE = HEAD_DIM ** -0.5

LANE_SEL = 64
LANE_POS = 96
LANE_FOX = 64

OFF_QN = 0
OFF_KS = OFF_QN + NSA_HEADS * HEAD_PAD
OFF_VS = OFF_KS + NSA_GROUPS * HEAD_PAD
OFF_KW = OFF_VS + NSA_GROUPS * HEAD_PAD
OFF_VW = OFF_KW + NSA_GROUPS * HEAD_PAD
OFF_KC = OFF_VW + NSA_GROUPS * HEAD_PAD
OFF_VC = OFF_KC + NSA_GROUPS * HEAD_DIM
OFF_QF = OFF_VC + NSA_GROUPS * HEAD_DIM
OFF_KF = OFF_QF + FOX_HEADS * HEAD_PAD
OFF_VF = OFF_KF + FOX_HEADS * HEAD_PAD
OFF_MISC = OFF_VF + FOX_HEADS * HEAD_PAD
W1_COLS = OFF_MISC + 128
MISC_GATES = 3 * NSA_HEADS

TM_PROJ = 256
TQ_NSA = 128
TQ_SELECT = 512
TILES_PER_REGION = 4
SELECTS_PER_REGION = 2
CK_ATTN = 256
TM_OUT = 512
TM_EXPERT = 512
TM_DISPATCH = 1024
TM_COMBINE = 512
ROWS_PER_ISSUE = 8
RANK_RADIX = 1 << 16
VMEM_LIMIT = 56 * 1024 * 1024


def _dot(a, b):
    return jnp.dot(a, b, preferred_element_type=F32)


def _dot_nt(a, b):
    return lax.dot_general(a, b, (((1,), (1,)), ((), ())), preferred_element_type=F32)


def _split3(v):
    hi = v.astype(BF16)
    r1 = v - hi.astype(F32)
    mid = r1.astype(BF16)
    lo = (r1 - mid.astype(F32)).astype(BF16)
    return hi, mid, lo


def _dot_exact_rhs(a_f32, b_bf16):
    hi, mid, lo = _split3(a_f32)
    return _dot(hi, b_bf16) + _dot(mid, b_bf16) + _dot(lo, b_bf16)


def _dot_exact_lhs(a_bf16, b_f32):
    hi, mid, lo = _split3(b_f32)
    return _dot(a_bf16, hi) + _dot(a_bf16, mid) + _dot(a_bf16, lo)


def _rms(x):
    return x * lax.rsqrt(jnp.mean(x * x, axis=-1, keepdims=True) + RMS_EPS)


def _head_rms(a, gain):
    ms = jnp.sum(a * a, axis=-1, keepdims=True) * (1.0 / HEAD_DIM)
    return a * lax.rsqrt(ms + RMS_EPS) * gain


def _pos_feat_k(pos, lane):
    hi = (256 * (pos >> 8)).astype(F32)
    lo = (pos & 255).astype(F32)
    return jnp.where(lane < LANE_POS + 2, 1.0, jnp.where(lane == LANE_POS + 2, hi, lo)) * (
        (lane >= LANE_POS) & (lane < LANE_POS + 4)).astype(F32)


def _in_proj_kernel(x_ref, g_ref, w_ref, gains_ref, bmisc_ref, pq_ref, pk_ref,
                    qn_ref, ks_ref, vs_ref, kw_ref, vw_ref, kc_ref, vc_ref,
                    qf_ref, kf_ref, vf_ref, gate_ref, carry_ref):
    tm = x_ref.shape[0]
    i = pl.program_id(1)
    h = (_rms(x_ref[...]) * g_ref[...]).astype(BF16)
    pos = i * tm + lax.broadcasted_iota(I32, (tm, 1), 0)
    lane = lax.broadcasted_iota(I32, (1, HEAD_PAD), 1)
    pos_hi = (256 * (pos >> 8)).astype(F32)
    pos_lo = (pos & 255).astype(F32)
    kfeat = _pos_feat_k(pos, lane)
    blk_onehot = ((lane - LANE_SEL) == (pos >> 6)).astype(F32)

    def seg(off, width):
        return _dot(h, w_ref[:, off:off + width])

    a = seg(OFF_QN, NSA_HEADS * HEAD_PAD)
    gq = gains_ref[0:1, :] * QK_SCALE
    for hd in range(NSA_HEADS):
        slope = 2.0 ** (-(hd + 1))
        qfeat = jnp.where(lane == LANE_POS, -slope * pos_hi,
                          jnp.where(lane == LANE_POS + 1, -slope * pos_lo,
                                    jnp.where((lane == LANE_POS + 2) | (lane == LANE_POS + 3), slope, 0.0)))
        sl = slice(hd * HEAD_PAD, (hd + 1) * HEAD_PAD)
        qn_ref[:, sl] = (_head_rms(a[:, sl], gq) + qfeat).astype(BF16)

    a = seg(OFF_KS, NSA_GROUPS * HEAD_PAD)
    for g in range(NSA_GROUPS):
        sl = slice(g * HEAD_PAD, (g + 1) * HEAD_PAD)
        ks_ref[:, sl] = (_head_rms(a[:, sl], gains_ref[1:2, :]) + kfeat + blk_onehot).astype(BF16)
    vs_ref[...] = seg(OFF_VS, NSA_GROUPS * HEAD_PAD).astype(BF16)
    a = seg(OFF_KW, NSA_GROUPS * HEAD_PAD)
    for g in range(NSA_GROUPS):
        sl = slice(g * HEAD_PAD, (g + 1) * HEAD_PAD)
        kw_ref[:, sl] = (_head_rms(a[:, sl], gains_ref[2:3, :]) + kfeat).astype(BF16)
    vw_ref[...] = seg(OFF_VW, NSA_GROUPS * HEAD_PAD).astype(BF16)
    kc_ref[...] = seg(OFF_KC, NSA_GROUPS * HEAD_DIM).astype(BF16)
    vc_ref[...] = seg(OFF_VC, NSA_GROUPS * HEAD_DIM).astype(BF16)

    misc = seg(OFF_MISC, 128) + bmisc_ref[...]
    is_gate = lane < MISC_GATES
    gate_ref[...] = jnp.where(is_gate, jax.nn.sigmoid(misc), 0.0)
    is_f = (lane >= MISC_GATES) & (lane < MISC_GATES + 3 * FOX_HEADS)
    logf = jnp.where(is_f, jax.nn.log_sigmoid(misc), 0.0)

    @pl.when(i == 0)
    def _():
        carry_ref[...] = jnp.zeros_like(carry_ref)

    r = lax.broadcasted_iota(I32, (tm, tm), 0)
    c = lax.broadcasted_iota(I32, (tm, tm), 1)
    tri = (r >= c).astype(BF16)
    csum = _dot_exact_lhs(tri, logf) + carry_ref[...]
    carry_ref[...] = csum[tm - 1:tm, :]
    c_hi, c_mid, c_lo = _split3(csum)
    c3 = jnp.where(lane < MISC_GATES + FOX_HEADS, c_hi,
                   jnp.where(lane < MISC_GATES + 2 * FOX_HEADS, c_mid, c_lo))
    c3 = jnp.where(is_f, c3, jnp.zeros_like(c3))
    ones_lo = ((lane >= LANE_FOX) & (lane < LANE_FOX + 3)).astype(F32)
    ones_hi = ((lane >= LANE_FOX + 3) & (lane < LANE_FOX + 6)).astype(F32)

    a = seg(OFF_QF, FOX_HEADS * HEAD_PAD)
    fq = _dot(c3, pq_ref[...])
    gq = gains_ref[3:4, :] * QK_SCALE
    for hd in range(FOX_HEADS):
        sl = slice(hd * HEAD_PAD, (hd + 1) * HEAD_PAD)
        qf_ref[:, sl] = (_head_rms(a[:, sl], gq) + fq[:, sl] + ones_hi).astype(BF16)
    a = seg(OFF_KF, FOX_HEADS * HEAD_PAD)
    fk = _dot(c3, pk_ref[...])
    for hd in range(FOX_HEADS):
        sl = slice(hd * HEAD_PAD, (hd + 1) * HEAD_PAD)
        kf_ref[:, sl] = (_head_rms(a[:, sl], gains_ref[4:5, :]) + fk[:, sl] + ones_lo).astype(BF16)
    vf_ref[...] = seg(OFF_VF, FOX_HEADS * HEAD_PAD).astype(BF16)


def _pad_heads(w, n_heads):
    lead = w.shape[:-1]
    w = w.reshape(lead + (n_heads, HEAD_DIM))
    w = jnp.pad(w, [(0, 0)] * len(lead) + [(0, 0), (0, HEAD_PAD - HEAD_DIM)])
    return w.reshape(lead + (n_heads * HEAD_PAD,))


def _pad_gain(g):
    return jnp.pad(g.astype(F32), (0, HEAD_PAD - HEAD_DIM))


def _fox_placement():
    pq = jnp.zeros((128, FOX_HEADS * HEAD_PAD), F32)
    pk = jnp.zeros((128, FOX_HEADS * HEAD_PAD), F32)
    for piece in range(3):
        for hd in range(FOX_HEADS):
            src = MISC_GATES + piece * FOX_HEADS + hd
            pq = pq.at[src, hd * HEAD_PAD + LANE_FOX + piece].set(1.0)
            pk = pk.at[src, hd * HEAD_PAD + LANE_FOX + 3 + piece].set(-1.0)
    return pq.astype(BF16), pk.astype(BF16)


def _in_proj(x, g_attn, w_in, qn_nsa, kn_sel, kn_win, b_nsa_gate, qn_fox, kn_fox, b_forget):
    B, T, D = x.shape
    s1 = NSA_HEADS * HEAD_DIM
    kv_w = NSA_GROUPS * HEAD_DIM
    s2 = s1 + 6 * kv_w
    s3 = s2 + 3 * NSA_HEADS
    fw = FOX_HEADS * HEAD_DIM
    s4 = s3 + 3 * fw
    s5 = s4 + FOX_HEADS
    kv = [w_in[:, s1 + j * kv_w: s1 + (j + 1) * kv_w] for j in range(6)]
    f_cols = w_in[:, s4:s5]
    misc = jnp.concatenate([w_in[:, s2:s3], f_cols, f_cols, f_cols,
                            jnp.zeros((D, 128 - MISC_GATES - 3 * FOX_HEADS), F32)], axis=1)
    w1 = jnp.concatenate([
        _pad_heads(w_in[:, :s1], NSA_HEADS),
        _pad_heads(kv[2], NSA_GROUPS), _pad_heads(kv[3], NSA_GROUPS),
        _pad_heads(kv[4], NSA_GROUPS), _pad_heads(kv[5], NSA_GROUPS),
        kv[0], kv[1],
        _pad_heads(w_in[:, s3:s3 + fw], FOX_HEADS),
        _pad_heads(w_in[:, s3 + fw:s3 + 2 * fw], FOX_HEADS),
        _pad_heads(w_in[:, s3 + 2 * fw:s4], FOX_HEADS),
        misc], axis=1).astype(BF16)
    assert w1.shape[1] == W1_COLS
    gains = jnp.stack([_pad_gain(qn_nsa), _pad_gain(kn_sel), _pad_gain(kn_win),
                       _pad_gain(qn_fox), _pad_gain(kn_fox),
                       jnp.zeros(HEAD_PAD, F32), jnp.zeros(HEAD_PAD, F32), jnp.zeros(HEAD_PAD, F32)])
    bmisc = jnp.concatenate([b_nsa_gate, b_forget, b_forget, b_forget,
                             jnp.zeros(128 - MISC_GATES - 3 * FOX_HEADS, F32)]).reshape(1, 128)
    pq, pk = _fox_placement()
    tm = TM_PROJ

    def tok(width):
        return pl.BlockSpec((None, tm, width), lambda b, i: (b, i, 0))

    def full(shape):
        return pl.BlockSpec(shape, lambda b, i: (0,) * len(shape))

    widths = [NSA_HEADS * HEAD_PAD] + [NSA_GROUPS * HEAD_PAD] * 4 + [NSA_GROUPS * HEAD_DIM] * 2 + \
             [FOX_HEADS * HEAD_PAD] * 3
    out_shape = [jax.ShapeDtypeStruct((B, T, w), BF16) for w in widths] + \
                [jax.ShapeDtypeStruct((B, T, 128), F32)]
    return pl.pallas_call(
        _in_proj_kernel,
        grid=(B, T // tm),
        in_specs=[tok(D), full((1, D)), full((D, W1_COLS)), full((8, HEAD_PAD)), full((1, 128)),
                  full((128, FOX_HEADS * HEAD_PAD)), full((128, FOX_HEADS * HEAD_PAD))],
        out_specs=[tok(w) for w in widths] + [tok(128)],
        out_shape=out_shape,
        scratch_shapes=[pltpu.VMEM((1, 128), F32)],
        compiler_params=pltpu.CompilerParams(
            dimension_semantics=("arbitrary", "arbitrary"), vmem_limit_bytes=VMEM_LIMIT),
        name="in_proj",
    )(x, g_attn.reshape(1, D), w1, gains, bmisc, pq, pk)


def _gelu_tanh(x):
    return 0.5 * x * (1.0 + jnp.tanh(0.7978845608028654 * (x + 0.044715 * (x * x * x))))


def _compress_kernel(ak_ref, av_ref, posk_ref, posv_ref, wk1_ref, bk1_ref, wk2_ref, bk2_ref,
                     wv1_ref, bv1_ref, wv2_ref, bv2_ref, gain_ref, kc_ref, vc_ref):
    row = lax.broadcasted_iota(I32, (N_CMP_PAD, 1), 0)
    lane = lax.broadcasted_iota(I32, (1, HEAD_PAD), 1)
    kfeat = _pos_feat_k(CMP_STRIDE * row + CMP_BLOCK - 1, lane)

    def mlp(a_ref, pos_ref, w1_ref, b1_ref, w2_ref, b2_ref):
        a = a_ref[...].astype(F32)
        lo = (a + pos_ref[0:1, :]).astype(BF16)
        hi = (a + pos_ref[1:2, :]).astype(BF16)
        outs = []
        for g in range(NSA_GROUPS):
            hid = _dot(lo, w1_ref[0, g]) + pltpu.roll(_dot(hi, w1_ref[1, g]), N_CMP_PAD - 1, 0) + b1_ref[...]
            outs.append(_dot(_gelu_tanh(hid).astype(BF16), w2_ref[...]) + b2_ref[...])
        return outs

    ks = mlp(ak_ref, posk_ref, wk1_ref, bk1_ref, wk2_ref, bk2_ref)
    vs = mlp(av_ref, posv_ref, wv1_ref, bv1_ref, wv2_ref, bv2_ref)
    for g in range(NSA_GROUPS):
        kc_ref[g] = (_head_rms(ks[g], gain_ref[...]) + kfeat).astype(BF16)
        vc_ref[g] = vs[g].astype(BF16)


def _compress(kc_raw, vc_raw, pos_k, w_ck1, b_ck1, w_ck2, b_ck2, pos_v, w_cv1, b_cv1, w_cv2, b_cv2, kn_cmp):
    B, T, _ = kc_raw.shape
    n_chunks = T // CMP_STRIDE
    assert n_chunks == N_CMP_PAD
    chunk_w = CMP_STRIDE * NSA_GROUPS * HEAD_DIM

    def chunks(a):
        return a.reshape(B, n_chunks, chunk_w)

    def w1_rows(w):
        w = w.reshape(2, CMP_STRIDE, 1, HEAD_DIM, CMP_HIDDEN)
        per_group = [jnp.concatenate([w if gg == g else jnp.zeros_like(w) for gg in range(NSA_GROUPS)], axis=2)
                     for g in range(NSA_GROUPS)]
        return jnp.stack(per_group, axis=1).reshape(2, NSA_GROUPS, chunk_w, CMP_HIDDEN).astype(BF16)

    def pos_rows(p):
        p = jnp.broadcast_to(p.reshape(2, CMP_STRIDE, 1, HEAD_DIM), (2, CMP_STRIDE, NSA_GROUPS, HEAD_DIM))
        return p.reshape(2, chunk_w)

    def w2pad(w, b):
        return (jnp.pad(w, ((0, 0), (0, HEAD_PAD - HEAD_DIM))).astype(BF16),
                jnp.pad(b, (0, HEAD_PAD - HEAD_DIM)).reshape(1, HEAD_PAD))

    wk2, bk2 = w2pad(w_ck2, b_ck2)
    wv2, bv2 = w2pad(w_cv2, b_cv2)
    a_spec = pl.BlockSpec((None, n_chunks, chunk_w), lambda b: (b, 0, 0))
    o_spec = pl.BlockSpec((None, NSA_GROUPS, N_CMP_PAD, HEAD_PAD), lambda b: (b, 0, 0, 0))

    def full(shape):
        return pl.BlockSpec(shape, lambda b: (0,) * len(shape))

    w1_shape = (2, NSA_GROUPS, chunk_w, CMP_HIDDEN)
    return pl.pallas_call(
        _compress_kernel,
        grid=(B,),
        in_specs=[a_spec, a_spec, full((2, chunk_w)), full((2, chunk_w)),
                  full(w1_shape), full((1, CMP_HIDDEN)), full((CMP_HIDDEN, HEAD_PAD)), full((1, HEAD_PAD)),
                  full(w1_shape), full((1, CMP_HIDDEN)), full((CMP_HIDDEN, HEAD_PAD)), full((1, HEAD_PAD)),
                  full((1, HEAD_PAD))],
        out_specs=[o_spec, o_spec],
        out_shape=[jax.ShapeDtypeStruct((B, NSA_GROUPS, N_CMP_PAD, HEAD_PAD), BF16)] * 2,
        compiler_params=pltpu.CompilerParams(dimension_semantics=("arbitrary",), vmem_limit_bytes=VMEM_LIMIT),
        name="compress",
    )(chunks(kc_raw), chunks(vc_raw), pos_rows(pos_k), pos_rows(pos_v),
      w1_rows(w_ck1), b_ck1.reshape(1, -1), wk2, bk2,
      w1_rows(w_cv1), b_cv1.reshape(1, -1), wv2, bv2, _pad_gain(kn_cmp).reshape(1, HEAD_PAD))


def _lane_blocks(a):
    return [a[:, i * 128:(i + 1) * 128] for i in range(a.shape[1] // 128)]


def _fold_blocks(op, acc, a):
    for blk in _lane_blocks(a):
        acc = blk if acc is None else op(acc, blk)
    return acc


def _nsa_kernel(q_ref, ks_ref, vs_ref, kw_ref, vw_ref, kc_ref, vct_ref, gate_ref, ovl_ref, o_ref,
                sel_ref, oc_ref, s_ref):
    seq = q_ref.shape[0]
    tq = TQ_NSA
    ta = TQ_SELECT
    ck = CK_ATTN
    span = WINDOW + tq
    g = pl.program_id(1)
    col_l = lax.broadcasted_iota(I32, (1, 128), 1)
    c_end_sub = CMP_STRIDE * lax.broadcasted_iota(I32, (N_CMP_PAD, 1), 0) + (CMP_BLOCK - 1)
    blk_id = lax.broadcasted_iota(I32, (N_SEL_BLOCKS, 1), 0)
    sub_id = lax.broadcasted_iota(I32, (8, 1), 0)

    def gate_col(gates, branch, hh):
        lo = branch * NSA_HEADS + hh
        return jnp.where(g == 0, gates[:, lo:lo + 1], gates[:, lo + NSA_HPG:lo + NSA_HPG + 1])

    def select(a):
        a0 = a * ta
        t_lane = a0 + lax.broadcasted_iota(I32, (1, ta), 1)
        gates = gate_ref[a0:a0 + ta, :]

        mask_t = t_lane >= c_end_sub
        p_sum = None
        for hh in range(NSA_HPG):
            q_h = q_ref[a0:a0 + ta, hh * HEAD_PAD:(hh + 1) * HEAD_PAD]
            s_t = jnp.where(mask_t, _dot_nt(kc_ref[...], q_h), NEG_BIG)
            e_t = jnp.where(mask_t, jnp.exp(s_t - jnp.max(s_t, axis=0, keepdims=True)), 0.0)
            l_t = jnp.sum(e_t, axis=0, keepdims=True)
            p_t = e_t / jnp.where(l_t > 0.0, l_t, 1.0)
            o_t = _dot(vct_ref[...], p_t.astype(BF16))
            g_c = gate_col(gates, 0, hh)
            for blk in range(ta // 128):
                bs = slice(blk * 128, (blk + 1) * 128)
                oc_ref[a0 + blk * 128:a0 + (blk + 1) * 128, hh * HEAD_PAD:(hh + 1) * HEAD_PAD] = g_c[bs] * o_t[:, bs].T
            p_sum = p_t if p_sum is None else p_sum + p_t
        imp = _dot_exact_lhs(ovl_ref[...], p_sum)
        tb = t_lane >> 6
        valid = blk_id <= tb
        forced = (blk_id == 0) | (blk_id == tb) | (blk_id == tb - 1)
        score = jnp.where(forced & valid, FORCED_SCORE, imp)
        score = jnp.where(valid, score, -1.0)
        groups = [score[8 * b:8 * b + 8, :] for b in range(N_SEL_BLOCKS // 8)]
        n_above = [jnp.zeros((8, ta), F32) for _ in groups]
        for j in range(N_SEL_BLOCKS):
            cj = score[j:j + 1, :]
            for b, grp in enumerate(groups):
                if 8 * b > j:
                    n_above[b] = n_above[b] + jnp.where(cj >= grp, 1.0, 0.0)
                elif 8 * b + 7 < j:
                    n_above[b] = n_above[b] + jnp.where(cj > grp, 1.0, 0.0)
                else:
                    tie = jnp.where(sub_id + 8 * b > j, 1.0, 0.0)
                    n_above[b] = n_above[b] + jnp.where(cj > grp, 1.0, 0.0) + jnp.where(cj == grp, tie, 0.0)
        dropped = jnp.where(jnp.concatenate(n_above, axis=0) >= float(SEL_TOPN), NEG_BIG, 0.0)
        drop_t = jnp.concatenate([jnp.zeros((LANE_SEL, ta), F32), dropped,
                                  jnp.zeros((HEAD_PAD - LANE_SEL - N_SEL_BLOCKS, ta), F32)], axis=0)
        for blk in range(ta // 128):
            sel_ref[a0 + blk * 128:a0 + (blk + 1) * 128, :] = drop_t[:, blk * 128:(blk + 1) * 128].T.astype(BF16)

    for first in range(0, seq // ta, SELECTS_PER_REGION):
        @pl.when(g >= -first)
        def _(first=first):
            for a in range(first, first + SELECTS_PER_REGION):
                select(a)

    rows = NSA_HPG * tq
    row_l = lax.broadcasted_iota(I32, (rows, 1), 0) & (tq - 1)

    def attend(i):
        t0 = i * tq
        t_row = t0 + row_l
        q4 = jnp.concatenate([q_ref[t0:t0 + tq, hh * HEAD_PAD:(hh + 1) * HEAD_PAD] for hh in range(NSA_HPG)], axis=0)
        q4s = q4 + jnp.concatenate([sel_ref[t0:t0 + tq, :]] * NSA_HPG, axis=0)

        n_chunks = (t0 + tq - 1) // ck + 1
        slot = i % 2
        mx = None
        for c in range(n_chunks):
            s = _dot_nt(q4s, ks_ref[c * ck:(c + 1) * ck, :])
            if (c + 1) * ck > t0:
                s = jnp.where(t_row >= c * ck + lax.broadcasted_iota(I32, (1, ck), 1), s, NEG_BIG)
            s_ref[slot, :, c * ck:(c + 1) * ck] = s
            mx = _fold_blocks(jnp.maximum, mx, s)
        m = jnp.max(mx, axis=-1, keepdims=True)
        ls = None
        acc = None
        for c in range(n_chunks):
            p = jnp.exp(s_ref[slot, :, c * ck:(c + 1) * ck] - m)
            ls = _fold_blocks(jnp.add, ls, p)
            pv = _dot(p.astype(BF16), vs_ref[c * ck:(c + 1) * ck, :])
            acc = pv if acc is None else acc + pv
        o_s = acc / jnp.sum(ls, axis=-1, keepdims=True)

        ws = max(t0 - WINDOW, 0)
        blocks = _lane_blocks(_dot_nt(q4, kw_ref[ws:ws + span, :]))
        if t0 < WINDOW:
            blocks = [jnp.where(t_row >= ws + jb * 128 + col_l, blk, NEG_BIG) if ws + (jb + 1) * 128 > t0 else blk
                      for jb, blk in enumerate(blocks)]
        else:
            blocks[0] = jnp.where(col_l > row_l, blocks[0], NEG_BIG)
            blocks[-1] = jnp.where(col_l <= row_l, blocks[-1], NEG_BIG)
        s = jnp.concatenate(blocks, axis=1)
        m = jnp.max(_fold_blocks(jnp.maximum, None, s), axis=-1, keepdims=True)
        p = jnp.exp(s - m)
        l = jnp.sum(_fold_blocks(jnp.add, None, p), axis=-1, keepdims=True)
        o_w = _dot(p.astype(BF16), vw_ref[ws:ws + span, :]) / l

        gates = gate_ref[t0:t0 + tq, :]
        outs = []
        for hh in range(NSA_HPG):
            rs = slice(hh * tq, (hh + 1) * tq)
            o = oc_ref[t0:t0 + tq, hh * HEAD_PAD:(hh + 1) * HEAD_PAD]
            outs.append((o + gate_col(gates, 1, hh) * o_s[rs] + gate_col(gates, 2, hh) * o_w[rs]).astype(BF16))
        o_ref[t0:t0 + tq, :] = jnp.concatenate(outs, axis=1)

    for first in range(0, seq // tq, TILES_PER_REGION):
        @pl.when(g > -1 - first)
        def _(first=first):
            for i in range(first, first + TILES_PER_REGION):
                attend(i)


def _overlap_matrix():
    c = jnp.arange(N_CMP_PAD)[None, :]
    j = jnp.arange(N_SEL_BLOCKS)[:, None]
    lo = (CMP_STRIDE * c) // SEL_BLOCK
    hi = (CMP_STRIDE * c + CMP_BLOCK - 1) // SEL_BLOCK
    return ((j == lo) | (j == hi)).astype(BF16)


def _nsa(qn, ks, vs, kw, vw, kc, vc, gates):
    B, T, _ = qn.shape
    gw = NSA_HPG * HEAD_PAD
    q_spec = pl.BlockSpec((None, T, gw), lambda b, g: (b, 0, g))
    kv_spec = pl.BlockSpec((None, T, HEAD_PAD), lambda b, g: (b, 0, g))
    c_spec = pl.BlockSpec((None, None, N_CMP_PAD, HEAD_PAD), lambda b, g: (b, g, 0, 0))
    return pl.pallas_call(
        _nsa_kernel,
        grid=(B, NSA_GROUPS),
        in_specs=[q_spec, kv_spec, kv_spec, kv_spec, kv_spec, c_spec, c_spec,
                  pl.BlockSpec((None, T, 128), lambda b, g: (b, 0, 0)),
                  pl.BlockSpec((N_SEL_BLOCKS, N_CMP_PAD), lambda b, g: (0, 0))],
        out_specs=q_spec,
        out_shape=jax.ShapeDtypeStruct((B, T, NSA_HEADS * HEAD_PAD), BF16),
        scratch_shapes=[pltpu.VMEM((T, HEAD_PAD), BF16), pltpu.VMEM((T, gw), F32),
                        pltpu.VMEM((2, NSA_HPG * TQ_NSA, T), F32)],
        compiler_params=pltpu.CompilerParams(
            dimension_semantics=("arbitrary", "arbitrary"), vmem_limit_bytes=VMEM_LIMIT),
        name="nsa",
    )(qn, ks, vs, kw, vw, kc, vc.swapaxes(2, 3), gates, _overlap_matrix())


def _fox_kernel(q_ref, k_ref, v_ref, o_ref):
    seq = q_ref.shape[0]
    ck = CK_ATTN
    row = lax.broadcasted_iota(I32, (ck, 1), 0)
    col = lax.broadcasted_iota(I32, (1, ck), 1)
    for qi in range(seq // ck):
        q = q_ref[qi * ck:(qi + 1) * ck, :]
        s_chunks = []
        mx = None
        for c in range(qi + 1):
            s = _dot_nt(q, k_ref[c * ck:(c + 1) * ck, :])
            if c == qi:
                s = jnp.where(row >= col, s, NEG_BIG)
            s_chunks.append(s)
            mx = _fold_blocks(jnp.maximum, mx, s)
        m = jnp.max(mx, axis=-1, keepdims=True)
        ls = None
        acc = None
        for c in range(qi + 1):
            p = jnp.exp(s_chunks[c] - m)
            ls = _fold_blocks(jnp.add, ls, p)
            pv = _dot(p.astype(BF16), v_ref[c * ck:(c + 1) * ck, :])
            acc = pv if acc is None else acc + pv
        o_ref[qi * ck:(qi + 1) * ck, :] = (acc / jnp.sum(ls, axis=-1, keepdims=True)).astype(BF16)


def _fox(qf, kf, vf):
    B, T, _ = qf.shape
    spec = pl.BlockSpec((None, T, HEAD_PAD), lambda b, h: (b, 0, h))
    return pl.pallas_call(
        _fox_kernel,
        grid=(B, FOX_HEADS),
        in_specs=[spec, spec, spec],
        out_specs=spec,
        out_shape=jax.ShapeDtypeStruct((B, T, FOX_HEADS * HEAD_PAD), BF16),
        compiler_params=pltpu.CompilerParams(
            dimension_semantics=("arbitrary", "arbitrary"), vmem_limit_bytes=VMEM_LIMIT),
        name="fox",
    )(qf, kf, vf)


def _merge_out_kernel(x_ref, g_ref, on_ref, of_ref, wm_ref, bm_ref, wun_ref, wuf_ref, wo_ref,
                      gm_ref, wrh_ref, wrm_ref, br_ref, o_ref, code_ref, wgt_ref, cnt_ref, carry_ref):
    x = x_ref[...]
    h = (_rms(x) * g_ref[...]).astype(BF16)
    merge = jax.nn.sigmoid(_dot(h, wm_ref[...]) + bm_ref[...])
    y = merge[:, :D_MODEL] * _dot(on_ref[...], wun_ref[...]) + merge[:, D_MODEL:] * _dot(of_ref[...], wuf_ref[...])
    x1 = x + _dot(y.astype(BF16), wo_ref[...])
    o_ref[...] = x1
    _route_tile(x1, gm_ref, wrh_ref, wrm_ref, br_ref, code_ref, wgt_ref, cnt_ref, carry_ref)


def _merge_out(x2, g_attn, o_nsa, o_fox, w_merge, b_merge, w_up_nsa, w_up_fox, w_out, g_mlp, w_router, b_router):
    N, D = x2.shape
    tm = TM_OUT
    assert N <= RANK_RADIX
    wr = jnp.pad(w_router, ((0, 0), (0, 128 - N_EXPERTS)))
    wr_hi = wr.astype(BF16)
    wr_mid = (wr - wr_hi.astype(F32)).astype(BF16)
    br = jnp.pad(b_router, (0, 128 - N_EXPERTS)).reshape(1, 128)

    def pad_rows(w, n_heads):
        return _pad_heads(w.T, n_heads).T.astype(BF16)

    def tok(width):
        return pl.BlockSpec((tm, width), lambda i: (i, 0))

    def full(shape):
        return pl.BlockSpec(shape, lambda i: (0,) * len(shape))

    hw = NSA_HEADS * HEAD_PAD
    return pl.pallas_call(
        _merge_out_kernel,
        grid=(N // tm,),
        in_specs=[tok(D), full((1, D)), tok(hw), tok(hw), full((D, 2 * D)), full((1, 2 * D)),
                  full((hw, D)), full((hw, D)), full((D, D)),
                  full((1, D)), full((D, 128)), full((D, 128)), full((1, 128))],
        out_specs=[tok(D), tok(128), tok(128), full((1, 128))],
        out_shape=[jax.ShapeDtypeStruct((N, D), F32), jax.ShapeDtypeStruct((N, 128), I32),
                   jax.ShapeDtypeStruct((N, 128), F32), jax.ShapeDtypeStruct((1, 128), I32)],
        scratch_shapes=[pltpu.VMEM((1, 128), F32)],
        compiler_params=pltpu.CompilerParams(dimension_semantics=("arbitrary",), vmem_limit_bytes=VMEM_LIMIT),
        name="merge_out",
    )(x2, g_attn.reshape(1, D), o_nsa, o_fox, w_merge.astype(BF16), b_merge.reshape(1, 2 * D),
      pad_rows(w_up_nsa, NSA_HEADS), pad_rows(w_up_fox, FOX_HEADS), w_out.astype(BF16),
      g_mlp.reshape(1, D), wr_hi, wr_mid, br)


def _route_tile(x, g_ref, wh_ref, wm_ref, br_ref, code_ref, wgt_ref, cnt_ref, carry_ref):
    tm = x.shape[0]

    @pl.when(pl.program_id(0) == 0)
    def _():
        carry_ref[...] = jnp.zeros_like(carry_ref)

    h = _rms(x) * g_ref[...]
    h_hi = h.astype(BF16)
    h_mid = (h - h_hi.astype(F32)).astype(BF16)
    logits = _dot(h_hi, wh_ref[...]) + _dot(h_hi, wm_ref[...]) + _dot(h_mid, wh_ref[...]) + br_ref[...]
    lane = lax.broadcasted_iota(I32, (1, 128), 1)
    lane_f = lane.astype(F32)
    work = jnp.where(lane < N_EXPERTS, logits, -jnp.inf)
    vals, hots = [], []
    idx_out = jnp.zeros((tm, 128), F32)
    for k in range(TOP_K):
        m = jnp.max(work, axis=-1, keepdims=True)
        idx = jnp.min(jnp.where(work == m, lane_f, 128.0), axis=-1, keepdims=True)
        hot = lane_f == idx
        work = jnp.where(hot, -jnp.inf, work)
        vals.append(m)
        hots.append(hot)
        idx_out = idx_out + jnp.where(lane == k, idx, 0.0)
    e = [jnp.exp(v - vals[0]) for v in vals]
    denom = e[0] + e[1] + e[2] + e[3]
    wgt = jnp.zeros((tm, 128), F32)
    for k in range(TOP_K):
        wgt = wgt + jnp.where(lane == k, e[k] / denom, 0.0)

    multi = (hots[0] | hots[1] | hots[2] | hots[3]).astype(F32)
    r = lax.broadcasted_iota(I32, (tm, tm), 0)
    c = lax.broadcasted_iota(I32, (tm, tm), 1)
    before = _dot((r > c).astype(BF16), multi.astype(BF16)) + carry_ref[...]
    rank = jnp.zeros((tm, 128), F32)
    for k in range(TOP_K):
        rk = jnp.sum(jnp.where(hots[k], before, 0.0), axis=-1, keepdims=True)
        rank = rank + jnp.where(lane == k, rk, 0.0)
    total = carry_ref[...] + jnp.sum(multi, axis=0, keepdims=True)
    carry_ref[...] = total
    code_ref[...] = idx_out.astype(I32) * RANK_RADIX + rank.astype(I32)
    wgt_ref[...] = wgt
    cnt_ref[...] = total.astype(I32)


def _row_copy(src_ref, src_row, dst_ref, dst_row, sem):
    return pltpu.make_async_copy(src_ref.at[pl.ds(src_row, 1), :], dst_ref.at[pl.ds(dst_row, 1), :], sem)


def _rows_copy(src_ref, dst_ref, n_rows, sem):
    return pltpu.make_async_copy(src_ref.at[pl.ds(0, n_rows), :], dst_ref.at[pl.ds(0, n_rows), :], sem)


def _dispatch_kernel(pad_lo_ref, pad_hi_ref, pos_ref, x_ref, xs_ref, zero_ref, sem, zsem):
    tm = x_ref.shape[0]

    def issue(j, carry):
        for u in range(ROWS_PER_ISSUE):
            r = j * ROWS_PER_ISSUE + u
            for k in range(TOP_K):
                _row_copy(x_ref, r, xs_ref, pos_ref[r * TOP_K + k], sem).start(priority=k % 2)
        return carry

    lax.fori_loop(0, tm // ROWS_PER_ISSUE, issue, 0)

    @pl.when(pl.program_id(0) == pl.num_programs(0) - 1)
    def _():
        zero_ref[...] = jnp.zeros_like(zero_ref)
        for e in range(N_EXPERTS):
            def pad(p, carry):
                _row_copy(zero_ref, 0, xs_ref, p, zsem).start()
                return carry
            lax.fori_loop(pad_lo_ref[e], pad_hi_ref[e], pad, 0)
        for e in range(N_EXPERTS):
            def pad_wait(p, carry):
                _row_copy(zero_ref, 0, xs_ref, 0, zsem).wait()
                return carry
            lax.fori_loop(pad_lo_ref[e], pad_hi_ref[e], pad_wait, 0)

    for k in range(TOP_K):
        _rows_copy(x_ref, xs_ref, tm, sem).wait()


def _dispatch(x1, pos_flat, pad_lo, pad_hi, n_rows):
    N, D = x1.shape
    tm = TM_DISPATCH
    return pl.pallas_call(
        _dispatch_kernel,
        grid_spec=pltpu.PrefetchScalarGridSpec(
            num_scalar_prefetch=2,
            grid=(N // tm,),
            in_specs=[pl.BlockSpec((tm * TOP_K,), lambda i, lo, hi: (i,), memory_space=pltpu.SMEM),
                      pl.BlockSpec((tm, D), lambda i, lo, hi: (i, 0))],
            out_specs=pl.BlockSpec(memory_space=pl.ANY),
            scratch_shapes=[pltpu.VMEM((8, D), F32), pltpu.SemaphoreType.DMA, pltpu.SemaphoreType.DMA]),
        out_shape=jax.ShapeDtypeStruct((n_rows, D), F32),
        compiler_params=pltpu.CompilerParams(dimension_semantics=("arbitrary",), vmem_limit_bytes=VMEM_LIMIT),
        name="dispatch",
    )(pad_lo, pad_hi, pos_flat, x1)


def _experts_kernel(te_ref, nv_ref, xs_ref, g_ref, wg_ref, bg_ref, wu_ref, bu_ref, wd_ref, bd_ref, ys_ref,
                    wg_s, wu_s, wd_s):
    i = pl.program_id(0)

    @pl.when((i == 0) | (te_ref[i] != te_ref[jnp.maximum(i - 1, 0)]))
    def _():
        wg_s[...] = wg_ref[...].astype(BF16)
        wu_s[...] = wu_ref[...].astype(BF16)
        wd_s[...] = wd_ref[...].astype(BF16)

    @pl.when(i < nv_ref[0])
    def _():
        h = (_rms(xs_ref[...]) * g_ref[...]).astype(BF16)
        gate = jnp.minimum(_dot(h, wg_s[...]) + bg_ref[...], SWIGLU_LIMIT)
        up = jnp.clip(_dot(h, wu_s[...]) + bu_ref[...], -SWIGLU_LIMIT, SWIGLU_LIMIT)
        act = (up + 1.0) * gate * jax.nn.sigmoid(SWIGLU_ALPHA * gate)
        ys_ref[...] = _dot(act.astype(BF16), wd_s[...]) + bd_ref[...]

    @pl.when(i >= nv_ref[0])
    def _():
        ys_ref[...] = jnp.zeros_like(ys_ref)


def _experts(xs, tile_expert, n_valid, g_mlp, w_gate, b_gate, w_up, b_up, w_down, b_down):
    P, D = xs.shape
    tm = TM_EXPERT
    E, _, F = w_gate.shape
    w_spec = pl.BlockSpec((None, D, F), lambda i, te, nv: (te[i], 0, 0))
    wd_spec = pl.BlockSpec((None, F, D), lambda i, te, nv: (te[i], 0, 0))
    b_spec = pl.BlockSpec((None, 1, F), lambda i, te, nv: (te[i], 0, 0))
    bd_spec = pl.BlockSpec((None, 1, D), lambda i, te, nv: (te[i], 0, 0))
    x_spec = pl.BlockSpec((tm, D), lambda i, te, nv: (jnp.minimum(i, nv[0] - 1), 0))
    return pl.pallas_call(
        _experts_kernel,
        grid_spec=pltpu.PrefetchScalarGridSpec(
            num_scalar_prefetch=2,
            grid=(P // tm,),
            in_specs=[x_spec, pl.BlockSpec((1, D), lambda i, te, nv: (0, 0)),
                      w_spec, b_spec, w_spec, b_spec, wd_spec, bd_spec],
            out_specs=pl.BlockSpec((tm, D), lambda i, te, nv: (i, 0)),
            scratch_shapes=[pltpu.VMEM((D, F), BF16), pltpu.VMEM((D, F), BF16), pltpu.VMEM((F, D), BF16)]),
        out_shape=jax.ShapeDtypeStruct((P, D), F32),
        compiler_params=pltpu.CompilerParams(dimension_semantics=("arbitrary",), vmem_limit_bytes=VMEM_LIMIT),
        name="experts",
    )(tile_expert, n_valid, xs, g_mlp.reshape(1, D),
      w_gate, b_gate.reshape(E, 1, F), w_up, b_up.reshape(E, 1, F), w_down, b_down.reshape(E, 1, D))


def _combine_kernel(pos_ref, x_ref, wgt_ref, ys_ref, o_ref, buf_ref, sem):
    tm = x_ref.shape[0]

    def issue(j, carry):
        for u in range(ROWS_PER_ISSUE):
            r = j * ROWS_PER_ISSUE + u
            for k in range(TOP_K):
                _row_copy(ys_ref, pos_ref[r * TOP_K + k], buf_ref.at[k], r, sem).start(priority=k % 2)
        return carry

    lax.fori_loop(0, tm // ROWS_PER_ISSUE, issue, 0)
    for k in range(TOP_K):
        _rows_copy(ys_ref, buf_ref.at[k], tm, sem).wait()
    wgt = wgt_ref[...]
    acc = x_ref[...]
    for k in range(TOP_K):
        acc = acc + wgt[:, k:k + 1] * buf_ref[k]
    o_ref[...] = acc


def _combine(x1, wgt, pos_flat, ys):
    N, D = x1.shape
    tm = TM_COMBINE
    return pl.pallas_call(
        _combine_kernel,
        grid=(N // tm,),
        in_specs=[pl.BlockSpec((tm * TOP_K,), lambda i: (i,), memory_space=pltpu.SMEM),
                  pl.BlockSpec((tm, D), lambda i: (i, 0)),
                  pl.BlockSpec((tm, 128), lambda i: (i, 0)),
                  pl.BlockSpec(memory_space=pl.ANY)],
        out_specs=pl.BlockSpec((tm, D), lambda i: (i, 0)),
        out_shape=jax.ShapeDtypeStruct((N, D), F32),
        scratch_shapes=[pltpu.VMEM((TOP_K, tm, D), F32), pltpu.SemaphoreType.DMA],
        compiler_params=pltpu.CompilerParams(dimension_semantics=("arbitrary",), vmem_limit_bytes=VMEM_LIMIT),
        name="combine",
    )(pos_flat, x1, wgt, ys)


def _moe(x1, code, wgt, cnt, g_mlp, w_gate, b_gate, w_up, b_up, w_down, b_down):
    N, D = x1.shape
    tm = TM_EXPERT
    n_tiles = -(-(N * TOP_K + N_EXPERTS * (tm - 1)) // tm)
    counts = cnt[0, :N_EXPERTS]
    padded = ((counts + tm - 1) // tm) * tm
    ends = jnp.cumsum(padded)
    starts = ends - padded
    code = code[:, :TOP_K]
    pos = (starts[code // RANK_RADIX] + code % RANK_RADIX).reshape(-1).astype(I32)
    tile_start = jnp.arange(n_tiles, dtype=I32) * tm
    tile_expert = jnp.minimum(jnp.sum(ends[None, :] <= tile_start[:, None], axis=1), N_EXPERTS - 1).astype(I32)
    n_valid = (ends[-1] // tm).astype(I32).reshape(1)
    xs = _dispatch(x1, pos, (starts + counts).astype(I32), ends.astype(I32), n_tiles * tm)
    ys = _experts(xs, tile_expert, n_valid, g_mlp, w_gate, b_gate, w_up, b_up, w_down, b_down)
    return _combine(x1, wgt, pos, ys)


def _layer(x, g_attn, w_in, qn_nsa, kn_cmp, kn_sel, kn_win,
           pos_cmp_k, w_ck1, b_ck1, w_ck2, b_ck2, pos_cmp_v, w_cv1, b_cv1, w_cv2, b_cv2,
           b_nsa_gate, qn_fox, kn_fox, b_forget, w_up_nsa, w_up_fox, b_merge, w_out, g_mlp,
           w_router, b_router, w_e_gate, b_e_gate, w_e_up, b_e_up, w_e_down, b_e_down):
    B, T, D = x.shape
    qn, ks, vs, kw, vw, kc_raw, vc_raw, qf, kf, vf, gates = _in_proj(
        x, g_attn, w_in, qn_nsa, kn_sel, kn_win, b_nsa_gate, qn_fox, kn_fox, b_forget)
    kc, vc = _compress(kc_raw, vc_raw, pos_cmp_k, w_ck1, b_ck1, w_ck2, b_ck2,
                       pos_cmp_v, w_cv1, b_cv1, w_cv2, b_cv2, kn_cmp)
    o_nsa = _nsa(qn, ks, vs, kw, vw, kc, vc, gates)
    o_fox = _fox(qf, kf, vf)
    hw = NSA_HEADS * HEAD_PAD
    w_merge = w_in[:, w_in.shape[1] - 2 * D:]
    x1, code, wgt, cnt = _merge_out(x.reshape(B * T, D), g_attn, o_nsa.reshape(B * T, hw), o_fox.reshape(B * T, hw),
                                    w_merge, b_merge, w_up_nsa, w_up_fox, w_out, g_mlp, w_router, b_router)
    out = _moe(x1, code, wgt, cnt, g_mlp, w_e_gate, b_e_gate, w_e_up, b_e_up, w_e_down, b_e_down)
    return out.reshape(B, T, D)


def kernel(x, g_attn, w_in, qn_nsa, kn_cmp, kn_sel, kn_win, pos_cmp_k, w_ck1, b_ck1, w_ck2, b_ck2, pos_cmp_v, w_cv1, b_cv1, w_cv2, b_cv2, b_nsa_gate, qn_fox, kn_fox, b_forget, w_up_nsa, w_up_fox, b_merge, w_out, g_mlp, w_router, b_router, w_e_gate, b_e_gate, w_e_up, b_e_up, w_e_down, b_e_down):
    params = (g_attn, w_in, qn_nsa, kn_cmp, kn_sel, kn_win, pos_cmp_k, w_ck1, b_ck1, w_ck2, b_ck2,
              pos_cmp_v, w_cv1, b_cv1, w_cv2, b_cv2, b_nsa_gate, qn_fox, kn_fox, b_forget,
              w_up_nsa, w_up_fox, b_merge, w_out, g_mlp, w_router, b_router,
              w_e_gate, b_e_gate, w_e_up, b_e_up, w_e_down, b_e_down)
    for layer in range(g_attn.shape[0]):
        x = _layer(x, *[p[layer] for p in params])
    return x
```

```python
import functools

import jax
import jax.numpy as jnp
from jax import lax
from jax.experimental import pallas as pl
from jax.experimental.pallas import tpu as pltpu

F32 = jnp.float32
BF16 = jnp.bfloat16
I32 = jnp.int32

D_MODEL = 1024
HEAD_DIM = 64
HEAD_PAD = 128
NSA_HEADS = 8
NSA_GROUPS = 2
NSA_HPG = NSA_HEADS // NSA_GROUPS
FOX_HEADS = 8
CMP_BLOCK = 32
CMP_STRIDE = 16
CMP_HIDDEN = 256
N_CMP_PAD = 128
SEL_BLOCK = 64
SEL_TOPN = 16
N_SEL_BLOCKS = 32
WINDOW = 512
N_EXPERTS = 32
TOP_K = 4
SWIGLU_LIMIT = 7.0
SWIGLU_ALPHA = 1.702
RMS_EPS = 1e-6
NEG_BIG = -1e30
FORCED_SCORE = 1e9
QK_SCALE = HEAD_DIM ** -0.5

LANE_SEL = 64
LANE_POS = 96
LANE_FOX = 64

OFF_QN = 0
OFF_KS = OFF_QN + NSA_HEADS * HEAD_PAD
OFF_VS = OFF_KS + NSA_GROUPS * HEAD_PAD
OFF_KW = OFF_VS + NSA_GROUPS * HEAD_PAD
OFF_VW = OFF_KW + NSA_GROUPS * HEAD_PAD
OFF_KC = OFF_VW + NSA_GROUPS * HEAD_PAD
OFF_VC = OFF_KC + NSA_GROUPS * HEAD_DIM
OFF_QF = OFF_VC + NSA_GROUPS * HEAD_DIM
OFF_KF = OFF_QF + FOX_HEADS * HEAD_DIM
OFF_VF = OFF_KF + FOX_HEADS * HEAD_DIM
OFF_MISC = OFF_VF + FOX_HEADS * HEAD_DIM
W1_COLS = OFF_MISC + 128
MISC_GATES = 3 * NSA_HEADS

TM_PROJ = 256
TQ_NSA = 128
TQ_SELECT = 512
TILES_PER_REGION = 4
SELECTS_PER_REGION = 2
CK_ATTN = 256
TM_OUT = 512
TM_EXPERT = 512
TM_DISPATCH = 1024
TM_COMBINE = 512
ROWS_PER_ISSUE = 8
RANK_RADIX = 1 << 16
VMEM_LIMIT = 56 * 1024 * 1024


def _dot(a, b):
    return jnp.dot(a, b, preferred_element_type=F32)


def _dot_nt(a, b):
    return lax.dot_general(a, b, (((1,), (1,)), ((), ())), preferred_element_type=F32)


def _split3(v):
    hi = v.astype(BF16)
    r1 = v - hi.astype(F32)
    mid = r1.astype(BF16)
    lo = (r1 - mid.astype(F32)).astype(BF16)
    return hi, mid, lo


def _dot_exact_rhs(a_f32, b_bf16):
    hi, mid, lo = _split3(a_f32)
    return _dot(hi, b_bf16) + _dot(mid, b_bf16) + _dot(lo, b_bf16)


def _dot_exact_lhs(a_bf16, b_f32):
    hi, mid, lo = _split3(b_f32)
    return _dot(a_bf16, hi) + _dot(a_bf16, mid) + _dot(a_bf16, lo)


def _rms(x):
    return x * lax.rsqrt(jnp.mean(x * x, axis=-1, keepdims=True) + RMS_EPS)


def _head_rms(a, gain):
    ms = jnp.sum(a * a, axis=-1, keepdims=True) * (1.0 / HEAD_DIM)
    return a * lax.rsqrt(ms + RMS_EPS) * gain


def _pos_feat_k(pos, lane):
    hi = (256 * (pos >> 8)).astype(F32)
    lo = (pos & 255).astype(F32)
    return jnp.where(lane < LANE_POS + 2, 1.0, jnp.where(lane == LANE_POS + 2, hi, lo)) * (
        (lane >= LANE_POS) & (lane < LANE_POS + 4)).astype(F32)


def _in_proj_kernel(x_ref, g_ref, w_ref, gains_ref, bmisc_ref, pq_ref, pk_ref,
                    qn_ref, ks_ref, vs_ref, kw_ref, vw_ref, kc_ref, vc_ref,
                    qf_ref, kf_ref, vf_ref, gate_ref, carry_ref):
    tm = x_ref.shape[0]
    i = pl.program_id(1)
    h = (_rms(x_ref[...]) * g_ref[...]).astype(BF16)
    pos = i * tm + lax.broadcasted_iota(I32, (tm, 1), 0)
    lane = lax.broadcasted_iota(I32, (1, HEAD_PAD), 1)
    pos_hi = (256 * (pos >> 8)).astype(F32)
    pos_lo = (pos & 255).astype(F32)
    kfeat = _pos_feat_k(pos, lane)
    blk_onehot = ((lane - LANE_SEL) == (pos >> 6)).astype(F32)

    def seg(off, width):
        return _dot(h, w_ref[:, off:off + width])

    a = seg(OFF_QN, NSA_HEADS * HEAD_PAD)
    gq = gains_ref[0:1, :] * QK_SCALE
    for hd in range(NSA_HEADS):
        slope = 2.0 ** (-(hd + 1))
        qfeat = jnp.where(lane == LANE_POS, -slope * pos_hi,
                          jnp.where(lane == LANE_POS + 1, -slope * pos_lo,
                                    jnp.where((lane == LANE_POS + 2) | (lane == LANE_POS + 3), slope, 0.0)))
        sl = slice(hd * HEAD_PAD, (hd + 1) * HEAD_PAD)
        qn_ref[:, sl] = (_head_rms(a[:, sl], gq) + qfeat).astype(BF16)

    a = seg(OFF_KS, NSA_GROUPS * HEAD_PAD)
    for g in range(NSA_GROUPS):
        sl = slice(g * HEAD_PAD, (g + 1) * HEAD_PAD)
        ks_ref[:, sl] = (_head_rms(a[:, sl], gains_ref[1:2, :]) + kfeat + blk_onehot).astype(BF16)
    vs_ref[...] = seg(OFF_VS, NSA_GROUPS * HEAD_PAD).astype(BF16)
    a = seg(OFF_KW, NSA_GROUPS * HEAD_PAD)
    for g in range(NSA_GROUPS):
        sl = slice(g * HEAD_PAD, (g + 1) * HEAD_PAD)
        kw_ref[:, sl] = (_head_rms(a[:, sl], gains_ref[2:3, :]) + kfeat).astype(BF16)
    vw_ref[...] = seg(OFF_VW, NSA_GROUPS * HEAD_PAD).astype(BF16)
    kc_ref[...] = seg(OFF_KC, NSA_GROUPS * HEAD_DIM).astype(BF16)
    vc_ref[...] = seg(OFF_VC, NSA_GROUPS * HEAD_DIM).astype(BF16)

    misc = seg(OFF_MISC, 128) + bmisc_ref[...]
    is_gate = lane < MISC_GATES
    gate_ref[...] = jnp.where(is_gate, jax.nn.sigmoid(misc), 0.0)
    is_f = (lane >= MISC_GATES) & (lane < MISC_GATES + 3 * FOX_HEADS)
    logf = jnp.where(is_f, jax.nn.log_sigmoid(misc), 0.0)

    @pl.when(i == 0)
    def _():
        carry_ref[...] = jnp.zeros_like(carry_ref)

    r = lax.broadcasted_iota(I32, (tm, tm), 0)
    c = lax.broadcasted_iota(I32, (tm, tm), 1)
    tri = (r >= c).astype(BF16)
    csum = _dot_exact_lhs(tri, logf) + carry_ref[...]
    carry_ref[...] = csum[tm - 1:tm, :]
    c_hi, c_mid, c_lo = _split3(csum)
    c3 = jnp.where(lane < MISC_GATES + FOX_HEADS, c_hi,
                   jnp.where(lane < MISC_GATES + 2 * FOX_HEADS, c_mid, c_lo))
    c3 = jnp.where(is_f, c3, jnp.zeros_like(c3))
    low = lane < HEAD_DIM

    def fox_heads(off, gain2, feat, ones_at, out_ref):
        a = seg(off, FOX_HEADS * HEAD_DIM)
        for j in range(FOX_HEADS // 2):
            pair = a[:, j * HEAD_PAD:(j + 1) * HEAD_PAD]
            sq = pair * pair
            ms_even = jnp.sum(jnp.where(low, sq, 0.0), axis=-1, keepdims=True) * (1.0 / HEAD_DIM)
            ms_odd = jnp.sum(jnp.where(low, 0.0, sq), axis=-1, keepdims=True) * (1.0 / HEAD_DIM)
            normed = pair * jnp.where(low, lax.rsqrt(ms_even + RMS_EPS), lax.rsqrt(ms_odd + RMS_EPS)) * gain2
            for parity in range(2):
                hd = 2 * j + parity
                base = _fox_feature_base(hd)
                ones = ((lane >= base + ones_at) & (lane < base + ones_at + 3)).astype(F32)
                keep = low if parity == 0 else jnp.logical_not(low)
                slab = jnp.where(keep, normed, 0.0) + feat[:, hd * HEAD_PAD:(hd + 1) * HEAD_PAD] + ones
                out_ref[:, hd * HEAD_PAD:(hd + 1) * HEAD_PAD] = slab.astype(BF16)

    fox_heads(OFF_QF, gains_ref[5:6, :] * QK_SCALE, _dot(c3, pq_ref[...]), 3, qf_ref)
    fox_heads(OFF_KF, gains_ref[6:7, :], _dot(c3, pk_ref[...]), 0, kf_ref)
    vf_ref[...] = seg(OFF_VF, FOX_HEADS * HEAD_DIM).astype(BF16)


def _pad_heads(w, n_heads):
    lead = w.shape[:-1]
    w = w.reshape(lead + (n_heads, HEAD_DIM))
    w = jnp.pad(w, [(0, 0)] * len(lead) + [(0, 0), (0, HEAD_PAD - HEAD_DIM)])
    return w.reshape(lead + (n_heads * HEAD_PAD,))


def _pad_gain(g):
    return jnp.pad(g.astype(F32), (0, HEAD_PAD - HEAD_DIM))


def _fox_feature_base(hd):
    return HEAD_DIM if hd % 2 == 0 else 0


def _fox_placement():
    pq = jnp.zeros((128, FOX_HEADS * HEAD_PAD), F32)
    pk = jnp.zeros((128, FOX_HEADS * HEAD_PAD), F32)
    for piece in range(3):
        for hd in range(FOX_HEADS):
            src = MISC_GATES + piece * FOX_HEADS + hd
            base = hd * HEAD_PAD + _fox_feature_base(hd)
            pq = pq.at[src, base + piece].set(1.0)
            pk = pk.at[src, base + 3 + piece].set(-1.0)
    return pq.astype(BF16), pk.astype(BF16)


def _in_proj(x, g_attn, w_in, qn_nsa, kn_sel, kn_win, b_nsa_gate, qn_fox, kn_fox, b_forget):
    B, T, D = x.shape
    s1 = NSA_HEADS * HEAD_DIM
    kv_w = NSA_GROUPS * HEAD_DIM
    s2 = s1 + 6 * kv_w
    s3 = s2 + 3 * NSA_HEADS
    fw = FOX_HEADS * HEAD_DIM
    s4 = s3 + 3 * fw
    s5 = s4 + FOX_HEADS
    kv = [w_in[:, s1 + j * kv_w: s1 + (j + 1) * kv_w] for j in range(6)]
    f_cols = w_in[:, s4:s5]
    misc = jnp.concatenate([w_in[:, s2:s3], f_cols, f_cols, f_cols,
                            jnp.zeros((D, 128 - MISC_GATES - 3 * FOX_HEADS), F32)], axis=1)
    w1 = jnp.concatenate([
        _pad_heads(w_in[:, :s1], NSA_HEADS),
        _pad_heads(kv[2], NSA_GROUPS), _pad_heads(kv[3], NSA_GROUPS),
        _pad_heads(kv[4], NSA_GROUPS), _pad_heads(kv[5], NSA_GROUPS),
        kv[0], kv[1],
        w_in[:, s3:s4],
        misc], axis=1).astype(BF16)
    assert w1.shape[1] == W1_COLS
    gains = jnp.stack([_pad_gain(qn_nsa), _pad_gain(kn_sel), _pad_gain(kn_win),
                       _pad_gain(qn_fox), _pad_gain(kn_fox),
                       jnp.tile(qn_fox.astype(F32), 2), jnp.tile(kn_fox.astype(F32), 2), jnp.zeros(HEAD_PAD, F32)])
    bmisc = jnp.concatenate([b_nsa_gate, b_forget, b_forget, b_forget,
                             jnp.zeros(128 - MISC_GATES - 3 * FOX_HEADS, F32)]).reshape(1, 128)
    pq, pk = _fox_placement()
    tm = TM_PROJ

    def tok(width):
        return pl.BlockSpec((None, tm, width), lambda b, i: (b, i, 0))

    def full(shape):
        return pl.BlockSpec(shape, lambda b, i: (0,) * len(shape))

    widths = [NSA_HEADS * HEAD_PAD] + [NSA_GROUPS * HEAD_PAD] * 4 + [NSA_GROUPS * HEAD_DIM] * 2 + \
             [FOX_HEADS * HEAD_PAD] * 2 + [FOX_HEADS * HEAD_DIM]
    out_shape = [jax.ShapeDtypeStruct((B, T, w), BF16) for w in widths] + \
                [jax.ShapeDtypeStruct((B, T, 128), F32)]
    return pl.pallas_call(
        _in_proj_kernel,
        grid=(B, T // tm),
        in_specs=[tok(D), full((1, D)), full((D, W1_COLS)), full((8, HEAD_PAD)), full((1, 128)),
                  full((128, FOX_HEADS * HEAD_PAD)), full((128, FOX_HEADS * HEAD_PAD))],
        out_specs=[tok(w) for w in widths] + [tok(128)],
        out_shape=out_shape,
        scratch_shapes=[pltpu.VMEM((1, 128), F32)],
        compiler_params=pltpu.CompilerParams(
            dimension_semantics=("arbitrary", "arbitrary"), vmem_limit_bytes=VMEM_LIMIT),
        name="in_proj",
    )(x, g_attn.reshape(1, D), w1, gains, bmisc, pq, pk)


def _gelu_tanh(x):
    return 0.5 * x * (1.0 + jnp.tanh(0.7978845608028654 * (x + 0.044715 * (x * x * x))))


def _compress_kernel(ak_ref, av_ref, posk_ref, posv_ref, wk1_ref, bk1_ref, wk2_ref, bk2_ref,
                     wv1_ref, bv1_ref, wv2_ref, bv2_ref, gain_ref, kc_ref, vc_ref):
    row = lax.broadcasted_iota(I32, (N_CMP_PAD, 1), 0)
    lane = lax.broadcasted_iota(I32, (1, HEAD_PAD), 1)
    kfeat = _pos_feat_k(CMP_STRIDE * row + CMP_BLOCK - 1, lane)

    def mlp(a_ref, pos_ref, w1_ref, b1_ref, w2_ref, b2_ref):
        a = a_ref[...].astype(F32)
        lo = (a + pos_ref[0:1, :]).astype(BF16)
        hi = (a + pos_ref[1:2, :]).astype(BF16)
        outs = []
        for g in range(NSA_GROUPS):
            hid = _dot(lo, w1_ref[0, g]) + pltpu.roll(_dot(hi, w1_ref[1, g]), N_CMP_PAD - 1, 0) + b1_ref[...]
            outs.append(_dot(_gelu_tanh(hid).astype(BF16), w2_ref[...]) + b2_ref[...])
        return outs

    ks = mlp(ak_ref, posk_ref, wk1_ref, bk1_ref, wk2_ref, bk2_ref)
    vs = mlp(av_ref, posv_ref, wv1_ref, bv1_ref, wv2_ref, bv2_ref)
    for g in range(NSA_GROUPS):
        kc_ref[g] = (_head_rms(ks[g], gain_ref[...]) + kfeat).astype(BF16)
        vc_ref[g] = vs[g].astype(BF16)


def _compress(kc_raw, vc_raw, pos_k, w_ck1, b_ck1, w_ck2, b_ck2, pos_v, w_cv1, b_cv1, w_cv2, b_cv2, kn_cmp):
    B, T, _ = kc_raw.shape
    n_chunks = T // CMP_STRIDE
    assert n_chunks == N_CMP_PAD
    chunk_w = CMP_STRIDE * NSA_GROUPS * HEAD_DIM

    def chunks(a):
        return a.reshape(B, n_chunks, chunk_w)

    def w1_rows(w):
        w = w.reshape(2, CMP_STRIDE, 1, HEAD_DIM, CMP_HIDDEN)
        per_group = [jnp.concatenate([w if gg == g else jnp.zeros_like(w) for gg in range(NSA_GROUPS)], axis=2)
                     for g in range(NSA_GROUPS)]
        return jnp.stack(per_group, axis=1).reshape(2, NSA_GROUPS, chunk_w, CMP_HIDDEN).astype(BF16)

    def pos_rows(p):
        p = jnp.broadcast_to(p.reshape(2, CMP_STRIDE, 1, HEAD_DIM), (2, CMP_STRIDE, NSA_GROUPS, HEAD_DIM))
        return p.reshape(2, chunk_w)

    def w2pad(w, b):
        return (jnp.pad(w, ((0, 0), (0, HEAD_PAD - HEAD_DIM))).astype(BF16),
                jnp.pad(b, (0, HEAD_PAD - HEAD_DIM)).reshape(1, HEAD_PAD))

    wk2, bk2 = w2pad(w_ck2, b_ck2)
    wv2, bv2 = w2pad(w_cv2, b_cv2)
    a_spec = pl.BlockSpec((None, n_chunks, chunk_w), lambda b: (b, 0, 0))
    o_spec = pl.BlockSpec((None, NSA_GROUPS, N_CMP_PAD, HEAD_PAD), lambda b: (b, 0, 0, 0))

    def full(shape):
        return pl.BlockSpec(shape, lambda b: (0,) * len(shape))

    w1_shape = (2, NSA_GROUPS, chunk_w, CMP_HIDDEN)
    return pl.pallas_call(
        _compress_kernel,
        grid=(B,),
        in_specs=[a_spec, a_spec, full((2, chunk_w)), full((2, chunk_w)),
                  full(w1_shape), full((1, CMP_HIDDEN)), full((CMP_HIDDEN, HEAD_PAD)), full((1, HEAD_PAD)),
                  full(w1_shape), full((1, CMP_HIDDEN)), full((CMP_HIDDEN, HEAD_PAD)), full((1, HEAD_PAD)),
                  full((1, HEAD_PAD))],
        out_specs=[o_spec, o_spec],
        out_shape=[jax.ShapeDtypeStruct((B, NSA_GROUPS, N_CMP_PAD, HEAD_PAD), BF16)] * 2,
        compiler_params=pltpu.CompilerParams(dimension_semantics=("arbitrary",), vmem_limit_bytes=VMEM_LIMIT),
        name="compress",
    )(chunks(kc_raw), chunks(vc_raw), pos_rows(pos_k), pos_rows(pos_v),
      w1_rows(w_ck1), b_ck1.reshape(1, -1), wk2, bk2,
      w1_rows(w_cv1), b_cv1.reshape(1, -1), wv2, bv2, _pad_gain(kn_cmp).reshape(1, HEAD_PAD))


def _lane_blocks(a):
    return [a[:, i * 128:(i + 1) * 128] for i in range(a.shape[1] // 128)]


def _fold_blocks(op, acc, a):
    for blk in _lane_blocks(a):
        acc = blk if acc is None else op(acc, blk)
    return acc


def _nsa_kernel(q_ref, ks_ref, vs_ref, kw_ref, vw_ref, kc_ref, vct_ref, gate_ref, ovl_ref, o_ref,
                sel_ref, oc_ref, s_ref):
    seq = q_ref.shape[0]
    tq = TQ_NSA
    ta = TQ_SELECT
    ck = CK_ATTN
    span = WINDOW + tq
    g = pl.program_id(1)
    col_l = lax.broadcasted_iota(I32, (1, 128), 1)
    c_end_sub = CMP_STRIDE * lax.broadcasted_iota(I32, (N_CMP_PAD, 1), 0) + (CMP_BLOCK - 1)
    blk_id = lax.broadcasted_iota(I32, (N_SEL_BLOCKS, 1), 0)
    sub_id = lax.broadcasted_iota(I32, (8, 1), 0)

    def gate_col(gates, branch, hh):
        lo = branch * NSA_HEADS + hh
        return jnp.where(g == 0, gates[:, lo:lo + 1], gates[:, lo + NSA_HPG:lo + NSA_HPG + 1])

    def select(a):
        a0 = a * ta
        t_lane = a0 + lax.broadcasted_iota(I32, (1, ta), 1)
        gates = gate_ref[a0:a0 + ta, :]

        mask_t = t_lane >= c_end_sub
        p_sum = None
        for hh in range(NSA_HPG):
            q_h = q_ref[a0:a0 + ta, hh * HEAD_PAD:(hh + 1) * HEAD_PAD]
            s_t = jnp.where(mask_t, _dot_nt(kc_ref[...], q_h), NEG_BIG)
            e_t = jnp.where(mask_t, jnp.exp(s_t - jnp.max(s_t, axis=0, keepdims=True)), 0.0)
            l_t = jnp.sum(e_t, axis=0, keepdims=True)
            p_t = e_t / jnp.where(l_t > 0.0, l_t, 1.0)
            o_t = _dot(vct_ref[...], p_t.astype(BF16))
            g_c = gate_col(gates, 0, hh)
            for blk in range(ta // 128):
                bs = slice(blk * 128, (blk + 1) * 128)
                oc_ref[a0 + blk * 128:a0 + (blk + 1) * 128, hh * HEAD_PAD:(hh + 1) * HEAD_PAD] = g_c[bs] * o_t[:, bs].T
            p_sum = p_t if p_sum is None else p_sum + p_t
        imp = _dot_exact_lhs(ovl_ref[...], p_sum)
        tb = t_lane >> 6
        valid = blk_id <= tb
        forced = (blk_id == 0) | (blk_id == tb) | (blk_id == tb - 1)
        score = jnp.where(forced & valid, FORCED_SCORE, imp)
        score = jnp.where(valid, score, -1.0)
        groups = [score[8 * b:8 * b + 8, :] for b in range(N_SEL_BLOCKS // 8)]
        n_above = [jnp.zeros((8, ta), F32) for _ in groups]
        for j in range(N_SEL_BLOCKS):
            cj = score[j:j + 1, :]
            for b, grp in enumerate(groups):
                if 8 * b > j:
                    n_above[b] = n_above[b] + jnp.where(cj >= grp, 1.0, 0.0)
                elif 8 * b + 7 < j:
                    n_above[b] = n_above[b] + jnp.where(cj > grp, 1.0, 0.0)
                else:
                    tie = jnp.where(sub_id + 8 * b > j, 1.0, 0.0)
                    n_above[b] = n_above[b] + jnp.where(cj > grp, 1.0, 0.0) + jnp.where(cj == grp, tie, 0.0)
        dropped = jnp.where(jnp.concatenate(n_above, axis=0) >= float(SEL_TOPN), NEG_BIG, 0.0)
        drop_t = jnp.concatenate([jnp.zeros((LANE_SEL, ta), F32), dropped,
                                  jnp.zeros((HEAD_PAD - LANE_SEL - N_SEL_BLOCKS, ta), F32)], axis=0)
        for blk in range(ta // 128):
            sel_ref[a0 + blk * 128:a0 + (blk + 1) * 128, :] = drop_t[:, blk * 128:(blk + 1) * 128].T.astype(BF16)

    for first in range(0, seq // ta, SELECTS_PER_REGION):
        @pl.when(g >= -first)
        def _(first=first):
            for a in range(first, first + SELECTS_PER_REGION):
                select(a)

    rows = NSA_HPG * tq
    row_l = lax.broadcasted_iota(I32, (rows, 1), 0) & (tq - 1)

    def attend(i):
        t0 = i * tq
        t_row = t0 + row_l
        q4 = jnp.concatenate([q_ref[t0:t0 + tq, hh * HEAD_PAD:(hh + 1) * HEAD_PAD] for hh in range(NSA_HPG)], axis=0)
        q4s = q4 + jnp.concatenate([sel_ref[t0:t0 + tq, :]] * NSA_HPG, axis=0)

        n_chunks = (t0 + tq - 1) // ck + 1
        slot = i % 2
        mx = None
        for c in range(n_chunks):
            s = _dot_nt(q4s, ks_ref[c * ck:(c + 1) * ck, :])
            if (c + 1) * ck > t0:
                s = jnp.where(t_row >= c * ck + lax.broadcasted_iota(I32, (1, ck), 1), s, NEG_BIG)
            s_ref[slot, :, c * ck:(c + 1) * ck] = s
            mx = _fold_blocks(jnp.maximum, mx, s)
        m = jnp.max(mx, axis=-1, keepdims=True)
        ls = None
        acc = None
        for c in range(n_chunks):
            p = jnp.exp(s_ref[slot, :, c * ck:(c + 1) * ck] - m)
            ls = _fold_blocks(jnp.add, ls, p)
            pv = _dot(p.astype(BF16), vs_ref[c * ck:(c + 1) * ck, :])
            acc = pv if acc is None else acc + pv
        o_s = acc / jnp.sum(ls, axis=-1, keepdims=True)

        ws = max(t0 - WINDOW, 0)
        blocks = _lane_blocks(_dot_nt(q4, kw_ref[ws:ws + span, :]))
        if t0 < WINDOW:
            blocks = [jnp.where(t_row >= ws + jb * 128 + col_l, blk, NEG_BIG) if ws + (jb + 1) * 128 > t0 else blk
                      for jb, blk in enumerate(blocks)]
        else:
            blocks[0] = jnp.where(col_l > row_l, blocks[0], NEG_BIG)
            blocks[-1] = jnp.where(col_l <= row_l, blocks[-1], NEG_BIG)
        s = jnp.concatenate(blocks, axis=1)
        m = jnp.max(_fold_blocks(jnp.maximum, None, s), axis=-1, keepdims=True)
        p = jnp.exp(s - m)
        l = jnp.sum(_fold_blocks(jnp.add, None, p), axis=-1, keepdims=True)
        o_w = _dot(p.astype(BF16), vw_ref[ws:ws + span, :]) / l

        gates = gate_ref[t0:t0 + tq, :]
        outs = []
        for hh in range(NSA_HPG):
            rs = slice(hh * tq, (hh + 1) * tq)
            o = oc_ref[t0:t0 + tq, hh * HEAD_PAD:(hh + 1) * HEAD_PAD]
            outs.append((o + gate_col(gates, 1, hh) * o_s[rs] + gate_col(gates, 2, hh) * o_w[rs]).astype(BF16))
        o_ref[t0:t0 + tq, :] = jnp.concatenate(outs, axis=1)

    for first in range(0, seq // tq, TILES_PER_REGION):
        @pl.when(g > -1 - first)
        def _(first=first):
            for i in range(first, first + TILES_PER_REGION):
                attend(i)


def _overlap_matrix():
    c = jnp.arange(N_CMP_PAD)[None, :]
    j = jnp.arange(N_SEL_BLOCKS)[:, None]
    lo = (CMP_STRIDE * c) // SEL_BLOCK
    hi = (CMP_STRIDE * c + CMP_BLOCK - 1) // SEL_BLOCK
    return ((j == lo) | (j == hi)).astype(BF16)


def _nsa(qn, ks, vs, kw, vw, kc, vc, gates):
    B, T, _ = qn.shape
    gw = NSA_HPG * HEAD_PAD
    q_spec = pl.BlockSpec((None, T, gw), lambda b, g: (b, 0, g))
    kv_spec = pl.BlockSpec((None, T, HEAD_PAD), lambda b, g: (b, 0, g))
    c_spec = pl.BlockSpec((None, None, N_CMP_PAD, HEAD_PAD), lambda b, g: (b, g, 0, 0))
    return pl.pallas_call(
        _nsa_kernel,
        grid=(B, NSA_GROUPS),
        in_specs=[q_spec, kv_spec, kv_spec, kv_spec, kv_spec, c_spec, c_spec,
                  pl.BlockSpec((None, T, 128), lambda b, g: (b, 0, 0)),
                  pl.BlockSpec((N_SEL_BLOCKS, N_CMP_PAD), lambda b, g: (0, 0))],
        out_specs=q_spec,
        out_shape=jax.ShapeDtypeStruct((B, T, NSA_HEADS * HEAD_PAD), BF16),
        scratch_shapes=[pltpu.VMEM((T, HEAD_PAD), BF16), pltpu.VMEM((T, gw), F32),
                        pltpu.VMEM((2, NSA_HPG * TQ_NSA, T), F32)],
        compiler_params=pltpu.CompilerParams(
            dimension_semantics=("arbitrary", "arbitrary"), vmem_limit_bytes=VMEM_LIMIT),
        name="nsa",
    )(qn, ks, vs, kw, vw, kc, vc.swapaxes(2, 3), gates, _overlap_matrix())


def _fox_kernel(q_ref, k_ref, v_ref, o_ref):
    seq = q_ref.shape[0]
    ck = CK_ATTN
    row = lax.broadcasted_iota(I32, (ck, 1), 0)
    col = lax.broadcasted_iota(I32, (1, ck), 1)
    for qi in range(seq // ck):
        q = q_ref[qi * ck:(qi + 1) * ck, :]
        s_chunks = []
        mx = None
        for c in range(qi + 1):
            s = _dot_nt(q, k_ref[c * ck:(c + 1) * ck, :])
            if c == qi:
                s = jnp.where(row >= col, s, NEG_BIG)
            s_chunks.append(s)
            mx = _fold_blocks(jnp.maximum, mx, s)
        m = jnp.max(mx, axis=-1, keepdims=True)
        ls = None
        acc = None
        for c in range(qi + 1):
            p = jnp.exp(s_chunks[c] - m)
            ls = _fold_blocks(jnp.add, ls, p)
            pv = _dot(p.astype(BF16), v_ref[c * ck:(c + 1) * ck, :])
            acc = pv if acc is None else acc + pv
        o_ref[qi * ck:(qi + 1) * ck, :] = (acc / jnp.sum(ls, axis=-1, keepdims=True)).astype(BF16)


def _fox(qf, kf, vf):
    B, T, _ = qf.shape
    spec = pl.BlockSpec((None, T, HEAD_PAD), lambda b, h: (b, 0, h))
    v_spec = pl.BlockSpec((None, T, HEAD_PAD), lambda b, h: (b, 0, h // 2))
    return pl.pallas_call(
        _fox_kernel,
        grid=(B, FOX_HEADS),
        in_specs=[spec, spec, v_spec],
        out_specs=spec,
        out_shape=jax.ShapeDtypeStruct((B, T, FOX_HEADS * HEAD_PAD), BF16),
        compiler_params=pltpu.CompilerParams(
            dimension_semantics=("arbitrary", "arbitrary"), vmem_limit_bytes=VMEM_LIMIT),
        name="fox",
    )(qf, kf, vf)


def _merge_out_kernel(x_ref, g_ref, on_ref, of_ref, wm_ref, bm_ref, wun_ref, wuf_ref, wo_ref,
                      gm_ref, wrh_ref, wrm_ref, br_ref, o_ref, code_ref, wgt_ref, cnt_ref, carry_ref):
    x = x_ref[...]
    h = (_rms(x) * g_ref[...]).astype(BF16)
    merge = jax.nn.sigmoid(_dot(h, wm_ref[...]) + bm_ref[...])
    y = merge[:, :D_MODEL] * _dot(on_ref[...], wun_ref[...]) + merge[:, D_MODEL:] * _dot(of_ref[...], wuf_ref[...])
    x1 = x + _dot(y.astype(BF16), wo_ref[...])
    o_ref[...] = x1
    _route_tile(x1, gm_ref, wrh_ref, wrm_ref, br_ref, code_ref, wgt_ref, cnt_ref, carry_ref)


def _merge_out(x2, g_attn, o_nsa, o_fox, w_merge, b_merge, w_up_nsa, w_up_fox, w_out, g_mlp, w_router, b_router):
    N, D = x2.shape
    tm = TM_OUT
    assert N <= RANK_RADIX
    wr = jnp.pad(w_router, ((0, 0), (0, 128 - N_EXPERTS)))
    wr_hi = wr.astype(BF16)
    wr_mid = (wr - wr_hi.astype(F32)).astype(BF16)
    br = jnp.pad(b_router, (0, 128 - N_EXPERTS)).reshape(1, 128)

    def pad_rows(w, n_heads):
        return _pad_heads(w.T, n_heads).T.astype(BF16)

    def fox_rows(w):
        w = w.reshape(FOX_HEADS, HEAD_DIM, D)
        z = jnp.zeros_like(w)
        even = jnp.concatenate([w, z], axis=1)
        odd = jnp.concatenate([z, w], axis=1)
        parity = (jnp.arange(FOX_HEADS) % 2 == 0)[:, None, None]
        return jnp.where(parity, even, odd).reshape(FOX_HEADS * HEAD_PAD, D).astype(BF16)

    def tok(width):
        return pl.BlockSpec((tm, width), lambda i: (i, 0))

    def full(shape):
        return pl.BlockSpec(shape, lambda i: (0,) * len(shape))

    hw = NSA_HEADS * HEAD_PAD
    return pl.pallas_call(
        _merge_out_kernel,
        grid=(N // tm,),
        in_specs=[tok(D), full((1, D)), tok(hw), tok(hw), full((D, 2 * D)), full((1, 2 * D)),
                  full((hw, D)), full((hw, D)), full((D, D)),
                  full((1, D)), full((D, 128)), full((D, 128)), full((1, 128))],
        out_specs=[tok(D), tok(128), tok(128), full((1, 128))],
        out_shape=[jax.ShapeDtypeStruct((N, D), F32), jax.ShapeDtypeStruct((N, 128), I32),
                   jax.ShapeDtypeStruct((N, 128), F32), jax.ShapeDtypeStruct((1, 128), I32)],
        scratch_shapes=[pltpu.VMEM((1, 128), F32)],
        compiler_params=pltpu.CompilerParams(dimension_semantics=("arbitrary",), vmem_limit_bytes=VMEM_LIMIT),
        name="merge_out",
    )(x2, g_attn.reshape(1, D), o_nsa, o_fox, w_merge.astype(BF16), b_merge.reshape(1, 2 * D),
      pad_rows(w_up_nsa, NSA_HEADS), fox_rows(w_up_fox), w_out.astype(BF16),
      g_mlp.reshape(1, D), wr_hi, wr_mid, br)


def _route_tile(x, g_ref, wh_ref, wm_ref, br_ref, code_ref, wgt_ref, cnt_ref, carry_ref):
    tm = x.shape[0]

    @pl.when(pl.program_id(0) == 0)
    def _():
        carry_ref[...] = jnp.zeros_like(carry_ref)

    h = _rms(x) * g_ref[...]
    h_hi = h.astype(BF16)
    h_mid = (h - h_hi.astype(F32)).astype(BF16)
    logits = _dot(h_hi, wh_ref[...]) + _dot(h_hi, wm_ref[...]) + _dot(h_mid, wh_ref[...]) + br_ref[...]
    lane = lax.broadcasted_iota(I32, (1, 128), 1)
    lane_f = lane.astype(F32)
    work = jnp.where(lane < N_EXPERTS, logits, -jnp.inf)
    vals, hots = [], []
    idx_out = jnp.zeros((tm, 128), F32)
    for k in range(TOP_K):
        m = jnp.max(work, axis=-1, keepdims=True)
        idx = jnp.min(jnp.where(work == m, lane_f, 128.0), axis=-1, keepdims=True)
        hot = lane_f == idx
        work = jnp.where(hot, -jnp.inf, work)
        vals.append(m)
        hots.append(hot)
        idx_out = idx_out + jnp.where(lane == k, idx, 0.0)
    e = [jnp.exp(v - vals[0]) for v in vals]
    denom = e[0] + e[1] + e[2] + e[3]
    wgt = jnp.zeros((tm, 128), F32)
    for k in range(TOP_K):
        wgt = wgt + jnp.where(lane == k, e[k] / denom, 0.0)

    multi = (hots[0] | hots[1] | hots[2] | hots[3]).astype(F32)
    r = lax.broadcasted_iota(I32, (tm, tm), 0)
    c = lax.broadcasted_iota(I32, (tm, tm), 1)
    before = _dot((r > c).astype(BF16), multi.astype(BF16)) + carry_ref[...]
    rank = jnp.zeros((tm, 128), F32)
    for k in range(TOP_K):
        rk = jnp.sum(jnp.where(hots[k], before, 0.0), axis=-1, keepdims=True)
        rank = rank + jnp.where(lane == k, rk, 0.0)
    total = carry_ref[...] + jnp.sum(multi, axis=0, keepdims=True)
    carry_ref[...] = total
    code_ref[...] = idx_out.astype(I32) * RANK_RADIX + rank.astype(I32)
    wgt_ref[...] = wgt
    cnt_ref[...] = total.astype(I32)


def _row_copy(src_ref, src_row, dst_ref, dst_row, sem):
    return pltpu.make_async_copy(src_ref.at[pl.ds(src_row, 1), :], dst_ref.at[pl.ds(dst_row, 1), :], sem)


def _rows_copy(src_ref, dst_ref, n_rows, sem):
    return pltpu.make_async_copy(src_ref.at[pl.ds(0, n_rows), :], dst_ref.at[pl.ds(0, n_rows), :], sem)


def _dispatch_kernel(pad_lo_ref, pad_hi_ref, pos_ref, x_ref, xs_ref, zero_ref, sem, zsem):
    tm = x_ref.shape[0]

    def issue(j, carry):
        for u in range(ROWS_PER_ISSUE):
            r = j * ROWS_PER_ISSUE + u
            for k in range(TOP_K):
                _row_copy(x_ref, r, xs_ref, pos_ref[r * TOP_K + k], sem).start(priority=k % 2)
        return carry

    lax.fori_loop(0, tm // ROWS_PER_ISSUE, issue, 0)

    @pl.when(pl.program_id(0) == pl.num_programs(0) - 1)
    def _():
        zero_ref[...] = jnp.zeros_like(zero_ref)
        for e in range(N_EXPERTS):
            def pad(p, carry):
                _row_copy(zero_ref, 0, xs_ref, p, zsem).start()
                return carry
            lax.fori_loop(pad_lo_ref[e], pad_hi_ref[e], pad, 0)
        for e in range(N_EXPERTS):
            def pad_wait(p, carry):
                _row_copy(zero_ref, 0, xs_ref, 0, zsem).wait()
                return carry
            lax.fori_loop(pad_lo_ref[e], pad_hi_ref[e], pad_wait, 0)

    for k in range(TOP_K):
        _rows_copy(x_ref, xs_ref, tm, sem).wait()


def _dispatch(x1, pos_flat, pad_lo, pad_hi, n_rows):
    N, D = x1.shape
    tm = TM_DISPATCH
    return pl.pallas_call(
        _dispatch_kernel,
        grid_spec=pltpu.PrefetchScalarGridSpec(
            num_scalar_prefetch=2,
            grid=(N // tm,),
            in_specs=[pl.BlockSpec((tm * TOP_K,), lambda i, lo, hi: (i,), memory_space=pltpu.SMEM),
                      pl.BlockSpec((tm, D), lambda i, lo, hi: (i, 0))],
            out_specs=pl.BlockSpec(memory_space=pl.ANY),
            scratch_shapes=[pltpu.VMEM((8, D), F32), pltpu.SemaphoreType.DMA, pltpu.SemaphoreType.DMA]),
        out_shape=jax.ShapeDtypeStruct((n_rows, D), F32),
        compiler_params=pltpu.CompilerParams(dimension_semantics=("arbitrary",), vmem_limit_bytes=VMEM_LIMIT),
        name="dispatch",
    )(pad_lo, pad_hi, pos_flat, x1)


def _experts_kernel(te_ref, nv_ref, xs_ref, g_ref, wg_ref, bg_ref, wu_ref, bu_ref, wd_ref, bd_ref, ys_ref,
                    wg_s, wu_s, wd_s):
    i = pl.program_id(0)

    @pl.when((i == 0) | (te_ref[i] != te_ref[jnp.maximum(i - 1, 0)]))
    def _():
        wg_s[...] = wg_ref[...].astype(BF16)
        wu_s[...] = wu_ref[...].astype(BF16)
        wd_s[...] = wd_ref[...].astype(BF16)

    @pl.when(i < nv_ref[0])
    def _():
        h = (_rms(xs_ref[...]) * g_ref[...]).astype(BF16)
        gate = jnp.minimum(_dot(h, wg_s[...]) + bg_ref[...], SWIGLU_LIMIT)
        up = jnp.clip(_dot(h, wu_s[...]) + bu_ref[...], -SWIGLU_LIMIT, SWIGLU_LIMIT)
        act = (up + 1.0) * gate * jax.nn.sigmoid(SWIGLU_ALPHA * gate)
        ys_ref[...] = _dot(act.astype(BF16), wd_s[...]) + bd_ref[...]

    @pl.when(i >= nv_ref[0])
    def _():
        ys_ref[...] = jnp.zeros_like(ys_ref)


def _experts(xs, tile_expert, n_valid, g_mlp, w_gate, b_gate, w_up, b_up, w_down, b_down):
    P, D = xs.shape
    tm = TM_EXPERT
    E, _, F = w_gate.shape
    w_spec = pl.BlockSpec((None, D, F), lambda i, te, nv: (te[i], 0, 0))
    wd_spec = pl.BlockSpec((None, F, D), lambda i, te, nv: (te[i], 0, 0))
    b_spec = pl.BlockSpec((None, 1, F), lambda i, te, nv: (te[i], 0, 0))
    bd_spec = pl.BlockSpec((None, 1, D), lambda i, te, nv: (te[i], 0, 0))
    x_spec = pl.BlockSpec((tm, D), lambda i, te, nv: (jnp.minimum(i, nv[0] - 1), 0))
    return pl.pallas_call(
        _experts_kernel,
        grid_spec=pltpu.PrefetchScalarGridSpec(
            num_scalar_prefetch=2,
            grid=(P // tm,),
            in_specs=[x_spec, pl.BlockSpec((1, D), lambda i, te, nv: (0, 0)),
                      w_spec, b_spec, w_spec, b_spec, wd_spec, bd_spec],
            out_specs=pl.BlockSpec((tm, D), lambda i, te, nv: (i, 0)),
            scratch_shapes=[pltpu.VMEM((D, F), BF16), pltpu.VMEM((D, F), BF16), pltpu.VMEM((F, D), BF16)]),
        out_shape=jax.ShapeDtypeStruct((P, D), F32),
        compiler_params=pltpu.CompilerParams(dimension_semantics=("arbitrary",), vmem_limit_bytes=VMEM_LIMIT),
        name="experts",
    )(tile_expert, n_valid, xs, g_mlp.reshape(1, D),
      w_gate, b_gate.reshape(E, 1, F), w_up, b_up.reshape(E, 1, F), w_down, b_down.reshape(E, 1, D))


def _combine_kernel(pos_ref, x_ref, wgt_ref, ys_ref, o_ref, buf_ref, sem):
    tm = x_ref.shape[0]

    def issue(j, carry):
        for u in range(ROWS_PER_ISSUE):
            r = j * ROWS_PER_ISSUE + u
            for k in range(TOP_K):
                _row_copy(ys_ref, pos_ref[r * TOP_K + k], buf_ref.at[k], r, sem).start(priority=k % 2)
        return carry

    lax.fori_loop(0, tm // ROWS_PER_ISSUE, issue, 0)
    for k in range(TOP_K):
        _rows_copy(ys_ref, buf_ref.at[k], tm, sem).wait()
    wgt = wgt_ref[...]
    acc = x_ref[...]
    for k in range(TOP_K):
        acc = acc + wgt[:, k:k + 1] * buf_ref[k]
    o_ref[...] = acc


def _combine(x1, wgt, pos_flat, ys):
    N, D = x1.shape
    tm = TM_COMBINE
    return pl.pallas_call(
        _combine_kernel,
        grid=(N // tm,),
        in_specs=[pl.BlockSpec((tm * TOP_K,), lambda i: (i,), memory_space=pltpu.SMEM),
                  pl.BlockSpec((tm, D), lambda i: (i, 0)),
                  pl.BlockSpec((tm, 128), lambda i: (i, 0)),
                  pl.BlockSpec(memory_space=pl.ANY)],
        out_specs=pl.BlockSpec((tm, D), lambda i: (i, 0)),
        out_shape=jax.ShapeDtypeStruct((N, D), F32),
        scratch_shapes=[pltpu.VMEM((TOP_K, tm, D), F32), pltpu.SemaphoreType.DMA],
        compiler_params=pltpu.CompilerParams(dimension_semantics=("arbitrary",), vmem_limit_bytes=VMEM_LIMIT),
        name="combine",
    )(pos_flat, x1, wgt, ys)


def _moe(x1, code, wgt, cnt, g_mlp, w_gate, b_gate, w_up, b_up, w_down, b_down):
    N, D = x1.shape
    tm = TM_EXPERT
    n_tiles = -(-(N * TOP_K + N_EXPERTS * (tm - 1)) // tm)
    counts = cnt[0, :N_EXPERTS]
    padded = ((counts + tm - 1) // tm) * tm
    ends = jnp.cumsum(padded)
    starts = ends - padded
    code = code[:, :TOP_K]
    pos = (starts[code // RANK_RADIX] + code % RANK_RADIX).reshape(-1).astype(I32)
    tile_start = jnp.arange(n_tiles, dtype=I32) * tm
    tile_expert = jnp.minimum(jnp.sum(ends[None, :] <= tile_start[:, None], axis=1), N_EXPERTS - 1).astype(I32)
    n_valid = (ends[-1] // tm).astype(I32).reshape(1)
    xs = _dispatch(x1, pos, (starts + counts).astype(I32), ends.astype(I32), n_tiles * tm)
    ys = _experts(xs, tile_expert, n_valid, g_mlp, w_gate, b_gate, w_up, b_up, w_down, b_down)
    return _combine(x1, wgt, pos, ys)


def _layer(x, g_attn, w_in, qn_nsa, kn_cmp, kn_sel, kn_win,
           pos_cmp_k, w_ck1, b_ck1, w_ck2, b_ck2, pos_cmp_v, w_cv1, b_cv1, w_cv2, b_cv2,
           b_nsa_gate, qn_fox, kn_fox, b_forget, w_up_nsa, w_up_fox, b_merge, w_out, g_mlp,
           w_router, b_router, w_e_gate, b_e_gate, w_e_up, b_e_up, w_e_down, b_e_down):
    B, T, D = x.shape
    qn, ks, vs, kw, vw, kc_raw, vc_raw, qf, kf, vf, gates = _in_proj(
        x, g_attn, w_in, qn_nsa, kn_sel, kn_win, b_nsa_gate, qn_fox, kn_fox, b_forget)
    kc, vc = _compress(kc_raw, vc_raw, pos_cmp_k, w_ck1, b_ck1, w_ck2, b_ck2,
                       pos_cmp_v, w_cv1, b_cv1, w_cv2, b_cv2, kn_cmp)
    o_nsa = _nsa(qn, ks, vs, kw, vw, kc, vc, gates)
    o_fox = _fox(qf, kf, vf)
    hw = NSA_HEADS * HEAD_PAD
    w_merge = w_in[:, w_in.shape[1] - 2 * D:]
    x1, code, wgt, cnt = _merge_out(x.reshape(B * T, D), g_attn, o_nsa.reshape(B * T, hw), o_fox.reshape(B * T, hw),
                                    w_merge, b_merge, w_up_nsa, w_up_fox, w_out, g_mlp, w_router, b_router)
    out = _moe(x1, code, wgt, cnt, g_mlp, w_e_gate, b_e_gate, w_e_up, b_e_up, w_e_down, b_e_down)
    return out.reshape(B, T, D)


def kernel(x, g_attn, w_in, qn_nsa, kn_cmp, kn_sel, kn_win, pos_cmp_k, w_ck1, b_ck1, w_ck2, b_ck2, pos_cmp_v, w_cv1, b_cv1, w_cv2, b_cv2, b_nsa_gate, qn_fox, kn_fox, b_forget, w_up_nsa, w_up_fox, b_merge, w_out, g_mlp, w_router, b_router, w_e_gate, b_e_gate, w_e_up, b_e_up, w_e_down, b_e_down):
    params = (g_attn, w_in, qn_nsa, kn_cmp, kn_sel, kn_win, pos_cmp_k, w_ck1, b_ck1, w_ck2, b_ck2,
              pos_cmp_v, w_cv1, b_cv1, w_cv2, b_cv2, b_nsa_gate, qn_fox, kn_fox, b_forget,
              w_up_nsa, w_up_fox, b_merge, w_out, g_mlp, w_router, b_router,
              w_e_gate, b_e_gate, w_e_up, b_e_up, w_e_down, b_e_down)
    for layer in range(g_attn.shape[0]):
        x = _layer(x, *[p[layer] for p in params])
    return x
```

```python
import functools

import jax
import jax.numpy as jnp
from jax import lax
from jax.experimental import pallas as pl
from jax.experimental.pallas import tpu as pltpu

F32 = jnp.float32
BF16 = jnp.bfloat16
I32 = jnp.int32

D_MODEL = 1024
HEAD_DIM = 64
HEAD_PAD = 128
NSA_HEADS = 8
NSA_GROUPS = 2
NSA_HPG = NSA_HEADS // NSA_GROUPS
FOX_HEADS = 8
CMP_BLOCK = 32
CMP_STRIDE = 16
CMP_HIDDEN = 256
N_CMP_PAD = 128
SEL_BLOCK = 64
SEL_TOPN = 16
N_SEL_BLOCKS = 32
WINDOW = 512
N_EXPERTS = 32
TOP_K = 4
SWIGLU_LIMIT = 7.0
SWIGLU_ALPHA = 1.702
RMS_EPS = 1e-6
NEG_BIG = -1e30
FORCED_SCORE = 1e9
QK_SCALE = HEAD_DIM ** -0.5

LANE_SEL = 64
LANE_POS = 96
LANE_FOX = 64

OFF_QN = 0
OFF_KS = OFF_QN + NSA_HEADS * HEAD_PAD
OFF_VS = OFF_KS + NSA_GROUPS * HEAD_PAD
OFF_KW = OFF_VS + NSA_GROUPS * HEAD_PAD
OFF_VW = OFF_KW + NSA_GROUPS * HEAD_PAD
OFF_KC = OFF_VW + NSA_GROUPS * HEAD_PAD
OFF_VC = OFF_KC + NSA_GROUPS * HEAD_DIM
OFF_QF = OFF_VC + NSA_GROUPS * HEAD_DIM
OFF_KF = OFF_QF + FOX_HEADS * HEAD_DIM
OFF_VF = OFF_KF + FOX_HEADS * HEAD_DIM
OFF_MISC = OFF_VF + FOX_HEADS * HEAD_DIM
W1_COLS = OFF_MISC + 128
MISC_GATES = 3 * NSA_HEADS

TM_PROJ = 256
TQ_NSA = 128
TQ_SELECT = 512
TILES_PER_REGION = 4
SELECTS_PER_REGION = 2
CK_ATTN = 256
TM_OUT = 512
TM_EXPERT = 512
TM_DISPATCH = 2048
TM_COMBINE = 1024
ROWS_PER_ISSUE = 8
RANK_RADIX = 1 << 16
VMEM_LIMIT = 56 * 1024 * 1024


def _dot(a, b):
    return jnp.dot(a, b, preferred_element_type=F32)


def _dot_nt(a, b):
    return lax.dot_general(a, b, (((1,), (1,)), ((), ())), preferred_element_type=F32)


def _split3(v):
    hi = v.astype(BF16)
    r1 = v - hi.astype(F32)
    mid = r1.astype(BF16)
    lo = (r1 - mid.astype(F32)).astype(BF16)
    return hi, mid, lo


def _dot_exact_rhs(a_f32, b_bf16):
    hi, mid, lo = _split3(a_f32)
    return _dot(hi, b_bf16) + _dot(mid, b_bf16) + _dot(lo, b_bf16)


def _dot_exact_lhs(a_bf16, b_f32):
    hi, mid, lo = _split3(b_f32)
    return _dot(a_bf16, hi) + _dot(a_bf16, mid) + _dot(a_bf16, lo)


def _rms(x):
    return x * lax.rsqrt(jnp.mean(x * x, axis=-1, keepdims=True) + RMS_EPS)


def _head_rms(a, gain):
    ms = jnp.sum(a * a, axis=-1, keepdims=True) * (1.0 / HEAD_DIM)
    return a * lax.rsqrt(ms + RMS_EPS) * gain


def _pos_feat_k(pos, lane):
    hi = (256 * (pos >> 8)).astype(F32)
    lo = (pos & 255).astype(F32)
    return jnp.where(lane < LANE_POS + 2, 1.0, jnp.where(lane == LANE_POS + 2, hi, lo)) * (
        (lane >= LANE_POS) & (lane < LANE_POS + 4)).astype(F32)


def _in_proj_kernel(x_ref, g_ref, w_ref, gains_ref, bmisc_ref, pq_ref, pk_ref,
                    qn_ref, ks_ref, vs_ref, kw_ref, vw_ref, kc_ref, vc_ref,
                    qf_ref, kf_ref, vf_ref, gate_ref, carry_ref):
    tm = x_ref.shape[0]
    i = pl.program_id(1)
    h = (_rms(x_ref[...]) * g_ref[...]).astype(BF16)
    pos = i * tm + lax.broadcasted_iota(I32, (tm, 1), 0)
    lane = lax.broadcasted_iota(I32, (1, HEAD_PAD), 1)
    pos_hi = (256 * (pos >> 8)).astype(F32)
    pos_lo = (pos & 255).astype(F32)
    kfeat = _pos_feat_k(pos, lane)
    blk_onehot = ((lane - LANE_SEL) == (pos >> 6)).astype(F32)

    def seg(off, width):
        return _dot(h, w_ref[:, off:off + width])

    a = seg(OFF_QN, NSA_HEADS * HEAD_PAD)
    gq = gains_ref[0:1, :] * QK_SCALE
    for hd in range(NSA_HEADS):
        slope = 2.0 ** (-(hd + 1))
        qfeat = jnp.where(lane == LANE_POS, -slope * pos_hi,
                          jnp.where(lane == LANE_POS + 1, -slope * pos_lo,
                                    jnp.where((lane == LANE_POS + 2) | (lane == LANE_POS + 3), slope, 0.0)))
        sl = slice(hd * HEAD_PAD, (hd + 1) * HEAD_PAD)
        qn_ref[:, sl] = (_head_rms(a[:, sl], gq) + qfeat).astype(BF16)

    a = seg(OFF_KS, NSA_GROUPS * HEAD_PAD)
    for g in range(NSA_GROUPS):
        sl = slice(g * HEAD_PAD, (g + 1) * HEAD_PAD)
        ks_ref[:, sl] = (_head_rms(a[:, sl], gains_ref[1:2, :]) + kfeat + blk_onehot).astype(BF16)
    vs_ref[...] = seg(OFF_VS, NSA_GROUPS * HEAD_PAD).astype(BF16)
    a = seg(OFF_KW, NSA_GROUPS * HEAD_PAD)
    for g in range(NSA_GROUPS):
        sl = slice(g * HEAD_PAD, (g + 1) * HEAD_PAD)
        kw_ref[:, sl] = (_head_rms(a[:, sl], gains_ref[2:3, :]) + kfeat).astype(BF16)
    vw_ref[...] = seg(OFF_VW, NSA_GROUPS * HEAD_PAD).astype(BF16)
    kc_ref[...] = seg(OFF_KC, NSA_GROUPS * HEAD_DIM).astype(BF16)
    vc_ref[...] = seg(OFF_VC, NSA_GROUPS * HEAD_DIM).astype(BF16)

    misc = seg(OFF_MISC, 128) + bmisc_ref[...]
    is_gate = lane < MISC_GATES
    gate_ref[...] = jnp.where(is_gate, jax.nn.sigmoid(misc), 0.0)
    is_f = (lane >= MISC_GATES) & (lane < MISC_GATES + 3 * FOX_HEADS)
    logf = jnp.where(is_f, jax.nn.log_sigmoid(misc), 0.0)

    @pl.when(i == 0)
    def _():
        carry_ref[...] = jnp.zeros_like(carry_ref)

    r = lax.broadcasted_iota(I32, (tm, tm), 0)
    c = lax.broadcasted_iota(I32, (tm, tm), 1)
    tri = (r >= c).astype(BF16)
    csum = _dot_exact_lhs(tri, logf) + carry_ref[...]
    carry_ref[...] = csum[tm - 1:tm, :]
    c_hi, c_mid, c_lo = _split3(csum)
    c3 = jnp.where(lane < MISC_GATES + FOX_HEADS, c_hi,
                   jnp.where(lane < MISC_GATES + 2 * FOX_HEADS, c_mid, c_lo))
    c3 = jnp.where(is_f, c3, jnp.zeros_like(c3))
    low = lane < HEAD_DIM

    def fox_heads(off, gain2, feat, ones_at, out_ref):
        a = seg(off, FOX_HEADS * HEAD_DIM)
        for j in range(FOX_HEADS // 2):
            pair = a[:, j * HEAD_PAD:(j + 1) * HEAD_PAD]
            sq = pair * pair
            ms_even = jnp.sum(jnp.where(low, sq, 0.0), axis=-1, keepdims=True) * (1.0 / HEAD_DIM)
            ms_odd = jnp.sum(jnp.where(low, 0.0, sq), axis=-1, keepdims=True) * (1.0 / HEAD_DIM)
            normed = pair * jnp.where(low, lax.rsqrt(ms_even + RMS_EPS), lax.rsqrt(ms_odd + RMS_EPS)) * gain2
            for parity in range(2):
                hd = 2 * j + parity
                base = _fox_feature_base(hd)
                ones = ((lane >= base + ones_at) & (lane < base + ones_at + 3)).astype(F32)
                keep = low if parity == 0 else jnp.logical_not(low)
                slab = jnp.where(keep, normed, 0.0) + feat[:, hd * HEAD_PAD:(hd + 1) * HEAD_PAD] + ones
                out_ref[:, hd * HEAD_PAD:(hd + 1) * HEAD_PAD] = slab.astype(BF16)

    fox_heads(OFF_QF, gains_ref[5:6, :] * QK_SCALE, _dot(c3, pq_ref[...]), 3, qf_ref)
    fox_heads(OFF_KF, gains_ref[6:7, :], _dot(c3, pk_ref[...]), 0, kf_ref)
    vf_ref[...] = seg(OFF_VF, FOX_HEADS * HEAD_DIM).astype(BF16)


def _pad_heads(w, n_heads):
    lead = w.shape[:-1]
    w = w.reshape(lead + (n_heads, HEAD_DIM))
    w = jnp.pad(w, [(0, 0)] * len(lead) + [(0, 0), (0, HEAD_PAD - HEAD_DIM)])
    return w.reshape(lead + (n_heads * HEAD_PAD,))


def _pad_gain(g):
    return jnp.pad(g.astype(F32), (0, HEAD_PAD - HEAD_DIM))


def _fox_feature_base(hd):
    return HEAD_DIM if hd % 2 == 0 else 0


def _fox_placement():
    pq = jnp.zeros((128, FOX_HEADS * HEAD_PAD), F32)
    pk = jnp.zeros((128, FOX_HEADS * HEAD_PAD), F32)
    for piece in range(3):
        for hd in range(FOX_HEADS):
            src = MISC_GATES + piece * FOX_HEADS + hd
            base = hd * HEAD_PAD + _fox_feature_base(hd)
            pq = pq.at[src, base + piece].set(1.0)
            pk = pk.at[src, base + 3 + piece].set(-1.0)
    return pq.astype(BF16), pk.astype(BF16)


def _in_proj(x, g_attn, w_in, qn_nsa, kn_sel, kn_win, b_nsa_gate, qn_fox, kn_fox, b_forget):
    B, T, D = x.shape
    s1 = NSA_HEADS * HEAD_DIM
    kv_w = NSA_GROUPS * HEAD_DIM
    s2 = s1 + 6 * kv_w
    s3 = s2 + 3 * NSA_HEADS
    fw = FOX_HEADS * HEAD_DIM
    s4 = s3 + 3 * fw
    s5 = s4 + FOX_HEADS
    kv = [w_in[:, s1 + j * kv_w: s1 + (j + 1) * kv_w] for j in range(6)]
    f_cols = w_in[:, s4:s5]
    misc = jnp.concatenate([w_in[:, s2:s3], f_cols, f_cols, f_cols,
                            jnp.zeros((D, 128 - MISC_GATES - 3 * FOX_HEADS), F32)], axis=1)
    w1 = jnp.concatenate([
        _pad_heads(w_in[:, :s1], NSA_HEADS),
        _pad_heads(kv[2], NSA_GROUPS), _pad_heads(kv[3], NSA_GROUPS),
        _pad_heads(kv[4], NSA_GROUPS), _pad_heads(kv[5], NSA_GROUPS),
        kv[0], kv[1],
        w_in[:, s3:s4],
        misc], axis=1).astype(BF16)
    assert w1.shape[1] == W1_COLS
    gains = jnp.stack([_pad_gain(qn_nsa), _pad_gain(kn_sel), _pad_gain(kn_win),
                       _pad_gain(qn_fox), _pad_gain(kn_fox),
                       jnp.tile(qn_fox.astype(F32), 2), jnp.tile(kn_fox.astype(F32), 2), jnp.zeros(HEAD_PAD, F32)])
    bmisc = jnp.concatenate([b_nsa_gate, b_forget, b_forget, b_forget,
                             jnp.zeros(128 - MISC_GATES - 3 * FOX_HEADS, F32)]).reshape(1, 128)
    pq, pk = _fox_placement()
    tm = TM_PROJ

    def tok(width):
        return pl.BlockSpec((None, tm, width), lambda b, i: (b, i, 0))

    def full(shape):
        return pl.BlockSpec(shape, lambda b, i: (0,) * len(shape))

    widths = [NSA_HEADS * HEAD_PAD] + [NSA_GROUPS * HEAD_PAD] * 4 + [NSA_GROUPS * HEAD_DIM] * 2 + \
             [FOX_HEADS * HEAD_PAD] * 2 + [FOX_HEADS * HEAD_DIM]
    out_shape = [jax.ShapeDtypeStruct((B, T, w), BF16) for w in widths] + \
                [jax.ShapeDtypeStruct((B, T, 128), F32)]
    return pl.pallas_call(
        _in_proj_kernel,
        grid=(B, T // tm),
        in_specs=[tok(D), full((1, D)), full((D, W1_COLS)), full((8, HEAD_PAD)), full((1, 128)),
                  full((128, FOX_HEADS * HEAD_PAD)), full((128, FOX_HEADS * HEAD_PAD))],
        out_specs=[tok(w) for w in widths] + [tok(128)],
        out_shape=out_shape,
        scratch_shapes=[pltpu.VMEM((1, 128), F32)],
        compiler_params=pltpu.CompilerParams(
            dimension_semantics=("arbitrary", "arbitrary"), vmem_limit_bytes=VMEM_LIMIT),
        name="in_proj",
    )(x, g_attn.reshape(1, D), w1, gains, bmisc, pq, pk)


def _gelu_tanh(x):
    return 0.5 * x * (1.0 + jnp.tanh(0.7978845608028654 * (x + 0.044715 * (x * x * x))))


def _compress_kernel(ak_ref, av_ref, posk_ref, posv_ref, wk1_ref, bk1_ref, wk2_ref, bk2_ref,
                     wv1_ref, bv1_ref, wv2_ref, bv2_ref, gain_ref, kc_ref, vc_ref):
    row = lax.broadcasted_iota(I32, (N_CMP_PAD, 1), 0)
    lane = lax.broadcasted_iota(I32, (1, HEAD_PAD), 1)
    kfeat = _pos_feat_k(CMP_STRIDE * row + CMP_BLOCK - 1, lane)

    def mlp(a_ref, pos_ref, w1_ref, b1_ref, w2_ref, b2_ref):
        a = a_ref[...].astype(F32)
        lo = (a + pos_ref[0:1, :]).astype(BF16)
        hi = (a + pos_ref[1:2, :]).astype(BF16)
        outs = []
        for g in range(NSA_GROUPS):
            hid = _dot(lo, w1_ref[0, g]) + pltpu.roll(_dot(hi, w1_ref[1, g]), N_CMP_PAD - 1, 0) + b1_ref[...]
            outs.append(_dot(_gelu_tanh(hid).astype(BF16), w2_ref[...]) + b2_ref[...])
        return outs

    ks = mlp(ak_ref, posk_ref, wk1_ref, bk1_ref, wk2_ref, bk2_ref)
    vs = mlp(av_ref, posv_ref, wv1_ref, bv1_ref, wv2_ref, bv2_ref)
    for g in range(NSA_GROUPS):
        kc_ref[g] = (_head_rms(ks[g], gain_ref[...]) + kfeat).astype(BF16)
        vc_ref[g] = vs[g].astype(BF16)


def _compress(kc_raw, vc_raw, pos_k, w_ck1, b_ck1, w_ck2, b_ck2, pos_v, w_cv1, b_cv1, w_cv2, b_cv2, kn_cmp):
    B, T, _ = kc_raw.shape
    n_chunks = T // CMP_STRIDE
    assert n_chunks == N_CMP_PAD
    chunk_w = CMP_STRIDE * NSA_GROUPS * HEAD_DIM

    def chunks(a):
        return a.reshape(B, n_chunks, chunk_w)

    def w1_rows(w):
        w = w.reshape(2, CMP_STRIDE, 1, HEAD_DIM, CMP_HIDDEN)
        per_group = [jnp.concatenate([w if gg == g else jnp.zeros_like(w) for gg in range(NSA_GROUPS)], axis=2)
                     for g in range(NSA_GROUPS)]
        return jnp.stack(per_group, axis=1).reshape(2, NSA_GROUPS, chunk_w, CMP_HIDDEN).astype(BF16)

    def pos_rows(p):
        p = jnp.broadcast_to(p.reshape(2, CMP_STRIDE, 1, HEAD_DIM), (2, CMP_STRIDE, NSA_GROUPS, HEAD_DIM))
        return p.reshape(2, chunk_w)

    def w2pad(w, b):
        return (jnp.pad(w, ((0, 0), (0, HEAD_PAD - HEAD_DIM))).astype(BF16),
                jnp.pad(b, (0, HEAD_PAD - HEAD_DIM)).reshape(1, HEAD_PAD))

    wk2, bk2 = w2pad(w_ck2, b_ck2)
    wv2, bv2 = w2pad(w_cv2, b_cv2)
    a_spec = pl.BlockSpec((None, n_chunks, chunk_w), lambda b: (b, 0, 0))
    o_spec = pl.BlockSpec((None, NSA_GROUPS, N_CMP_PAD, HEAD_PAD), lambda b: (b, 0, 0, 0))

    def full(shape):
        return pl.BlockSpec(shape, lambda b: (0,) * len(shape))

    w1_shape = (2, NSA_GROUPS, chunk_w, CMP_HIDDEN)
    return pl.pallas_call(
        _compress_kernel,
        grid=(B,),
        in_specs=[a_spec, a_spec, full((2, chunk_w)), full((2, chunk_w)),
                  full(w1_shape), full((1, CMP_HIDDEN)), full((CMP_HIDDEN, HEAD_PAD)), full((1, HEAD_PAD)),
                  full(w1_shape), full((1, CMP_HIDDEN)), full((CMP_HIDDEN, HEAD_PAD)), full((1, HEAD_PAD)),
                  full((1, HEAD_PAD))],
        out_specs=[o_spec, o_spec],
        out_shape=[jax.ShapeDtypeStruct((B, NSA_GROUPS, N_CMP_PAD, HEAD_PAD), BF16)] * 2,
        compiler_params=pltpu.CompilerParams(dimension_semantics=("arbitrary",), vmem_limit_bytes=VMEM_LIMIT),
        name="compress",
    )(chunks(kc_raw), chunks(vc_raw), pos_rows(pos_k), pos_rows(pos_v),
      w1_rows(w_ck1), b_ck1.reshape(1, -1), wk2, bk2,
      w1_rows(w_cv1), b_cv1.reshape(1, -1), wv2, bv2, _pad_gain(kn_cmp).reshape(1, HEAD_PAD))


def _lane_blocks(a):
    return [a[:, i * 128:(i + 1) * 128] for i in range(a.shape[1] // 128)]


def _fold_blocks(op, acc, a):
    for blk in _lane_blocks(a):
        acc = blk if acc is None else op(acc, blk)
    return acc


def _nsa_kernel(q_ref, ks_ref, vs_ref, kw_ref, vw_ref, kc_ref, vct_ref, gate_ref, ovl_ref, o_ref,
                sel_ref, oc_ref, s_ref):
    seq = q_ref.shape[0]
    tq = TQ_NSA
    ta = TQ_SELECT
    ck = CK_ATTN
    span = WINDOW + tq
    g = pl.program_id(1)
    col_l = lax.broadcasted_iota(I32, (1, 128), 1)
    c_end_sub = CMP_STRIDE * lax.broadcasted_iota(I32, (N_CMP_PAD, 1), 0) + (CMP_BLOCK - 1)
    blk_id = lax.broadcasted_iota(I32, (N_SEL_BLOCKS, 1), 0)
    sub_id = lax.broadcasted_iota(I32, (8, 1), 0)

    def gate_col(gates, branch, hh):
        lo = branch * NSA_HEADS + hh
        return jnp.where(g == 0, gates[:, lo:lo + 1], gates[:, lo + NSA_HPG:lo + NSA_HPG + 1])

    def select(a):
        a0 = a * ta
        t_lane = a0 + lax.broadcasted_iota(I32, (1, ta), 1)
        gates = gate_ref[a0:a0 + ta, :]

        mask_t = t_lane >= c_end_sub
        p_sum = None
        for hh in range(NSA_HPG):
            q_h = q_ref[a0:a0 + ta, hh * HEAD_PAD:(hh + 1) * HEAD_PAD]
            s_t = jnp.where(mask_t, _dot_nt(kc_ref[...], q_h), NEG_BIG)
            e_t = jnp.where(mask_t, jnp.exp(s_t - jnp.max(s_t, axis=0, keepdims=True)), 0.0)
            l_t = jnp.sum(e_t, axis=0, keepdims=True)
            p_t = e_t / jnp.where(l_t > 0.0, l_t, 1.0)
            o_t = _dot(vct_ref[...], p_t.astype(BF16))
            g_c = gate_col(gates, 0, hh)
            for blk in range(ta // 128):
                bs = slice(blk * 128, (blk + 1) * 128)
                oc_ref[a0 + blk * 128:a0 + (blk + 1) * 128, hh * HEAD_PAD:(hh + 1) * HEAD_PAD] = g_c[bs] * o_t[:, bs].T
            p_sum = p_t if p_sum is None else p_sum + p_t
        imp = _dot_exact_lhs(ovl_ref[...], p_sum)
        tb = t_lane >> 6
        valid = blk_id <= tb
        forced = (blk_id == 0) | (blk_id == tb) | (blk_id == tb - 1)
        score = jnp.where(forced & valid, FORCED_SCORE, imp)
        score = jnp.where(valid, score, -1.0)
        groups = [score[8 * b:8 * b + 8, :] for b in range(N_SEL_BLOCKS // 8)]
        n_above = [jnp.zeros((8, ta), F32) for _ in groups]
        for j in range(N_SEL_BLOCKS):
            cj = score[j:j + 1, :]
            for b, grp in enumerate(groups):
                if 8 * b > j:
                    n_above[b] = n_above[b] + jnp.where(cj >= grp, 1.0, 0.0)
                elif 8 * b + 7 < j:
                    n_above[b] = n_above[b] + jnp.where(cj > grp, 1.0, 0.0)
                else:
                    tie = jnp.where(sub_id + 8 * b > j, 1.0, 0.0)
                    n_above[b] = n_above[b] + jnp.where(cj > grp, 1.0, 0.0) + jnp.where(cj == grp, tie, 0.0)
        dropped = jnp.where(jnp.concatenate(n_above, axis=0) >= float(SEL_TOPN), NEG_BIG, 0.0)
        drop_t = jnp.concatenate([jnp.zeros((LANE_SEL, ta), F32), dropped,
                                  jnp.zeros((HEAD_PAD - LANE_SEL - N_SEL_BLOCKS, ta), F32)], axis=0)
        for blk in range(ta // 128):
            sel_ref[a0 + blk * 128:a0 + (blk + 1) * 128, :] = drop_t[:, blk * 128:(blk + 1) * 128].T.astype(BF16)

    for first in range(0, seq // ta, SELECTS_PER_REGION):
        @pl.when(g >= -first)
        def _(first=first):
            for a in range(first, first + SELECTS_PER_REGION):
                select(a)

    rows = NSA_HPG * tq
    row_l = lax.broadcasted_iota(I32, (rows, 1), 0) & (tq - 1)

    def attend(i):
        t0 = i * tq
        t_row = t0 + row_l
        q4 = jnp.concatenate([q_ref[t0:t0 + tq, hh * HEAD_PAD:(hh + 1) * HEAD_PAD] for hh in range(NSA_HPG)], axis=0)
        q4s = q4 + jnp.concatenate([sel_ref[t0:t0 + tq, :]] * NSA_HPG, axis=0)

        n_chunks = (t0 + tq - 1) // ck + 1
        slot = i % 2
        mx = None
        for c in range(n_chunks):
            s = _dot_nt(q4s, ks_ref[c * ck:(c + 1) * ck, :])
            if (c + 1) * ck > t0:
                s = jnp.where(t_row >= c * ck + lax.broadcasted_iota(I32, (1, ck), 1), s, NEG_BIG)
            s_ref[slot, :, c * ck:(c + 1) * ck] = s
            mx = _fold_blocks(jnp.maximum, mx, s)
        m = jnp.max(mx, axis=-1, keepdims=True)
        ls = None
        acc = None
        for c in range(n_chunks):
            p = jnp.exp(s_ref[slot, :, c * ck:(c + 1) * ck] - m)
            ls = _fold_blocks(jnp.add, ls, p)
            pv = _dot(p.astype(BF16), vs_ref[c * ck:(c + 1) * ck, :])
            acc = pv if acc is None else acc + pv
        o_s = acc / jnp.sum(ls, axis=-1, keepdims=True)

        ws = max(t0 - WINDOW, 0)
        blocks = _lane_blocks(_dot_nt(q4, kw_ref[ws:ws + span, :]))
        if t0 < WINDOW:
            blocks = [jnp.where(t_row >= ws + jb * 128 + col_l, blk, NEG_BIG) if ws + (jb + 1) * 128 > t0 else blk
                      for jb, blk in enumerate(blocks)]
        else:
            blocks[0] = jnp.where(col_l > row_l, blocks[0], NEG_BIG)
            blocks[-1] = jnp.where(col_l <= row_l, blocks[-1], NEG_BIG)
        s = jnp.concatenate(blocks, axis=1)
        m = jnp.max(_fold_blocks(jnp.maximum, None, s), axis=-1, keepdims=True)
        p = jnp.exp(s - m)
        l = jnp.sum(_fold_blocks(jnp.add, None, p), axis=-1, keepdims=True)
        o_w = _dot(p.astype(BF16), vw_ref[ws:ws + span, :]) / l

        gates = gate_ref[t0:t0 + tq, :]
        outs = []
        for hh in range(NSA_HPG):
            rs = slice(hh * tq, (hh + 1) * tq)
            o = oc_ref[t0:t0 + tq, hh * HEAD_PAD:(hh + 1) * HEAD_PAD]
            outs.append((o + gate_col(gates, 1, hh) * o_s[rs] + gate_col(gates, 2, hh) * o_w[rs]).astype(BF16))
        o_ref[t0:t0 + tq, :] = jnp.concatenate(outs, axis=1)

    for first in range(0, seq // tq, TILES_PER_REGION):
        @pl.when(g > -1 - first)
        def _(first=first):
            for i in range(first, first + TILES_PER_REGION):
                attend(i)


def _overlap_matrix():
    c = jnp.arange(N_CMP_PAD)[None, :]
    j = jnp.arange(N_SEL_BLOCKS)[:, None]
    lo = (CMP_STRIDE * c) // SEL_BLOCK
    hi = (CMP_STRIDE * c + CMP_BLOCK - 1) // SEL_BLOCK
    return ((j == lo) | (j == hi)).astype(BF16)


def _nsa(qn, ks, vs, kw, vw, kc, vc, gates):
    B, T, _ = qn.shape
    gw = NSA_HPG * HEAD_PAD
    q_spec = pl.BlockSpec((None, T, gw), lambda b, g: (b, 0, g))
    kv_spec = pl.BlockSpec((None, T, HEAD_PAD), lambda b, g: (b, 0, g))
    c_spec = pl.BlockSpec((None, None, N_CMP_PAD, HEAD_PAD), lambda b, g: (b, g, 0, 0))
    return pl.pallas_call(
        _nsa_kernel,
        grid=(B, NSA_GROUPS),
        in_specs=[q_spec, kv_spec, kv_spec, kv_spec, kv_spec, c_spec, c_spec,
                  pl.BlockSpec((None, T, 128), lambda b, g: (b, 0, 0)),
                  pl.BlockSpec((N_SEL_BLOCKS, N_CMP_PAD), lambda b, g: (0, 0))],
        out_specs=q_spec,
        out_shape=jax.ShapeDtypeStruct((B, T, NSA_HEADS * HEAD_PAD), BF16),
        scratch_shapes=[pltpu.VMEM((T, HEAD_PAD), BF16), pltpu.VMEM((T, gw), F32),
                        pltpu.VMEM((2, NSA_HPG * TQ_NSA, T), F32)],
        compiler_params=pltpu.CompilerParams(
            dimension_semantics=("arbitrary", "arbitrary"), vmem_limit_bytes=VMEM_LIMIT),
        name="nsa",
    )(qn, ks, vs, kw, vw, kc, vc.swapaxes(2, 3), gates, _overlap_matrix())


def _fox_kernel(q_ref, k_ref, v_ref, o_ref):
    seq = q_ref.shape[0]
    ck = CK_ATTN
    row = lax.broadcasted_iota(I32, (ck, 1), 0)
    col = lax.broadcasted_iota(I32, (1, ck), 1)
    for qi in range(seq // ck):
        q = q_ref[qi * ck:(qi + 1) * ck, :]
        s_chunks = []
        mx = None
        for c in range(qi + 1):
            s = _dot_nt(q, k_ref[c * ck:(c + 1) * ck, :])
            if c == qi:
                s = jnp.where(row >= col, s, NEG_BIG)
            s_chunks.append(s)
            mx = _fold_blocks(jnp.maximum, mx, s)
        m = jnp.max(mx, axis=-1, keepdims=True)
        ls = None
        acc = None
        for c in range(qi + 1):
            p = jnp.exp(s_chunks[c] - m)
            ls = _fold_blocks(jnp.add, ls, p)
            pv = _dot(p.astype(BF16), v_ref[c * ck:(c + 1) * ck, :])
            acc = pv if acc is None else acc + pv
        o_ref[qi * ck:(qi + 1) * ck, :] = (acc / jnp.sum(ls, axis=-1, keepdims=True)).astype(BF16)


def _fox(qf, kf, vf):
    B, T, _ = qf.shape
    spec = pl.BlockSpec((None, T, HEAD_PAD), lambda b, h: (b, 0, h))
    v_spec = pl.BlockSpec((None, T, HEAD_PAD), lambda b, h: (b, 0, h // 2))
    return pl.pallas_call(
        _fox_kernel,
        grid=(B, FOX_HEADS),
        in_specs=[spec, spec, v_spec],
        out_specs=spec,
        out_shape=jax.ShapeDtypeStruct((B, T, FOX_HEADS * HEAD_PAD), BF16),
        compiler_params=pltpu.CompilerParams(
            dimension_semantics=("arbitrary", "arbitrary"), vmem_limit_bytes=VMEM_LIMIT),
        name="fox",
    )(qf, kf, vf)


def _merge_out_kernel(x_ref, g_ref, on_ref, of_ref, wm_ref, bm_ref, wun_ref, wuf_ref, wo_ref,
                      gm_ref, wrh_ref, wrm_ref, br_ref, o_ref, code_ref, wgt_ref, cnt_ref, carry_ref):
    x = x_ref[...]
    h = (_rms(x) * g_ref[...]).astype(BF16)
    merge = jax.nn.sigmoid(_dot(h, wm_ref[...]) + bm_ref[...])
    y = merge[:, :D_MODEL] * _dot(on_ref[...], wun_ref[...]) + merge[:, D_MODEL:] * _dot(of_ref[...], wuf_ref[...])
    x1 = x + _dot(y.astype(BF16), wo_ref[...])
    o_ref[...] = x1
    _route_tile(x1, gm_ref, wrh_ref, wrm_ref, br_ref, code_ref, wgt_ref, cnt_ref, carry_ref)


def _merge_out(x2, g_attn, o_nsa, o_fox, w_merge, b_merge, w_up_nsa, w_up_fox, w_out, g_mlp, w_router, b_router):
    N, D = x2.shape
    tm = TM_OUT
    assert N <= RANK_RADIX
    wr = jnp.pad(w_router, ((0, 0), (0, 128 - N_EXPERTS)))
    wr_hi = wr.astype(BF16)
    wr_mid = (wr - wr_hi.astype(F32)).astype(BF16)
    br = jnp.pad(b_router, (0, 128 - N_EXPERTS)).reshape(1, 128)

    def pad_rows(w, n_heads):
        return _pad_heads(w.T, n_heads).T.astype(BF16)

    def fox_rows(w):
        w = w.reshape(FOX_HEADS, HEAD_DIM, D)
        z = jnp.zeros_like(w)
        even = jnp.concatenate([w, z], axis=1)
        odd = jnp.concatenate([z, w], axis=1)
        parity = (jnp.arange(FOX_HEADS) % 2 == 0)[:, None, None]
        return jnp.where(parity, even, odd).reshape(FOX_HEADS * HEAD_PAD, D).astype(BF16)

    def tok(width):
        return pl.BlockSpec((tm, width), lambda i: (i, 0))

    def full(shape):
        return pl.BlockSpec(shape, lambda i: (0,) * len(shape))

    hw = NSA_HEADS * HEAD_PAD
    return pl.pallas_call(
        _merge_out_kernel,
        grid=(N // tm,),
        in_specs=[tok(D), full((1, D)), tok(hw), tok(hw), full((D, 2 * D)), full((1, 2 * D)),
                  full((hw, D)), full((hw, D)), full((D, D)),
                  full((1, D)), full((D, 128)), full((D, 128)), full((1, 128))],
        out_specs=[tok(D), pl.BlockSpec((8, tm), lambda i: (0, i)), tok(128), full((1, 128))],
        out_shape=[jax.ShapeDtypeStruct((N, D), F32), jax.ShapeDtypeStruct((8, N), I32),
                   jax.ShapeDtypeStruct((N, 128), F32), jax.ShapeDtypeStruct((1, 128), I32)],
        scratch_shapes=[pltpu.VMEM((1, 128), F32)],
        compiler_params=pltpu.CompilerParams(dimension_semantics=("arbitrary",), vmem_limit_bytes=VMEM_LIMIT),
        name="merge_out",
    )(x2, g_attn.reshape(1, D), o_nsa, o_fox, w_merge.astype(BF16), b_merge.reshape(1, 2 * D),
      pad_rows(w_up_nsa, NSA_HEADS), fox_rows(w_up_fox), w_out.astype(BF16),
      g_mlp.reshape(1, D), wr_hi, wr_mid, br)


def _route_tile(x, g_ref, wh_ref, wm_ref, br_ref, code_ref, wgt_ref, cnt_ref, carry_ref):
    tm = x.shape[0]

    @pl.when(pl.program_id(0) == 0)
    def _():
        carry_ref[...] = jnp.zeros_like(carry_ref)

    h = _rms(x) * g_ref[...]
    h_hi = h.astype(BF16)
    h_mid = (h - h_hi.astype(F32)).astype(BF16)
    logits = _dot(h_hi, wh_ref[...]) + _dot(h_hi, wm_ref[...]) + _dot(h_mid, wh_ref[...]) + br_ref[...]
    lane = lax.broadcasted_iota(I32, (1, 128), 1)
    lane_f = lane.astype(F32)
    work = jnp.where(lane < N_EXPERTS, logits, -jnp.inf)
    vals, hots = [], []
    idx_out = jnp.zeros((tm, 128), F32)
    for k in range(TOP_K):
        m = jnp.max(work, axis=-1, keepdims=True)
        idx = jnp.min(jnp.where(work == m, lane_f, 128.0), axis=-1, keepdims=True)
        hot = lane_f == idx
        work = jnp.where(hot, -jnp.inf, work)
        vals.append(m)
        hots.append(hot)
        idx_out = idx_out + jnp.where(lane == k, idx, 0.0)
    e = [jnp.exp(v - vals[0]) for v in vals]
    denom = e[0] + e[1] + e[2] + e[3]
    wgt = jnp.zeros((tm, 128), F32)
    for k in range(TOP_K):
        wgt = wgt + jnp.where(lane == k, e[k] / denom, 0.0)

    multi = (hots[0] | hots[1] | hots[2] | hots[3]).astype(F32)
    r = lax.broadcasted_iota(I32, (tm, tm), 0)
    c = lax.broadcasted_iota(I32, (tm, tm), 1)
    before = _dot((r > c).astype(BF16), multi.astype(BF16)) + carry_ref[...]
    rank = jnp.zeros((tm, 128), F32)
    for k in range(TOP_K):
        rk = jnp.sum(jnp.where(hots[k], before, 0.0), axis=-1, keepdims=True)
        rank = rank + jnp.where(lane == k, rk, 0.0)
    total = carry_ref[...] + jnp.sum(multi, axis=0, keepdims=True)
    carry_ref[...] = total
    code = idx_out * float(RANK_RADIX) + rank
    for blk in range(tm // 128):
        code_ref[:, blk * 128:(blk + 1) * 128] = code[blk * 128:(blk + 1) * 128, :].T[0:8, :].astype(I32)
    wgt_ref[...] = wgt
    cnt_ref[...] = total.astype(I32)


def _row_copy(src_ref, src_row, dst_ref, dst_row, sem):
    return pltpu.make_async_copy(src_ref.at[pl.ds(src_row, 1), :], dst_ref.at[pl.ds(dst_row, 1), :], sem)


def _rows_copy(src_ref, dst_ref, n_rows, sem):
    return pltpu.make_async_copy(src_ref.at[pl.ds(0, n_rows), :], dst_ref.at[pl.ds(0, n_rows), :], sem)


def _dispatch_kernel(pad_lo_ref, pad_hi_ref, pos_ref, x_ref, xs_ref, zero_ref, sem, zsem):
    tm = x_ref.shape[0]

    def issue(j, carry):
        for u in range(ROWS_PER_ISSUE):
            r = j * ROWS_PER_ISSUE + u
            for k in range(TOP_K):
                _row_copy(x_ref, r, xs_ref, pos_ref[k * tm + r], sem).start(priority=k % 2)
        return carry

    lax.fori_loop(0, tm // ROWS_PER_ISSUE, issue, 0)

    @pl.when(pl.program_id(0) == pl.num_programs(0) - 1)
    def _():
        zero_ref[...] = jnp.zeros_like(zero_ref)
        for e in range(N_EXPERTS):
            def pad(p, carry):
                _row_copy(zero_ref, 0, xs_ref, p, zsem).start()
                return carry
            lax.fori_loop(pad_lo_ref[e], pad_hi_ref[e], pad, 0)
        for e in range(N_EXPERTS):
            def pad_wait(p, carry):
                _row_copy(zero_ref, 0, xs_ref, 0, zsem).wait()
                return carry
            lax.fori_loop(pad_lo_ref[e], pad_hi_ref[e], pad_wait, 0)

    for k in range(TOP_K):
        _rows_copy(x_ref, xs_ref, tm, sem).wait()


def _dispatch(x1, pos_flat, pad_lo, pad_hi, n_rows):
    N, D = x1.shape
    tm = TM_DISPATCH
    return pl.pallas_call(
        _dispatch_kernel,
        grid_spec=pltpu.PrefetchScalarGridSpec(
            num_scalar_prefetch=2,
            grid=(N // tm,),
            in_specs=[pl.BlockSpec((tm * TOP_K,), lambda i, lo, hi: (i,), memory_space=pltpu.SMEM),
                      pl.BlockSpec((tm, D), lambda i, lo, hi: (i, 0))],
            out_specs=pl.BlockSpec(memory_space=pl.ANY),
            scratch_shapes=[pltpu.VMEM((8, D), F32), pltpu.SemaphoreType.DMA, pltpu.SemaphoreType.DMA]),
        out_shape=jax.ShapeDtypeStruct((n_rows, D), F32),
        compiler_params=pltpu.CompilerParams(dimension_semantics=("arbitrary",), vmem_limit_bytes=VMEM_LIMIT),
        name="dispatch",
    )(pad_lo, pad_hi, pos_flat, x1)


def _experts_kernel(te_ref, nv_ref, xs_ref, g_ref, wg_ref, bg_ref, wu_ref, bu_ref, wd_ref, bd_ref, ys_ref,
                    wg_s, wu_s, wd_s):
    i = pl.program_id(0)

    @pl.when((i == 0) | (te_ref[i] != te_ref[jnp.maximum(i - 1, 0)]))
    def _():
        wg_s[...] = wg_ref[...].astype(BF16)
        wu_s[...] = wu_ref[...].astype(BF16)
        wd_s[...] = wd_ref[...].astype(BF16)

    @pl.when(i < nv_ref[0])
    def _():
        h = (_rms(xs_ref[...]) * g_ref[...]).astype(BF16)
        gate = jnp.minimum(_dot(h, wg_s[...]) + bg_ref[...], SWIGLU_LIMIT)
        up = jnp.clip(_dot(h, wu_s[...]) + bu_ref[...], -SWIGLU_LIMIT, SWIGLU_LIMIT)
        act = (up + 1.0) * gate * jax.nn.sigmoid(SWIGLU_ALPHA * gate)
        ys_ref[...] = _dot(act.astype(BF16), wd_s[...]) + bd_ref[...]

    @pl.when(i >= nv_ref[0])
    def _():
        ys_ref[...] = jnp.zeros_like(ys_ref)


def _experts(xs, tile_expert, n_valid, g_mlp, w_gate, b_gate, w_up, b_up, w_down, b_down):
    P, D = xs.shape
    tm = TM_EXPERT
    E, _, F = w_gate.shape
    w_spec = pl.BlockSpec((None, D, F), lambda i, te, nv: (te[i], 0, 0))
    wd_spec = pl.BlockSpec((None, F, D), lambda i, te, nv: (te[i], 0, 0))
    b_spec = pl.BlockSpec((None, 1, F), lambda i, te, nv: (te[i], 0, 0))
    bd_spec = pl.BlockSpec((None, 1, D), lambda i, te, nv: (te[i], 0, 0))
    x_spec = pl.BlockSpec((tm, D), lambda i, te, nv: (jnp.minimum(i, nv[0] - 1), 0))
    return pl.pallas_call(
        _experts_kernel,
        grid_spec=pltpu.PrefetchScalarGridSpec(
            num_scalar_prefetch=2,
            grid=(P // tm,),
            in_specs=[x_spec, pl.BlockSpec((1, D), lambda i, te, nv: (0, 0)),
                      w_spec, b_spec, w_spec, b_spec, wd_spec, bd_spec],
            out_specs=pl.BlockSpec((tm, D), lambda i, te, nv: (i, 0)),
            scratch_shapes=[pltpu.VMEM((D, F), BF16), pltpu.VMEM((D, F), BF16), pltpu.VMEM((F, D), BF16)]),
        out_shape=jax.ShapeDtypeStruct((P, D), F32),
        compiler_params=pltpu.CompilerParams(dimension_semantics=("arbitrary",), vmem_limit_bytes=VMEM_LIMIT),
        name="experts",
    )(tile_expert, n_valid, xs, g_mlp.reshape(1, D),
      w_gate, b_gate.reshape(E, 1, F), w_up, b_up.reshape(E, 1, F), w_down, b_down.reshape(E, 1, D))


def _combine_kernel(pos_ref, x_ref, wgt_ref, ys_ref, o_ref, buf_ref, sem):
    tm = x_ref.shape[0]

    def issue(j, carry):
        for u in range(ROWS_PER_ISSUE):
            r = j * ROWS_PER_ISSUE + u
            for k in range(TOP_K):
                _row_copy(ys_ref, pos_ref[k * tm + r], buf_ref.at[k], r, sem).start(priority=k % 2)
        return carry

    lax.fori_loop(0, tm // ROWS_PER_ISSUE, issue, 0)
    for k in range(TOP_K):
        _rows_copy(ys_ref, buf_ref.at[k], tm, sem).wait()
    wgt = wgt_ref[...]
    acc = x_ref[...]
    for k in range(TOP_K):
        acc = acc + wgt[:, k:k + 1] * buf_ref[k]
    o_ref[...] = acc


def _combine(x1, wgt, pos_flat, ys):
    N, D = x1.shape
    tm = TM_COMBINE
    return pl.pallas_call(
        _combine_kernel,
        grid=(N // tm,),
        in_specs=[pl.BlockSpec((tm * TOP_K,), lambda i: (i,), memory_space=pltpu.SMEM),
                  pl.BlockSpec((tm, D), lambda i: (i, 0)),
                  pl.BlockSpec((tm, 128), lambda i: (i, 0)),
                  pl.BlockSpec(memory_space=pl.ANY)],
        out_specs=pl.BlockSpec((tm, D), lambda i: (i, 0)),
        out_shape=jax.ShapeDtypeStruct((N, D), F32),
        scratch_shapes=[pltpu.VMEM((TOP_K, tm, D), F32), pltpu.SemaphoreType.DMA],
        compiler_params=pltpu.CompilerParams(dimension_semantics=("arbitrary",), vmem_limit_bytes=VMEM_LIMIT),
        name="combine",
    )(pos_flat, x1, wgt, ys)


def _moe(x1, code, wgt, cnt, g_mlp, w_gate, b_gate, w_up, b_up, w_down, b_down):
    N, D = x1.shape
    tm = TM_EXPERT
    n_tiles = -(-(N * TOP_K + N_EXPERTS * (tm - 1)) // tm)
    counts = cnt[0, :N_EXPERTS]
    padded = ((counts + tm - 1) // tm) * tm
    ends = jnp.cumsum(padded)
    starts = ends - padded
    code = code[:TOP_K]
    pos = (starts[code // RANK_RADIX] + code % RANK_RADIX).astype(I32)

    def tiled(tile):
        return pos.reshape(TOP_K, N // tile, tile).transpose(1, 0, 2).reshape(-1)

    tile_start = jnp.arange(n_tiles, dtype=I32) * tm
    tile_expert = jnp.minimum(jnp.sum(ends[None, :] <= tile_start[:, None], axis=1), N_EXPERTS - 1).astype(I32)
    n_valid = (ends[-1] // tm).astype(I32).reshape(1)
    xs = _dispatch(x1, tiled(TM_DISPATCH), (starts + counts).astype(I32), ends.astype(I32), n_tiles * tm)
    ys = _experts(xs, tile_expert, n_valid, g_mlp, w_gate, b_gate, w_up, b_up, w_down, b_down)
    return _combine(x1, wgt, tiled(TM_COMBINE), ys)


def _layer(x, g_attn, w_in, qn_nsa, kn_cmp, kn_sel, kn_win,
           pos_cmp_k, w_ck1, b_ck1, w_ck2, b_ck2, pos_cmp_v, w_cv1, b_cv1, w_cv2, b_cv2,
           b_nsa_gate, qn_fox, kn_fox, b_forget, w_up_nsa, w_up_fox, b_merge, w_out, g_mlp,
           w_router, b_router, w_e_gate, b_e_gate, w_e_up, b_e_up, w_e_down, b_e_down):
    B, T, D = x.shape
    qn, ks, vs, kw, vw, kc_raw, vc_raw, qf, kf, vf, gates = _in_proj(
        x, g_attn, w_in, qn_nsa, kn_sel, kn_win, b_nsa_gate, qn_fox, kn_fox, b_forget)
    kc, vc = _compress(kc_raw, vc_raw, pos_cmp_k, w_ck1, b_ck1, w_ck2, b_ck2,
                       pos_cmp_v, w_cv1, b_cv1, w_cv2, b_cv2, kn_cmp)
    o_nsa = _nsa(qn, ks, vs, kw, vw, kc, vc, gates)
    o_fox = _fox(qf, kf, vf)
    hw = NSA_HEADS * HEAD_PAD
    w_merge = w_in[:, w_in.shape[1] - 2 * D:]
    x1, code, wgt, cnt = _merge_out(x.reshape(B * T, D), g_attn, o_nsa.reshape(B * T, hw), o_fox.reshape(B * T, hw),
                                    w_merge, b_merge, w_up_nsa, w_up_fox, w_out, g_mlp, w_router, b_router)
    out = _moe(x1, code, wgt, cnt, g_mlp, w_e_gate, b_e_gate, w_e_up, b_e_up, w_e_down, b_e_down)
    return out.reshape(B, T, D)


def kernel(x, g_attn, w_in, qn_nsa, kn_cmp, kn_sel, kn_win, pos_cmp_k, w_ck1, b_ck1, w_ck2, b_ck2, pos_cmp_v, w_cv1, b_cv1, w_cv2, b_cv2, b_nsa_gate, qn_fox, kn_fox, b_forget, w_up_nsa, w_up_fox, b_merge, w_out, g_mlp, w_router, b_router, w_e_gate, b_e_gate, w_e_up, b_e_up, w_e_down, b_e_down):
    params = (g_attn, w_in, qn_nsa, kn_cmp, kn_sel, kn_win, pos_cmp_k, w_ck1, b_ck1, w_ck2, b_ck2,
              pos_cmp_v, w_cv1, b_cv1, w_cv2, b_cv2, b_nsa_gate, qn_fox, kn_fox, b_forget,
              w_up_nsa, w_up_fox, b_merge, w_out, g_mlp, w_router, b_router,
              w_e_gate, b_e_gate, w_e_up, b_e_up, w_e_down, b_e_down)
    for layer in range(g_attn.shape[0]):
        x = _layer(x, *[p[layer] for p in params])
    return x
```

```python
import functools

import jax
import jax.numpy as jnp
from jax import lax
from jax.experimental import pallas as pl
from jax.experimental.pallas import tpu as pltpu

F32 = jnp.float32
BF16 = jnp.bfloat16
I32 = jnp.int32

D_MODEL = 1024
HEAD_DIM = 64
HEAD_PAD = 128
NSA_HEADS = 8
NSA_GROUPS = 2
NSA_HPG = NSA_HEADS // NSA_GROUPS
FOX_HEADS = 8
CMP_BLOCK = 32
CMP_STRIDE = 16
CMP_HIDDEN = 256
N_CMP_PAD = 128
SEL_BLOCK = 64
SEL_TOPN = 16
N_SEL_BLOCKS = 32
WINDOW = 512
N_EXPERTS = 32
TOP_K = 4
SWIGLU_LIMIT = 7.0
SWIGLU_ALPHA = 1.702
RMS_EPS = 1e-6
NEG_BIG = -1e30
FORCED_SCORE = 1e9
QK_SCALE = HEAD_DIM ** -0.5

LANE_SEL = 64
LANE_POS = 96
LANE_FOX = 64

OFF_QN = 0
OFF_KS = OFF_QN + NSA_HEADS * HEAD_PAD
OFF_VS = OFF_KS + NSA_GROUPS * HEAD_PAD
OFF_KW = OFF_VS + NSA_GROUPS * HEAD_PAD
OFF_VW = OFF_KW + NSA_GROUPS * HEAD_PAD
OFF_KC = OFF_VW + NSA_GROUPS * HEAD_PAD
OFF_VC = OFF_KC + NSA_GROUPS * HEAD_DIM
OFF_QF = OFF_VC + NSA_GROUPS * HEAD_DIM
OFF_KF = OFF_QF + FOX_HEADS * HEAD_DIM
OFF_VF = OFF_KF + FOX_HEADS * HEAD_DIM
OFF_MISC = OFF_VF + FOX_HEADS * HEAD_DIM
W1_COLS = OFF_MISC + 128
MISC_GATES = 3 * NSA_HEADS

TM_PROJ = 256
TQ_NSA = 128
TQ_SELECT = 512
TILES_PER_REGION = 4
SELECTS_PER_REGION = 2
CK_ATTN = 256
TM_OUT = 512
TM_EXPERT = 512
TM_DISPATCH = 2048
TM_COMBINE = 1024
ROWS_PER_ISSUE = 8
RANK_RADIX = 1 << 16
VMEM_LIMIT = 56 * 1024 * 1024


def _dot(a, b):
    return jnp.dot(a, b, preferred_element_type=F32)


def _dot_nt(a, b):
    return lax.dot_general(a, b, (((1,), (1,)), ((), ())), preferred_element_type=F32)


def _split3(v):
    hi = v.astype(BF16)
    r1 = v - hi.astype(F32)
    mid = r1.astype(BF16)
    lo = (r1 - mid.astype(F32)).astype(BF16)
    return hi, mid, lo


def _dot_exact_rhs(a_f32, b_bf16):
    hi, mid, lo = _split3(a_f32)
    return _dot(hi, b_bf16) + _dot(mid, b_bf16) + _dot(lo, b_bf16)


def _dot_exact_lhs(a_bf16, b_f32):
    hi, mid, lo = _split3(b_f32)
    return _dot(a_bf16, hi) + _dot(a_bf16, mid) + _dot(a_bf16, lo)


def _rms(x):
    return x * lax.rsqrt(jnp.mean(x * x, axis=-1, keepdims=True) + RMS_EPS)


def _head_rms(a, gain):
    ms = jnp.sum(a * a, axis=-1, keepdims=True) * (1.0 / HEAD_DIM)
    return a * lax.rsqrt(ms + RMS_EPS) * gain


def _pos_feat_k(pos, lane):
    hi = (256 * (pos >> 8)).astype(F32)
    lo = (pos & 255).astype(F32)
    return jnp.where(lane < LANE_POS + 2, 1.0, jnp.where(lane == LANE_POS + 2, hi, lo)) * (
        (lane >= LANE_POS) & (lane < LANE_POS + 4)).astype(F32)


def _in_proj_kernel(x_ref, g_ref, w_ref, gains_ref, bmisc_ref, pq_ref, pk_ref,
                    qn_ref, ks_ref, vs_ref, kw_ref, vw_ref, kc_ref, vc_ref,
                    qf_ref, kf_ref, vf_ref, gate_ref, carry_ref):
    tm = x_ref.shape[0]
    i = pl.program_id(1)
    h = (_rms(x_ref[...]) * g_ref[...]).astype(BF16)
    pos = i * tm + lax.broadcasted_iota(I32, (tm, 1), 0)
    lane = lax.broadcasted_iota(I32, (1, HEAD_PAD), 1)
    pos_hi = (256 * (pos >> 8)).astype(F32)
    pos_lo = (pos & 255).astype(F32)
    kfeat = _pos_feat_k(pos, lane)
    blk_onehot = ((lane - LANE_SEL) == (pos >> 6)).astype(F32)

    def seg(off, width):
        return _dot(h, w_ref[:, off:off + width])

    a = seg(OFF_QN, NSA_HEADS * HEAD_PAD)
    gq = gains_ref[0:1, :] * QK_SCALE
    for hd in range(NSA_HEADS):
        slope = 2.0 ** (-(hd + 1))
        qfeat = jnp.where(lane == LANE_POS, -slope * pos_hi,
                          jnp.where(lane == LANE_POS + 1, -slope * pos_lo,
                                    jnp.where((lane == LANE_POS + 2) | (lane == LANE_POS + 3), slope, 0.0)))
        sl = slice(hd * HEAD_PAD, (hd + 1) * HEAD_PAD)
        qn_ref[:, sl] = (_head_rms(a[:, sl], gq) + qfeat).astype(BF16)

    a = seg(OFF_KS, NSA_GROUPS * HEAD_PAD)
    for g in range(NSA_GROUPS):
        sl = slice(g * HEAD_PAD, (g + 1) * HEAD_PAD)
        ks_ref[:, sl] = (_head_rms(a[:, sl], gains_ref[1:2, :]) + kfeat + blk_onehot).astype(BF16)
    vs_ref[...] = seg(OFF_VS, NSA_GROUPS * HEAD_PAD).astype(BF16)
    a = seg(OFF_KW, NSA_GROUPS * HEAD_PAD)
    for g in range(NSA_GROUPS):
        sl = slice(g * HEAD_PAD, (g + 1) * HEAD_PAD)
        kw_ref[:, sl] = (_head_rms(a[:, sl], gains_ref[2:3, :]) + kfeat).astype(BF16)
    vw_ref[...] = seg(OFF_VW, NSA_GROUPS * HEAD_PAD).astype(BF16)
    kc_ref[...] = seg(OFF_KC, NSA_GROUPS * HEAD_DIM).astype(BF16)
    vc_ref[...] = seg(OFF_VC, NSA_GROUPS * HEAD_DIM).astype(BF16)

    misc = seg(OFF_MISC, 128) + bmisc_ref[...]
    is_gate = lane < MISC_GATES
    gate_ref[...] = jnp.where(is_gate, jax.nn.sigmoid(misc), 0.0)
    is_f = (lane >= MISC_GATES) & (lane < MISC_GATES + 3 * FOX_HEADS)
    logf = jnp.where(is_f, jax.nn.log_sigmoid(misc), 0.0)

    @pl.when(i == 0)
    def _():
        carry_ref[...] = jnp.zeros_like(carry_ref)

    r = lax.broadcasted_iota(I32, (tm, tm), 0)
    c = lax.broadcasted_iota(I32, (tm, tm), 1)
    tri = (r >= c).astype(BF16)
    csum = _dot_exact_lhs(tri, logf) + carry_ref[...]
    carry_ref[...] = csum[tm - 1:tm, :]
    c_hi, c_mid, c_lo = _split3(csum)
    c3 = jnp.where(lane < MISC_GATES + FOX_HEADS, c_hi,
                   jnp.where(lane < MISC_GATES + 2 * FOX_HEADS, c_mid, c_lo))
    c3 = jnp.where(is_f, c3, jnp.zeros_like(c3))
    low = lane < HEAD_DIM

    def fox_heads(off, gain2, feat, ones_at, out_ref):
        a = seg(off, FOX_HEADS * HEAD_DIM)
        for j in range(FOX_HEADS // 2):
            pair = a[:, j * HEAD_PAD:(j + 1) * HEAD_PAD]
            sq = pair * pair
            ms_even = jnp.sum(jnp.where(low, sq, 0.0), axis=-1, keepdims=True) * (1.0 / HEAD_DIM)
            ms_odd = jnp.sum(jnp.where(low, 0.0, sq), axis=-1, keepdims=True) * (1.0 / HEAD_DIM)
            normed = pair * jnp.where(low, lax.rsqrt(ms_even + RMS_EPS), lax.rsqrt(ms_odd + RMS_EPS)) * gain2
            for parity in range(2):
                hd = 2 * j + parity
                base = _fox_feature_base(hd)
                ones = ((lane >= base + ones_at) & (lane < base + ones_at + 3)).astype(F32)
                keep = low if parity == 0 else jnp.logical_not(low)
                slab = jnp.where(keep, normed, 0.0) + feat[:, hd * HEAD_PAD:(hd + 1) * HEAD_PAD] + ones
                out_ref[:, hd * HEAD_PAD:(hd + 1) * HEAD_PAD] = slab.astype(BF16)

    fox_heads(OFF_QF, gains_ref[5:6, :] * QK_SCALE, _dot(c3, pq_ref[...]), 3, qf_ref)
    fox_heads(OFF_KF, gains_ref[6:7, :], _dot(c3, pk_ref[...]), 0, kf_ref)
    vf_ref[...] = seg(OFF_VF, FOX_HEADS * HEAD_DIM).astype(BF16)


def _pad_heads(w, n_heads):
    lead = w.shape[:-1]
    w = w.reshape(lead + (n_heads, HEAD_DIM))
    w = jnp.pad(w, [(0, 0)] * len(lead) + [(0, 0), (0, HEAD_PAD - HEAD_DIM)])
    return w.reshape(lead + (n_heads * HEAD_PAD,))


def _pad_gain(g):
    return jnp.pad(g.astype(F32), (0, HEAD_PAD - HEAD_DIM))


def _fox_feature_base(hd):
    return HEAD_DIM if hd % 2 == 0 else 0


def _fox_placement():
    pq = jnp.zeros((128, FOX_HEADS * HEAD_PAD), F32)
    pk = jnp.zeros((128, FOX_HEADS * HEAD_PAD), F32)
    for piece in range(3):
        for hd in range(FOX_HEADS):
            src = MISC_GATES + piece * FOX_HEADS + hd
            base = hd * HEAD_PAD + _fox_feature_base(hd)
            pq = pq.at[src, base + piece].set(1.0)
            pk = pk.at[src, base + 3 + piece].set(-1.0)
    return pq.astype(BF16), pk.astype(BF16)


def _in_proj(x, g_attn, w_in, qn_nsa, kn_sel, kn_win, b_nsa_gate, qn_fox, kn_fox, b_forget):
    B, T, D = x.shape
    s1 = NSA_HEADS * HEAD_DIM
    kv_w = NSA_GROUPS * HEAD_DIM
    s2 = s1 + 6 * kv_w
    s3 = s2 + 3 * NSA_HEADS
    fw = FOX_HEADS * HEAD_DIM
    s4 = s3 + 3 * fw
    s5 = s4 + FOX_HEADS
    kv = [w_in[:, s1 + j * kv_w: s1 + (j + 1) * kv_w] for j in range(6)]
    f_cols = w_in[:, s4:s5]
    misc = jnp.concatenate([w_in[:, s2:s3], f_cols, f_cols, f_cols,
                            jnp.zeros((D, 128 - MISC_GATES - 3 * FOX_HEADS), F32)], axis=1)
    w1 = jnp.concatenate([
        _pad_heads(w_in[:, :s1], NSA_HEADS),
        _pad_heads(kv[2], NSA_GROUPS), _pad_heads(kv[3], NSA_GROUPS),
        _pad_heads(kv[4], NSA_GROUPS), _pad_heads(kv[5], NSA_GROUPS),
        kv[0], kv[1],
        w_in[:, s3:s4],
        misc], axis=1).astype(BF16)
    assert w1.shape[1] == W1_COLS
    gains = jnp.stack([_pad_gain(qn_nsa), _pad_gain(kn_sel), _pad_gain(kn_win),
                       _pad_gain(qn_fox), _pad_gain(kn_fox),
                       jnp.tile(qn_fox.astype(F32), 2), jnp.tile(kn_fox.astype(F32), 2), jnp.zeros(HEAD_PAD, F32)])
    bmisc = jnp.concatenate([b_nsa_gate, b_forget, b_forget, b_forget,
                             jnp.zeros(128 - MISC_GATES - 3 * FOX_HEADS, F32)]).reshape(1, 128)
    pq, pk = _fox_placement()
    tm = TM_PROJ

    def tok(width):
        return pl.BlockSpec((None, tm, width), lambda b, i: (b, i, 0))

    def full(shape):
        return pl.BlockSpec(shape, lambda b, i: (0,) * len(shape))

    widths = [NSA_HEADS * HEAD_PAD] + [NSA_GROUPS * HEAD_PAD] * 4 + [NSA_GROUPS * HEAD_DIM] * 2 + \
             [FOX_HEADS * HEAD_PAD] * 2 + [FOX_HEADS * HEAD_DIM]
    out_shape = [jax.ShapeDtypeStruct((B, T, w), BF16) for w in widths] + \
                [jax.ShapeDtypeStruct((B, T, 128), F32)]
    return pl.pallas_call(
        _in_proj_kernel,
        grid=(B, T // tm),
        in_specs=[tok(D), full((1, D)), full((D, W1_COLS)), full((8, HEAD_PAD)), full((1, 128)),
                  full((128, FOX_HEADS * HEAD_PAD)), full((128, FOX_HEADS * HEAD_PAD))],
        out_specs=[tok(w) for w in widths] + [tok(128)],
        out_shape=out_shape,
        scratch_shapes=[pltpu.VMEM((1, 128), F32)],
        compiler_params=pltpu.CompilerParams(
            dimension_semantics=("arbitrary", "arbitrary"), vmem_limit_bytes=VMEM_LIMIT),
        name="in_proj",
    )(x, g_attn.reshape(1, D), w1, gains, bmisc, pq, pk)


def _gelu_tanh(x):
    return 0.5 * x * (1.0 + jnp.tanh(0.7978845608028654 * (x + 0.044715 * (x * x * x))))


def _compress_kernel(ak_ref, av_ref, posk_ref, posv_ref, wk1_ref, bk1_ref, wk2_ref, bk2_ref,
                     wv1_ref, bv1_ref, wv2_ref, bv2_ref, gain_ref, kc_ref, vc_ref):
    row = lax.broadcasted_iota(I32, (N_CMP_PAD, 1), 0)
    lane = lax.broadcasted_iota(I32, (1, HEAD_PAD), 1)
    kfeat = _pos_feat_k(CMP_STRIDE * row + CMP_BLOCK - 1, lane)

    def mlp(a_ref, pos_ref, w1_ref, b1_ref, w2_ref, b2_ref):
        a = a_ref[...].astype(F32)
        lo = (a + pos_ref[0:1, :]).astype(BF16)
        hi = (a + pos_ref[1:2, :]).astype(BF16)
        outs = []
        for g in range(NSA_GROUPS):
            hid = _dot(lo, w1_ref[0, g]) + pltpu.roll(_dot(hi, w1_ref[1, g]), N_CMP_PAD - 1, 0) + b1_ref[...]
            outs.append(_dot(_gelu_tanh(hid).astype(BF16), w2_ref[...]) + b2_ref[...])
        return outs

    ks = mlp(ak_ref, posk_ref, wk1_ref, bk1_ref, wk2_ref, bk2_ref)
    vs = mlp(av_ref, posv_ref, wv1_ref, bv1_ref, wv2_ref, bv2_ref)
    for g in range(NSA_GROUPS):
        kc_ref[g] = (_head_rms(ks[g], gain_ref[...]) + kfeat).astype(BF16)
        vc_ref[g] = vs[g].astype(BF16)


def _compress(kc_raw, vc_raw, pos_k, w_ck1, b_ck1, w_ck2, b_ck2, pos_v, w_cv1, b_cv1, w_cv2, b_cv2, kn_cmp):
    B, T, _ = kc_raw.shape
    n_chunks = T // CMP_STRIDE
    assert n_chunks == N_CMP_PAD
    chunk_w = CMP_STRIDE * NSA_GROUPS * HEAD_DIM

    def chunks(a):
        return a.reshape(B, n_chunks, chunk_w)

    def w1_rows(w):
        w = w.reshape(2, CMP_STRIDE, 1, HEAD_DIM, CMP_HIDDEN)
        per_group = [jnp.concatenate([w if gg == g else jnp.zeros_like(w) for gg in range(NSA_GROUPS)], axis=2)
                     for g in range(NSA_GROUPS)]
        return jnp.stack(per_group, axis=1).reshape(2, NSA_GROUPS, chunk_w, CMP_HIDDEN).astype(BF16)

    def pos_rows(p):
        p = jnp.broadcast_to(p.reshape(2, CMP_STRIDE, 1, HEAD_DIM), (2, CMP_STRIDE, NSA_GROUPS, HEAD_DIM))
        return p.reshape(2, chunk_w)

    def w2pad(w, b):
        return (jnp.pad(w, ((0, 0), (0, HEAD_PAD - HEAD_DIM))).astype(BF16),
                jnp.pad(b, (0, HEAD_PAD - HEAD_DIM)).reshape(1, HEAD_PAD))

    wk2, bk2 = w2pad(w_ck2, b_ck2)
    wv2, bv2 = w2pad(w_cv2, b_cv2)
    a_spec = pl.BlockSpec((None, n_chunks, chunk_w), lambda b: (b, 0, 0))
    o_spec = pl.BlockSpec((None, NSA_GROUPS, N_CMP_PAD, HEAD_PAD), lambda b: (b, 0, 0, 0))

    def full(shape):
        return pl.BlockSpec(shape, lambda b: (0,) * len(shape))

    w1_shape = (2, NSA_GROUPS, chunk_w, CMP_HIDDEN)
    return pl.pallas_call(
        _compress_kernel,
        grid=(B,),
        in_specs=[a_spec, a_spec, full((2, chunk_w)), full((2, chunk_w)),
                  full(w1_shape), full((1, CMP_HIDDEN)), full((CMP_HIDDEN, HEAD_PAD)), full((1, HEAD_PAD)),
                  full(w1_shape), full((1, CMP_HIDDEN)), full((CMP_HIDDEN, HEAD_PAD)), full((1, HEAD_PAD)),
                  full((1, HEAD_PAD))],
        out_specs=[o_spec, o_spec],
        out_shape=[jax.ShapeDtypeStruct((B, NSA_GROUPS, N_CMP_PAD, HEAD_PAD), BF16)] * 2,
        compiler_params=pltpu.CompilerParams(dimension_semantics=("arbitrary",), vmem_limit_bytes=VMEM_LIMIT),
        name="compress",
    )(chunks(kc_raw), chunks(vc_raw), pos_rows(pos_k), pos_rows(pos_v),
      w1_rows(w_ck1), b_ck1.reshape(1, -1), wk2, bk2,
      w1_rows(w_cv1), b_cv1.reshape(1, -1), wv2, bv2, _pad_gain(kn_cmp).reshape(1, HEAD_PAD))


def _lane_blocks(a):
    return [a[:, i * 128:(i + 1) * 128] for i in range(a.shape[1] // 128)]


def _fold_blocks(op, acc, a):
    for blk in _lane_blocks(a):
        acc = blk if acc is None else op(acc, blk)
    return acc


def _nsa_kernel(q_ref, ks_ref, vs_ref, kw_ref, vw_ref, kc_ref, vct_ref, gate_ref, ovl_ref, o_ref,
                sel_ref, oc_ref, s_ref):
    seq = q_ref.shape[0]
    tq = TQ_NSA
    ta = TQ_SELECT
    ck = CK_ATTN
    span = WINDOW + tq
    g = pl.program_id(1)
    col_l = lax.broadcasted_iota(I32, (1, 128), 1)
    c_end_sub = CMP_STRIDE * lax.broadcasted_iota(I32, (N_CMP_PAD, 1), 0) + (CMP_BLOCK - 1)
    blk_id = lax.broadcasted_iota(I32, (N_SEL_BLOCKS, 1), 0)
    sub_id = lax.broadcasted_iota(I32, (8, 1), 0)

    def gate_col(gates, branch, hh):
        lo = branch * NSA_HEADS + hh
        return jnp.where(g == 0, gates[:, lo:lo + 1], gates[:, lo + NSA_HPG:lo + NSA_HPG + 1])

    def select(a):
        a0 = a * ta
        t_lane = a0 + lax.broadcasted_iota(I32, (1, ta), 1)
        gates = gate_ref[a0:a0 + ta, :]

        mask_t = t_lane >= c_end_sub
        p_sum = None
        for hh in range(NSA_HPG):
            q_h = q_ref[a0:a0 + ta, hh * HEAD_PAD:(hh + 1) * HEAD_PAD]
            s_t = jnp.where(mask_t, _dot_nt(kc_ref[...], q_h), NEG_BIG)
            e_t = jnp.where(mask_t, jnp.exp(s_t - jnp.max(s_t, axis=0, keepdims=True)), 0.0)
            l_t = jnp.sum(e_t, axis=0, keepdims=True)
            p_t = e_t / jnp.where(l_t > 0.0, l_t, 1.0)
            o_t = _dot(vct_ref[...], p_t.astype(BF16))
            g_c = gate_col(gates, 0, hh)
            for blk in range(ta // 128):
                bs = slice(blk * 128, (blk + 1) * 128)
                oc_ref[a0 + blk * 128:a0 + (blk + 1) * 128, hh * HEAD_PAD:(hh + 1) * HEAD_PAD] = g_c[bs] * o_t[:, bs].T
            p_sum = p_t if p_sum is None else p_sum + p_t
        imp = _dot_exact_lhs(ovl_ref[...], p_sum)
        tb = t_lane >> 6
        valid = blk_id <= tb
        forced = (blk_id == 0) | (blk_id == tb) | (blk_id == tb - 1)
        score = jnp.where(forced & valid, FORCED_SCORE, imp)
        score = jnp.where(valid, score, -1.0)
        groups = [score[8 * b:8 * b + 8, :] for b in range(N_SEL_BLOCKS // 8)]
        n_above = [jnp.zeros((8, ta), F32) for _ in groups]
        for j in range(N_SEL_BLOCKS):
            cj = score[j:j + 1, :]
            for b, grp in enumerate(groups):
                if 8 * b > j:
                    n_above[b] = n_above[b] + jnp.where(cj >= grp, 1.0, 0.0)
                elif 8 * b + 7 < j:
                    n_above[b] = n_above[b] + jnp.where(cj > grp, 1.0, 0.0)
                else:
                    tie = jnp.where(sub_id + 8 * b > j, 1.0, 0.0)
                    n_above[b] = n_above[b] + jnp.where(cj > grp, 1.0, 0.0) + jnp.where(cj == grp, tie, 0.0)
        dropped = jnp.where(jnp.concatenate(n_above, axis=0) >= float(SEL_TOPN), NEG_BIG, 0.0)
        drop_t = jnp.concatenate([jnp.zeros((LANE_SEL, ta), F32), dropped,
                                  jnp.zeros((HEAD_PAD - LANE_SEL - N_SEL_BLOCKS, ta), F32)], axis=0)
        for blk in range(ta // 128):
            sel_ref[a0 + blk * 128:a0 + (blk + 1) * 128, :] = drop_t[:, blk * 128:(blk + 1) * 128].T.astype(BF16)

    for first in range(0, seq // ta, SELECTS_PER_REGION):
        @pl.when(g >= -first)
        def _(first=first):
            for a in range(first, first + SELECTS_PER_REGION):
                select(a)

    rows = NSA_HPG * tq
    row_l = lax.broadcasted_iota(I32, (rows, 1), 0) & (tq - 1)

    def attend(i):
        t0 = i * tq
        t_row = t0 + row_l
        q4 = jnp.concatenate([q_ref[t0:t0 + tq, hh * HEAD_PAD:(hh + 1) * HEAD_PAD] for hh in range(NSA_HPG)], axis=0)
        q4s = q4 + jnp.concatenate([sel_ref[t0:t0 + tq, :]] * NSA_HPG, axis=0)

        n_chunks = (t0 + tq - 1) // ck + 1
        slot = i % 2
        mx = None
        for c in range(n_chunks):
            s = _dot_nt(q4s, ks_ref[c * ck:(c + 1) * ck, :])
            if (c + 1) * ck > t0:
                s = jnp.where(t_row >= c * ck + lax.broadcasted_iota(I32, (1, ck), 1), s, NEG_BIG)
            s_ref[slot, :, c * ck:(c + 1) * ck] = s
            mx = _fold_blocks(jnp.maximum, mx, s)
        m = jnp.max(mx, axis=-1, keepdims=True)
        ls = None
        acc = None
        for c in range(n_chunks):
            p = jnp.exp(s_ref[slot, :, c * ck:(c + 1) * ck] - m)
            ls = _fold_blocks(jnp.add, ls, p)
            pv = _dot(p.astype(BF16), vs_ref[c * ck:(c + 1) * ck, :])
            acc = pv if acc is None else acc + pv
        o_s = acc / jnp.sum(ls, axis=-1, keepdims=True)

        ws = max(t0 - WINDOW, 0)
        blocks = _lane_blocks(_dot_nt(q4, kw_ref[ws:ws + span, :]))
        if t0 < WINDOW:
            blocks = [jnp.where(t_row >= ws + jb * 128 + col_l, blk, NEG_BIG) if ws + (jb + 1) * 128 > t0 else blk
                      for jb, blk in enumerate(blocks)]
        else:
            blocks[0] = jnp.where(col_l > row_l, blocks[0], NEG_BIG)
            blocks[-1] = jnp.where(col_l <= row_l, blocks[-1], NEG_BIG)
        s = jnp.concatenate(blocks, axis=1)
        m = jnp.max(_fold_blocks(jnp.maximum, None, s), axis=-1, keepdims=True)
        p = jnp.exp(s - m)
        l = jnp.sum(_fold_blocks(jnp.add, None, p), axis=-1, keepdims=True)
        o_w = _dot(p.astype(BF16), vw_ref[ws:ws + span, :]) / l

        gates = gate_ref[t0:t0 + tq, :]
        outs = []
        for hh in range(NSA_HPG):
            rs = slice(hh * tq, (hh + 1) * tq)
            o = oc_ref[t0:t0 + tq, hh * HEAD_PAD:(hh + 1) * HEAD_PAD]
            outs.append((o + gate_col(gates, 1, hh) * o_s[rs] + gate_col(gates, 2, hh) * o_w[rs]).astype(BF16))
        o_ref[t0:t0 + tq, :] = jnp.concatenate(outs, axis=1)

    for first in range(0, seq // tq, TILES_PER_REGION):
        @pl.when(g > -1 - first)
        def _(first=first):
            for i in range(first, first + TILES_PER_REGION):
                attend(i)


def _overlap_matrix():
    c = jnp.arange(N_CMP_PAD)[None, :]
    j = jnp.arange(N_SEL_BLOCKS)[:, None]
    lo = (CMP_STRIDE * c) // SEL_BLOCK
    hi = (CMP_STRIDE * c + CMP_BLOCK - 1) // SEL_BLOCK
    return ((j == lo) | (j == hi)).astype(BF16)


def _nsa(qn, ks, vs, kw, vw, kc, vc, gates):
    B, T, _ = qn.shape
    gw = NSA_HPG * HEAD_PAD
    q_spec = pl.BlockSpec((None, T, gw), lambda b, g: (b, 0, g))
    kv_spec = pl.BlockSpec((None, T, HEAD_PAD), lambda b, g: (b, 0, g))
    c_spec = pl.BlockSpec((None, None, N_CMP_PAD, HEAD_PAD), lambda b, g: (b, g, 0, 0))
    return pl.pallas_call(
        _nsa_kernel,
        grid=(B, NSA_GROUPS),
        in_specs=[q_spec, kv_spec, kv_spec, kv_spec, kv_spec, c_spec, c_spec,
                  pl.BlockSpec((None, T, 128), lambda b, g: (b, 0, 0)),
                  pl.BlockSpec((N_SEL_BLOCKS, N_CMP_PAD), lambda b, g: (0, 0))],
        out_specs=q_spec,
        out_shape=jax.ShapeDtypeStruct((B, T, NSA_HEADS * HEAD_PAD), BF16),
        scratch_shapes=[pltpu.VMEM((T, HEAD_PAD), BF16), pltpu.VMEM((T, gw), F32),
                        pltpu.VMEM((2, NSA_HPG * TQ_NSA, T), F32)],
        compiler_params=pltpu.CompilerParams(
            dimension_semantics=("arbitrary", "arbitrary"), vmem_limit_bytes=VMEM_LIMIT),
        name="nsa",
    )(qn, ks, vs, kw, vw, kc, vc.swapaxes(2, 3), gates, _overlap_matrix())


def _fox_kernel(q_ref, k_ref, v_ref, o_ref):
    seq = q_ref.shape[0]
    ck = CK_ATTN
    row = lax.broadcasted_iota(I32, (ck, 1), 0)
    col = lax.broadcasted_iota(I32, (1, ck), 1)
    for qi in range(seq // ck):
        q = q_ref[qi * ck:(qi + 1) * ck, :]
        s_chunks = []
        mx = None
        for c in range(qi + 1):
            s = _dot_nt(q, k_ref[c * ck:(c + 1) * ck, :])
            if c == qi:
                s = jnp.where(row >= col, s, NEG_BIG)
            s_chunks.append(s)
            mx = _fold_blocks(jnp.maximum, mx, s)
        m = jnp.max(mx, axis=-1, keepdims=True)
        ls = None
        acc = None
        for c in range(qi + 1):
            p = jnp.exp(s_chunks[c] - m)
            ls = _fold_blocks(jnp.add, ls, p)
            pv = _dot(p.astype(BF16), v_ref[c * ck:(c + 1) * ck, :])
            acc = pv if acc is None else acc + pv
        o_ref[qi * ck:(qi + 1) * ck, :] = (acc / jnp.sum(ls, axis=-1, keepdims=True)).astype(BF16)


def _fox(qf, kf, vf):
    B, T, _ = qf.shape
    spec = pl.BlockSpec((None, T, HEAD_PAD), lambda b, h: (b, 0, h))
    v_spec = pl.BlockSpec((None, T, HEAD_PAD), lambda b, h: (b, 0, h // 2))
    return pl.pallas_call(
        _fox_kernel,
        grid=(B, FOX_HEADS),
        in_specs=[spec, spec, v_spec],
        out_specs=spec,
        out_shape=jax.ShapeDtypeStruct((B, T, FOX_HEADS * HEAD_PAD), BF16),
        compiler_params=pltpu.CompilerParams(
            dimension_semantics=("arbitrary", "arbitrary"), vmem_limit_bytes=VMEM_LIMIT),
        name="fox",
    )(qf, kf, vf)


def _merge_out_kernel(x_ref, g_ref, on_ref, of_ref, wm_ref, bm_ref, wun_ref, wuf_ref, wo_ref,
                      gm_ref, wrh_ref, wrm_ref, br_ref, o_ref, code_ref, wgt_ref, cnt_ref, carry_ref):
    x = x_ref[...]
    h = (_rms(x) * g_ref[...]).astype(BF16)
    merge = jax.nn.sigmoid(_dot(h, wm_ref[...]) + bm_ref[...])
    y = merge[:, :D_MODEL] * _dot(on_ref[...], wun_ref[...]) + merge[:, D_MODEL:] * _dot(of_ref[...], wuf_ref[...])
    x1 = x + _dot(y.astype(BF16), wo_ref[...])
    o_ref[...] = x1
    _route_tile(x1, gm_ref, wrh_ref, wrm_ref, br_ref, code_ref, wgt_ref, cnt_ref, carry_ref)


def _merge_out(x2, g_attn, o_nsa, o_fox, w_merge, b_merge, w_up_nsa, w_up_fox, w_out, g_mlp, w_router, b_router):
    N, D = x2.shape
    tm = TM_OUT
    assert N <= RANK_RADIX
    wr = jnp.pad(w_router, ((0, 0), (0, 128 - N_EXPERTS)))
    wr_hi = wr.astype(BF16)
    wr_mid = (wr - wr_hi.astype(F32)).astype(BF16)
    br = jnp.pad(b_router, (0, 128 - N_EXPERTS)).reshape(1, 128)

    def pad_rows(w, n_heads):
        return _pad_heads(w.T, n_heads).T.astype(BF16)

    def fox_rows(w):
        w = w.reshape(FOX_HEADS, HEAD_DIM, D)
        z = jnp.zeros_like(w)
        even = jnp.concatenate([w, z], axis=1)
        odd = jnp.concatenate([z, w], axis=1)
        parity = (jnp.arange(FOX_HEADS) % 2 == 0)[:, None, None]
        return jnp.where(parity, even, odd).reshape(FOX_HEADS * HEAD_PAD, D).astype(BF16)

    def tok(width):
        return pl.BlockSpec((tm, width), lambda i: (i, 0))

    def full(shape):
        return pl.BlockSpec(shape, lambda i: (0,) * len(shape))

    hw = NSA_HEADS * HEAD_PAD
    return pl.pallas_call(
        _merge_out_kernel,
        grid=(N // tm,),
        in_specs=[tok(D), full((1, D)), tok(hw), tok(hw), full((D, 2 * D)), full((1, 2 * D)),
                  full((hw, D)), full((hw, D)), full((D, D)),
                  full((1, D)), full((D, 128)), full((D, 128)), full((1, 128))],
        out_specs=[tok(D), pl.BlockSpec((8, tm), lambda i: (0, i)), tok(128), full((1, 128))],
        out_shape=[jax.ShapeDtypeStruct((N, D), F32), jax.ShapeDtypeStruct((8, N), I32),
                   jax.ShapeDtypeStruct((N, 128), F32), jax.ShapeDtypeStruct((1, 128), I32)],
        scratch_shapes=[pltpu.VMEM((1, 128), F32)],
        compiler_params=pltpu.CompilerParams(dimension_semantics=("arbitrary",), vmem_limit_bytes=VMEM_LIMIT),
        name="merge_out",
    )(x2, g_attn.reshape(1, D), o_nsa, o_fox, w_merge.astype(BF16), b_merge.reshape(1, 2 * D),
      pad_rows(w_up_nsa, NSA_HEADS), fox_rows(w_up_fox), w_out.astype(BF16),
      g_mlp.reshape(1, D), wr_hi, wr_mid, br)


def _route_tile(x, g_ref, wh_ref, wm_ref, br_ref, code_ref, wgt_ref, cnt_ref, carry_ref):
    tm = x.shape[0]

    @pl.when(pl.program_id(0) == 0)
    def _():
        carry_ref[...] = jnp.zeros_like(carry_ref)

    h = _rms(x) * g_ref[...]
    h_hi = h.astype(BF16)
    h_mid = (h - h_hi.astype(F32)).astype(BF16)
    logits = _dot(h_hi, wh_ref[...]) + _dot(h_hi, wm_ref[...]) + _dot(h_mid, wh_ref[...]) + br_ref[...]
    lane = lax.broadcasted_iota(I32, (1, 128), 1)
    lane_f = lane.astype(F32)
    work = jnp.where(lane < N_EXPERTS, logits, -jnp.inf)
    vals, hots = [], []
    idx_out = jnp.zeros((tm, 128), F32)
    for k in range(TOP_K):
        m = jnp.max(work, axis=-1, keepdims=True)
        idx = jnp.min(jnp.where(work == m, lane_f, 128.0), axis=-1, keepdims=True)
        hot = lane_f == idx
        work = jnp.where(hot, -jnp.inf, work)
        vals.append(m)
        hots.append(hot)
        idx_out = idx_out + jnp.where(lane == k, idx, 0.0)
    e = [jnp.exp(v - vals[0]) for v in vals]
    denom = e[0] + e[1] + e[2] + e[3]
    wgt = jnp.zeros((tm, 128), F32)
    for k in range(TOP_K):
        wgt = wgt + jnp.where(lane == k, e[k] / denom, 0.0)

    multi = (hots[0] | hots[1] | hots[2] | hots[3]).astype(F32)
    r = lax.broadcasted_iota(I32, (tm, tm), 0)
    c = lax.broadcasted_iota(I32, (tm, tm), 1)
    before = _dot((r > c).astype(BF16), multi.astype(BF16)) + carry_ref[...]
    rank = jnp.zeros((tm, 128), F32)
    for k in range(TOP_K):
        rk = jnp.sum(jnp.where(hots[k], before, 0.0), axis=-1, keepdims=True)
        rank = rank + jnp.where(lane == k, rk, 0.0)
    total = carry_ref[...] + jnp.sum(multi, axis=0, keepdims=True)
    carry_ref[...] = total
    code = idx_out * float(RANK_RADIX) + rank
    for blk in range(tm // 128):
        code_ref[:, blk * 128:(blk + 1) * 128] = code[blk * 128:(blk + 1) * 128, :].T[0:8, :].astype(I32)
    wgt_ref[...] = wgt
    cnt_ref[...] = total.astype(I32)


def _row_copy(src_ref, src_row, dst_ref, dst_row, sem):
    return pltpu.make_async_copy(src_ref.at[pl.ds(src_row, 1), :], dst_ref.at[pl.ds(dst_row, 1), :], sem)


def _rows_copy(src_ref, dst_ref, n_rows, sem):
    return pltpu.make_async_copy(src_ref.at[pl.ds(0, n_rows), :], dst_ref.at[pl.ds(0, n_rows), :], sem)


def _dispatch_kernel(pad_lo_ref, pad_hi_ref, pos_ref, x_ref, xs_ref, zero_ref, sem, zsem):
    tm = x_ref.shape[0]

    def issue(j, carry):
        for u in range(ROWS_PER_ISSUE):
            r = j * ROWS_PER_ISSUE + u
            for k in range(TOP_K):
                _row_copy(x_ref, r, xs_ref, pos_ref[k * tm + r], sem).start(priority=k % 2)
        return carry

    lax.fori_loop(0, tm // ROWS_PER_ISSUE, issue, 0)

    @pl.when(pl.program_id(0) == pl.num_programs(0) - 1)
    def _():
        zero_ref[...] = jnp.zeros_like(zero_ref)
        for e in range(N_EXPERTS):
            def pad(p, carry):
                _row_copy(zero_ref, 0, xs_ref, p, zsem).start()
                return carry
            lax.fori_loop(pad_lo_ref[e], pad_hi_ref[e], pad, 0)
        for e in range(N_EXPERTS):
            def pad_wait(p, carry):
                _row_copy(zero_ref, 0, xs_ref, 0, zsem).wait()
                return carry
            lax.fori_loop(pad_lo_ref[e], pad_hi_ref[e], pad_wait, 0)

    for k in range(TOP_K):
        _rows_copy(x_ref, xs_ref, tm, sem).wait()


def _dispatch(x1, pos_flat, pad_lo, pad_hi, n_rows):
    N, D = x1.shape
    tm = TM_DISPATCH
    return pl.pallas_call(
        _dispatch_kernel,
        grid_spec=pltpu.PrefetchScalarGridSpec(
            num_scalar_prefetch=2,
            grid=(N // tm,),
            in_specs=[pl.BlockSpec((tm * TOP_K,), lambda i, lo, hi: (i,), memory_space=pltpu.SMEM),
                      pl.BlockSpec((tm, D), lambda i, lo, hi: (i, 0))],
            out_specs=pl.BlockSpec(memory_space=pl.ANY),
            scratch_shapes=[pltpu.VMEM((8, D), F32), pltpu.SemaphoreType.DMA, pltpu.SemaphoreType.DMA]),
        out_shape=jax.ShapeDtypeStruct((n_rows, D), F32),
        compiler_params=pltpu.CompilerParams(dimension_semantics=("arbitrary",), vmem_limit_bytes=VMEM_LIMIT),
        name="dispatch",
    )(pad_lo, pad_hi, pos_flat, x1)


def _experts_kernel(te_ref, nv_ref, xs_ref, g_ref, wg_ref, bg_ref, wu_ref, bu_ref, wd_ref, bd_ref, ys_ref,
                    wg_s, wu_s, wd_s):
    i = pl.program_id(0)

    @pl.when((i == 0) | (te_ref[i] != te_ref[jnp.maximum(i - 1, 0)]))
    def _():
        wg_s[...] = wg_ref[...].astype(BF16)
        wu_s[...] = wu_ref[...].astype(BF16)
        wd_s[...] = wd_ref[...].astype(BF16)

    @pl.when(i < nv_ref[0])
    def _():
        h = (_rms(xs_ref[...]) * g_ref[...]).astype(BF16)
        gate = jnp.minimum(_dot(h, wg_s[...]) + bg_ref[...], SWIGLU_LIMIT)
        up = jnp.clip(_dot(h, wu_s[...]) + bu_ref[...], -SWIGLU_LIMIT, SWIGLU_LIMIT)
        act = (up + 1.0) * gate * jax.nn.sigmoid(SWIGLU_ALPHA * gate)
        ys_ref[...] = _dot(act.astype(BF16), wd_s[...]) + bd_ref[...]

    @pl.when(i >= nv_ref[0])
    def _():
        ys_ref[...] = jnp.zeros_like(ys_ref)


def _experts(xs, tile_expert, n_valid, g_mlp, w_gate, b_gate, w_up, b_up, w_down, b_down):
    P, D = xs.shape
    tm = TM_EXPERT
    E, _, F = w_gate.shape
    w_spec = pl.BlockSpec((None, D, F), lambda i, te, nv: (te[i], 0, 0))
    wd_spec = pl.BlockSpec((None, F, D), lambda i, te, nv: (te[i], 0, 0))
    b_spec = pl.BlockSpec((None, 1, F), lambda i, te, nv: (te[i], 0, 0))
    bd_spec = pl.BlockSpec((None, 1, D), lambda i, te, nv: (te[i], 0, 0))
    x_spec = pl.BlockSpec((tm, D), lambda i, te, nv: (jnp.minimum(i, nv[0] - 1), 0))
    return pl.pallas_call(
        _experts_kernel,
        grid_spec=pltpu.PrefetchScalarGridSpec(
            num_scalar_prefetch=2,
            grid=(P // tm,),
            in_specs=[x_spec, pl.BlockSpec((1, D), lambda i, te, nv: (0, 0)),
                      w_spec, b_spec, w_spec, b_spec, wd_spec, bd_spec],
            out_specs=pl.BlockSpec((tm, D), lambda i, te, nv: (i, 0)),
            scratch_shapes=[pltpu.VMEM((D, F), BF16), pltpu.VMEM((D, F), BF16), pltpu.VMEM((F, D), BF16)]),
        out_shape=jax.ShapeDtypeStruct((P, D), F32),
        compiler_params=pltpu.CompilerParams(dimension_semantics=("arbitrary",), vmem_limit_bytes=VMEM_LIMIT),
        name="experts",
    )(tile_expert, n_valid, xs, g_mlp.reshape(1, D),
      w_gate, b_gate.reshape(E, 1, F), w_up, b_up.reshape(E, 1, F), w_down, b_down.reshape(E, 1, D))


def _combine_kernel(pos_ref, x_ref, wgt_ref, ys_ref, o_ref, buf_ref, sem):
    tm = x_ref.shape[0]

    def issue(j, carry):
        for u in range(ROWS_PER_ISSUE):
            r = j * ROWS_PER_ISSUE + u
            for k in range(TOP_K):
                _row_copy(ys_ref, pos_ref[k * tm + r], buf_ref.at[k], r, sem).start(priority=k % 2)
        return carry

    lax.fori_loop(0, tm // ROWS_PER_ISSUE, issue, 0)
    for k in range(TOP_K):
        _rows_copy(ys_ref, buf_ref.at[k], tm, sem).wait()
    wgt = wgt_ref[...]
    acc = x_ref[...]
    for k in range(TOP_K):
        acc = acc + wgt[:, k:k + 1] * buf_ref[k]
    o_ref[...] = acc


def _combine(x1, wgt, pos_flat, ys):
    N, D = x1.shape
    tm = TM_COMBINE
    return pl.pallas_call(
        _combine_kernel,
        grid=(N // tm,),
        in_specs=[pl.BlockSpec((tm * TOP_K,), lambda i: (i,), memory_space=pltpu.SMEM),
                  pl.BlockSpec((tm, D), lambda i: (i, 0)),
                  pl.BlockSpec((tm, 128), lambda i: (i, 0)),
                  pl.BlockSpec(memory_space=pl.ANY)],
        out_specs=pl.BlockSpec((tm, D), lambda i: (i, 0)),
        out_shape=jax.ShapeDtypeStruct((N, D), F32),
        scratch_shapes=[pltpu.VMEM((TOP_K, tm, D), F32), pltpu.SemaphoreType.DMA],
        compiler_params=pltpu.CompilerParams(dimension_semantics=("arbitrary",), vmem_limit_bytes=VMEM_LIMIT),
        name="combine",
    )(pos_flat, x1, wgt, ys)


def _moe(x1, code, wgt, cnt, g_mlp, w_gate, b_gate, w_up, b_up, w_down, b_down):
    N, D = x1.shape
    tm = TM_EXPERT
    n_tiles = -(-(N * TOP_K + N_EXPERTS * (tm - 1)) // tm)
    counts = cnt[0, :N_EXPERTS]
    padded = ((counts + tm - 1) // tm) * tm
    ends = jnp.cumsum(padded)
    starts = ends - padded
    code = code[:TOP_K]
    expert = code // RANK_RADIX
    start_of = jnp.sum(jnp.where(expert[..., None] == jnp.arange(N_EXPERTS, dtype=I32), starts.astype(I32), 0), axis=-1)
    pos = (start_of + code % RANK_RADIX).astype(I32)

    def tiled(tile):
        return pos.reshape(TOP_K, N // tile, tile).transpose(1, 0, 2).reshape(-1)

    tile_start = jnp.arange(n_tiles, dtype=I32) * tm
    tile_expert = jnp.minimum(jnp.sum(ends[None, :] <= tile_start[:, None], axis=1), N_EXPERTS - 1).astype(I32)
    n_valid = (ends[-1] // tm).astype(I32).reshape(1)
    xs = _dispatch(x1, tiled(TM_DISPATCH), (starts + counts).astype(I32), ends.astype(I32), n_tiles * tm)
    ys = _experts(xs, tile_expert, n_valid, g_mlp, w_gate, b_gate, w_up, b_up, w_down, b_down)
    return _combine(x1, wgt, tiled(TM_COMBINE), ys)


def _layer(x, g_attn, w_in, qn_nsa, kn_cmp, kn_sel, kn_win,
           pos_cmp_k, w_ck1, b_ck1, w_ck2, b_ck2, pos_cmp_v, w_cv1, b_cv1, w_cv2, b_cv2,
           b_nsa_gate, qn_fox, kn_fox, b_forget, w_up_nsa, w_up_fox, b_merge, w_out, g_mlp,
           w_router, b_router, w_e_gate, b_e_gate, w_e_up, b_e_up, w_e_down, b_e_down):
    B, T, D = x.shape
    qn, ks, vs, kw, vw, kc_raw, vc_raw, qf, kf, vf, gates = _in_proj(
        x, g_attn, w_in, qn_nsa, kn_sel, kn_win, b_nsa_gate, qn_fox, kn_fox, b_forget)
    kc, vc = _compress(kc_raw, vc_raw, pos_cmp_k, w_ck1, b_ck1, w_ck2, b_ck2,
                       pos_cmp_v, w_cv1, b_cv1, w_cv2, b_cv2, kn_cmp)
    o_nsa = _nsa(qn, ks, vs, kw, vw, kc, vc, gates)
    o_fox = _fox(qf, kf, vf)
    hw = NSA_HEADS * HEAD_PAD
    w_merge = w_in[:, w_in.shape[1] - 2 * D:]
    x1, code, wgt, cnt = _merge_out(x.reshape(B * T, D), g_attn, o_nsa.reshape(B * T, hw), o_fox.reshape(B * T, hw),
                                    w_merge, b_merge, w_up_nsa, w_up_fox, w_out, g_mlp, w_router, b_router)
    out = _moe(x1, code, wgt, cnt, g_mlp, w_e_gate, b_e_gate, w_e_up, b_e_up, w_e_down, b_e_down)
    return out.reshape(B, T, D)


def kernel(x, g_attn, w_in, qn_nsa, kn_cmp, kn_sel, kn_win, pos_cmp_k, w_ck1, b_ck1, w_ck2, b_ck2, pos_cmp_v, w_cv1, b_cv1, w_cv2, b_cv2, b_nsa_gate, qn_fox, kn_fox, b_forget, w_up_nsa, w_up_fox, b_merge, w_out, g_mlp, w_router, b_router, w_e_gate, b_e_gate, w_e_up, b_e_up, w_e_down, b_e_down):
    params = (g_attn, w_in, qn_nsa, kn_cmp, kn_sel, kn_win, pos_cmp_k, w_ck1, b_ck1, w_ck2, b_ck2,
              pos_cmp_v, w_cv1, b_cv1, w_cv2, b_cv2, b_nsa_gate, qn_fox, kn_fox, b_forget,
              w_up_nsa, w_up_fox, b_merge, w_out, g_mlp, w_router, b_router,
              w_e_gate, b_e_gate, w_e_up, b_e_up, w_e_down, b_e_down)
    for layer in range(g_attn.shape[0]):
        x = _layer(x, *[p[layer] for p in params])
    return x
```

```python
import functools

import jax
import jax.numpy as jnp
from jax import lax
from jax.experimental import pallas as pl
from jax.experimental.pallas import tpu as pltpu

F32 = jnp.float32
BF16 = jnp.bfloat16
I32 = jnp.int32

D_MODEL = 1024
HEAD_DIM = 64
HEAD_PAD = 128
NSA_HEADS = 8
NSA_GROUPS = 2
NSA_HPG = NSA_HEADS // NSA_GROUPS
FOX_HEADS = 8
CMP_BLOCK = 32
CMP_STRIDE = 16
CMP_HIDDEN = 256
N_CMP_PAD = 128
SEL_BLOCK = 64
SEL_TOPN = 16
N_SEL_BLOCKS = 32
WINDOW = 512
N_EXPERTS = 32
TOP_K = 4
SWIGLU_LIMIT = 7.0
SWIGLU_ALPHA = 1.702
RMS_EPS = 1e-6
NEG_BIG = -1e30
FORCED_SCORE = 1e9
QK_SCALE = HEAD_DIM ** -0.5

LANE_SEL = 64
LANE_POS = 96
LANE_FOX = 64

OFF_QN = 0
OFF_KS = OFF_QN + NSA_HEADS * HEAD_PAD
OFF_VS = OFF_KS + NSA_GROUPS * HEAD_PAD
OFF_KW = OFF_VS + NSA_GROUPS * HEAD_PAD
OFF_VW = OFF_KW + NSA_GROUPS * HEAD_PAD
OFF_KC = OFF_VW + NSA_GROUPS * HEAD_PAD
OFF_VC = OFF_KC + NSA_GROUPS * HEAD_DIM
OFF_QF = OFF_VC + NSA_GROUPS * HEAD_DIM
OFF_KF = OFF_QF + FOX_HEADS * HEAD_DIM
OFF_VF = OFF_KF + FOX_HEADS * HEAD_DIM
OFF_MISC = OFF_VF + FOX_HEADS * HEAD_DIM
W1_COLS = OFF_MISC + 128
MISC_GATES = 3 * NSA_HEADS

TM_PROJ = 256
TQ_NSA = 128
TQ_SELECT = 512
TILES_PER_REGION = 4
SELECTS_PER_REGION = 2
CK_ATTN = 256
TM_OUT = 512
TM_EXPERT = 512
TM_DISPATCH = 2048
TM_COMBINE = 1024
ROWS_PER_ISSUE = 8
RANK_RADIX = 1 << 16
VMEM_LIMIT = 56 * 1024 * 1024


def _dot(a, b):
    return jnp.dot(a, b, preferred_element_type=F32)


def _dot_nt(a, b):
    return lax.dot_general(a, b, (((1,), (1,)), ((), ())), preferred_element_type=F32)


def _split3(v):
    hi = v.astype(BF16)
    r1 = v - hi.astype(F32)
    mid = r1.astype(BF16)
    lo = (r1 - mid.astype(F32)).astype(BF16)
    return hi, mid, lo


def _dot_exact_rhs(a_f32, b_bf16):
    hi, mid, lo = _split3(a_f32)
    return _dot(hi, b_bf16) + _dot(mid, b_bf16) + _dot(lo, b_bf16)


def _dot_exact_lhs(a_bf16, b_f32):
    hi, mid, lo = _split3(b_f32)
    return _dot(a_bf16, hi) + _dot(a_bf16, mid) + _dot(a_bf16, lo)


def _rms(x):
    return x * lax.rsqrt(jnp.mean(x * x, axis=-1, keepdims=True) + RMS_EPS)


def _head_rms(a, gain):
    ms = jnp.sum(a * a, axis=-1, keepdims=True) * (1.0 / HEAD_DIM)
    return a * lax.rsqrt(ms + RMS_EPS) * gain


def _pos_feat_k(pos, lane):
    hi = (256 * (pos >> 8)).astype(F32)
    lo = (pos & 255).astype(F32)
    return jnp.where(lane < LANE_POS + 2, 1.0, jnp.where(lane == LANE_POS + 2, hi, lo)) * (
        (lane >= LANE_POS) & (lane < LANE_POS + 4)).astype(F32)


def _in_proj_kernel(x_ref, g_ref, w_ref, gains_ref, bmisc_ref, pq_ref, pk_ref,
                    qn_ref, ks_ref, vs_ref, kw_ref, vw_ref, kc_ref, vc_ref,
                    qf_ref, kf_ref, vf_ref, gate_ref, carry_ref):
    tm = x_ref.shape[0]
    i = pl.program_id(1)
    h = (_rms(x_ref[...]) * g_ref[...]).astype(BF16)
    pos = i * tm + lax.broadcasted_iota(I32, (tm, 1), 0)
    lane = lax.broadcasted_iota(I32, (1, HEAD_PAD), 1)
    pos_hi = (256 * (pos >> 8)).astype(F32)
    pos_lo = (pos & 255).astype(F32)
    kfeat = _pos_feat_k(pos, lane)
    blk_onehot = ((lane - LANE_SEL) == (pos >> 6)).astype(F32)

    def seg(off, width):
        return _dot(h, w_ref[:, off:off + width])

    a = seg(OFF_QN, NSA_HEADS * HEAD_PAD)
    gq = gains_ref[0:1, :] * QK_SCALE
    for hd in range(NSA_HEADS):
        slope = 2.0 ** (-(hd + 1))
        qfeat = jnp.where(lane == LANE_POS, -slope * pos_hi,
                          jnp.where(lane == LANE_POS + 1, -slope * pos_lo,
                                    jnp.where((lane == LANE_POS + 2) | (lane == LANE_POS + 3), slope, 0.0)))
        sl = slice(hd * HEAD_PAD, (hd + 1) * HEAD_PAD)
        qn_ref[:, sl] = (_head_rms(a[:, sl], gq) + qfeat).astype(BF16)

    a = seg(OFF_KS, NSA_GROUPS * HEAD_PAD)
    for g in range(NSA_GROUPS):
        sl = slice(g * HEAD_PAD, (g + 1) * HEAD_PAD)
        ks_ref[:, sl] = (_head_rms(a[:, sl], gains_ref[1:2, :]) + kfeat + blk_onehot).astype(BF16)
    vs_ref[...] = seg(OFF_VS, NSA_GROUPS * HEAD_PAD).astype(BF16)
    a = seg(OFF_KW, NSA_GROUPS * HEAD_PAD)
    for g in range(NSA_GROUPS):
        sl = slice(g * HEAD_PAD, (g + 1) * HEAD_PAD)
        kw_ref[:, sl] = (_head_rms(a[:, sl], gains_ref[2:3, :]) + kfeat).astype(BF16)
    vw_ref[...] = seg(OFF_VW, NSA_GROUPS * HEAD_PAD).astype(BF16)
    kc_ref[...] = seg(OFF_KC, NSA_GROUPS * HEAD_DIM).astype(BF16)
    vc_ref[...] = seg(OFF_VC, NSA_GROUPS * HEAD_DIM).astype(BF16)

    misc = seg(OFF_MISC, 128) + bmisc_ref[...]
    is_gate = lane < MISC_GATES
    gate_ref[...] = jnp.where(is_gate, jax.nn.sigmoid(misc), 0.0)
    is_f = (lane >= MISC_GATES) & (lane < MISC_GATES + 3 * FOX_HEADS)
    logf = jnp.where(is_f, jax.nn.log_sigmoid(misc), 0.0)

    @pl.when(i == 0)
    def _():
        carry_ref[...] = jnp.zeros_like(carry_ref)

    r = lax.broadcasted_iota(I32, (tm, tm), 0)
    c = lax.broadcasted_iota(I32, (tm, tm), 1)
    tri = (r >= c).astype(BF16)
    csum = _dot_exact_lhs(tri, logf) + carry_ref[...]
    carry_ref[...] = csum[tm - 1:tm, :]
    c_hi, c_mid, c_lo = _split3(csum)
    c3 = jnp.where(lane < MISC_GATES + FOX_HEADS, c_hi,
                   jnp.where(lane < MISC_GATES + 2 * FOX_HEADS, c_mid, c_lo))
    c3 = jnp.where(is_f, c3, jnp.zeros_like(c3))
    low = lane < HEAD_DIM

    def fox_heads(off, gain2, feat, ones_at, out_ref):
        a = seg(off, FOX_HEADS * HEAD_DIM)
        for j in range(FOX_HEADS // 2):
            pair = a[:, j * HEAD_PAD:(j + 1) * HEAD_PAD]
            sq = pair * pair
            ms_even = jnp.sum(jnp.where(low, sq, 0.0), axis=-1, keepdims=True) * (1.0 / HEAD_DIM)
            ms_odd = jnp.sum(jnp.where(low, 0.0, sq), axis=-1, keepdims=True) * (1.0 / HEAD_DIM)
            normed = pair * jnp.where(low, lax.rsqrt(ms_even + RMS_EPS), lax.rsqrt(ms_odd + RMS_EPS)) * gain2
            for parity in range(2):
                hd = 2 * j + parity
                base = _fox_feature_base(hd)
                ones = ((lane >= base + ones_at) & (lane < base + ones_at + 3)).astype(F32)
                keep = low if parity == 0 else jnp.logical_not(low)
                slab = jnp.where(keep, normed, 0.0) + feat[:, hd * HEAD_PAD:(hd + 1) * HEAD_PAD] + ones
                out_ref[:, hd * HEAD_PAD:(hd + 1) * HEAD_PAD] = slab.astype(BF16)

    fox_heads(OFF_QF, gains_ref[5:6, :] * QK_SCALE, _dot(c3, pq_ref[...]), 3, qf_ref)
    fox_heads(OFF_KF, gains_ref[6:7, :], _dot(c3, pk_ref[...]), 0, kf_ref)
    vf_ref[...] = seg(OFF_VF, FOX_HEADS * HEAD_DIM).astype(BF16)


def _pad_heads(w, n_heads):
    lead = w.shape[:-1]
    w = w.reshape(lead + (n_heads, HEAD_DIM))
    w = jnp.pad(w, [(0, 0)] * len(lead) + [(0, 0), (0, HEAD_PAD - HEAD_DIM)])
    return w.reshape(lead + (n_heads * HEAD_PAD,))


def _pad_gain(g):
    return jnp.pad(g.astype(F32), (0, HEAD_PAD - HEAD_DIM))


def _fox_feature_base(hd):
    return HEAD_DIM if hd % 2 == 0 else 0


def _fox_placement():
    pq = jnp.zeros((128, FOX_HEADS * HEAD_PAD), F32)
    pk = jnp.zeros((128, FOX_HEADS * HEAD_PAD), F32)
    for piece in range(3):
        for hd in range(FOX_HEADS):
            src = MISC_GATES + piece * FOX_HEADS + hd
            base = hd * HEAD_PAD + _fox_feature_base(hd)
            pq = pq.at[src, base + piece].set(1.0)
            pk = pk.at[src, base + 3 + piece].set(-1.0)
    return pq.astype(BF16), pk.astype(BF16)


def _in_proj(x, g_attn, w_in, qn_nsa, kn_sel, kn_win, b_nsa_gate, qn_fox, kn_fox, b_forget):
    B, T, D = x.shape
    s1 = NSA_HEADS * HEAD_DIM
    kv_w = NSA_GROUPS * HEAD_DIM
    s2 = s1 + 6 * kv_w
    s3 = s2 + 3 * NSA_HEADS
    fw = FOX_HEADS * HEAD_DIM
    s4 = s3 + 3 * fw
    s5 = s4 + FOX_HEADS
    kv = [w_in[:, s1 + j * kv_w: s1 + (j + 1) * kv_w] for j in range(6)]
    f_cols = w_in[:, s4:s5]
    misc = jnp.concatenate([w_in[:, s2:s3], f_cols, f_cols, f_cols,
                            jnp.zeros((D, 128 - MISC_GATES - 3 * FOX_HEADS), F32)], axis=1)
    w1 = jnp.concatenate([
        _pad_heads(w_in[:, :s1], NSA_HEADS),
        _pad_heads(kv[2], NSA_GROUPS), _pad_heads(kv[3], NSA_GROUPS),
        _pad_heads(kv[4], NSA_GROUPS), _pad_heads(kv[5], NSA_GROUPS),
        kv[0], kv[1],
        w_in[:, s3:s4],
        misc], axis=1).astype(BF16)
    assert w1.shape[1] == W1_COLS
    gains = jnp.stack([_pad_gain(qn_nsa), _pad_gain(kn_sel), _pad_gain(kn_win),
                       _pad_gain(qn_fox), _pad_gain(kn_fox),
                       jnp.tile(qn_fox.astype(F32), 2), jnp.tile(kn_fox.astype(F32), 2), jnp.zeros(HEAD_PAD, F32)])
    bmisc = jnp.concatenate([b_nsa_gate, b_forget, b_forget, b_forget,
                             jnp.zeros(128 - MISC_GATES - 3 * FOX_HEADS, F32)]).reshape(1, 128)
    pq, pk = _fox_placement()
    tm = TM_PROJ

    def tok(width):
        return pl.BlockSpec((None, tm, width), lambda b, i: (b, i, 0))

    def full(shape):
        return pl.BlockSpec(shape, lambda b, i: (0,) * len(shape))

    widths = [NSA_HEADS * HEAD_PAD] + [NSA_GROUPS * HEAD_PAD] * 4 + [NSA_GROUPS * HEAD_DIM] * 2 + \
             [FOX_HEADS * HEAD_PAD] * 2 + [FOX_HEADS * HEAD_DIM]
    out_shape = [jax.ShapeDtypeStruct((B, T, w), BF16) for w in widths] + \
                [jax.ShapeDtypeStruct((B, T, 128), F32)]
    return pl.pallas_call(
        _in_proj_kernel,
        grid=(B, T // tm),
        in_specs=[tok(D), full((1, D)), full((D, W1_COLS)), full((8, HEAD_PAD)), full((1, 128)),
                  full((128, FOX_HEADS * HEAD_PAD)), full((128, FOX_HEADS * HEAD_PAD))],
        out_specs=[tok(w) for w in widths] + [tok(128)],
        out_shape=out_shape,
        scratch_shapes=[pltpu.VMEM((1, 128), F32)],
        compiler_params=pltpu.CompilerParams(
            dimension_semantics=("arbitrary", "arbitrary"), vmem_limit_bytes=VMEM_LIMIT),
        name="in_proj",
    )(x, g_attn.reshape(1, D), w1, gains, bmisc, pq, pk)


def _gelu_tanh(x):
    return 0.5 * x * (1.0 + jnp.tanh(0.7978845608028654 * (x + 0.044715 * (x * x * x))))


def _compress_kernel(ak_ref, av_ref, posk_ref, posv_ref, wk1_ref, bk1_ref, wk2_ref, bk2_ref,
                     wv1_ref, bv1_ref, wv2_ref, bv2_ref, gain_ref, kc_ref, vc_ref):
    row = lax.broadcasted_iota(I32, (N_CMP_PAD, 1), 0)
    lane = lax.broadcasted_iota(I32, (1, HEAD_PAD), 1)
    kfeat = _pos_feat_k(CMP_STRIDE * row + CMP_BLOCK - 1, lane)

    def mlp(a_ref, pos_ref, w1_ref, b1_ref, w2_ref, b2_ref):
        a = a_ref[...].astype(F32)
        lo = (a + pos_ref[0:1, :]).astype(BF16)
        hi = (a + pos_ref[1:2, :]).astype(BF16)
        outs = []
        for g in range(NSA_GROUPS):
            hid = _dot(lo, w1_ref[0, g]) + pltpu.roll(_dot(hi, w1_ref[1, g]), N_CMP_PAD - 1, 0) + b1_ref[...]
            outs.append(_dot(_gelu_tanh(hid).astype(BF16), w2_ref[...]) + b2_ref[...])
        return outs

    ks = mlp(ak_ref, posk_ref, wk1_ref, bk1_ref, wk2_ref, bk2_ref)
    vs = mlp(av_ref, posv_ref, wv1_ref, bv1_ref, wv2_ref, bv2_ref)
    for g in range(NSA_GROUPS):
        kc_ref[g] = (_head_rms(ks[g], gain_ref[...]) + kfeat).astype(BF16)
        vc_ref[g] = vs[g].astype(BF16)


def _compress(kc_raw, vc_raw, pos_k, w_ck1, b_ck1, w_ck2, b_ck2, pos_v, w_cv1, b_cv1, w_cv2, b_cv2, kn_cmp):
    B, T, _ = kc_raw.shape
    n_chunks = T // CMP_STRIDE
    assert n_chunks == N_CMP_PAD
    chunk_w = CMP_STRIDE * NSA_GROUPS * HEAD_DIM

    def chunks(a):
        return a.reshape(B, n_chunks, chunk_w)

    def w1_rows(w):
        w = w.reshape(2, CMP_STRIDE, 1, HEAD_DIM, CMP_HIDDEN)
        per_group = [jnp.concatenate([w if gg == g else jnp.zeros_like(w) for gg in range(NSA_GROUPS)], axis=2)
                     for g in range(NSA_GROUPS)]
        return jnp.stack(per_group, axis=1).reshape(2, NSA_GROUPS, chunk_w, CMP_HIDDEN).astype(BF16)

    def pos_rows(p):
        p = jnp.broadcast_to(p.reshape(2, CMP_STRIDE, 1, HEAD_DIM), (2, CMP_STRIDE, NSA_GROUPS, HEAD_DIM))
        return p.reshape(2, chunk_w)

    def w2pad(w, b):
        return (jnp.pad(w, ((0, 0), (0, HEAD_PAD - HEAD_DIM))).astype(BF16),
                jnp.pad(b, (0, HEAD_PAD - HEAD_DIM)).reshape(1, HEAD_PAD))

    wk2, bk2 = w2pad(w_ck2, b_ck2)
    wv2, bv2 = w2pad(w_cv2, b_cv2)
    a_spec = pl.BlockSpec((None, n_chunks, chunk_w), lambda b: (b, 0, 0))
    o_spec = pl.BlockSpec((None, NSA_GROUPS, N_CMP_PAD, HEAD_PAD), lambda b: (b, 0, 0, 0))

    def full(shape):
        return pl.BlockSpec(shape, lambda b: (0,) * len(shape))

    w1_shape = (2, NSA_GROUPS, chunk_w, CMP_HIDDEN)
    return pl.pallas_call(
        _compress_kernel,
        grid=(B,),
        in_specs=[a_spec, a_spec, full((2, chunk_w)), full((2, chunk_w)),
                  full(w1_shape), full((1, CMP_HIDDEN)), full((CMP_HIDDEN, HEAD_PAD)), full((1, HEAD_PAD)),
                  full(w1_shape), full((1, CMP_HIDDEN)), full((CMP_HIDDEN, HEAD_PAD)), full((1, HEAD_PAD)),
                  full((1, HEAD_PAD))],
        out_specs=[o_spec, o_spec],
        out_shape=[jax.ShapeDtypeStruct((B, NSA_GROUPS, N_CMP_PAD, HEAD_PAD), BF16)] * 2,
        compiler_params=pltpu.CompilerParams(dimension_semantics=("arbitrary",), vmem_limit_bytes=VMEM_LIMIT),
        name="compress",
    )(chunks(kc_raw), chunks(vc_raw), pos_rows(pos_k), pos_rows(pos_v),
      w1_rows(w_ck1), b_ck1.reshape(1, -1), wk2, bk2,
      w1_rows(w_cv1), b_cv1.reshape(1, -1), wv2, bv2, _pad_gain(kn_cmp).reshape(1, HEAD_PAD))


def _lane_blocks(a):
    return [a[:, i * 128:(i + 1) * 128] for i in range(a.shape[1] // 128)]


def _fold_blocks(op, acc, a):
    for blk in _lane_blocks(a):
        acc = blk if acc is None else op(acc, blk)
    return acc


def _nsa_kernel(q_ref, ks_ref, vs_ref, kw_ref, vw_ref, kc_ref, vct_ref, gate_ref, ovl_ref, o_ref,
                sel_ref, oc_ref, s_ref):
    seq = q_ref.shape[0]
    tq = TQ_NSA
    ta = TQ_SELECT
    ck = CK_ATTN
    span = WINDOW + tq
    g = pl.program_id(1)
    col_l = lax.broadcasted_iota(I32, (1, 128), 1)
    c_end_sub = CMP_STRIDE * lax.broadcasted_iota(I32, (N_CMP_PAD, 1), 0) + (CMP_BLOCK - 1)
    blk_id = lax.broadcasted_iota(I32, (N_SEL_BLOCKS, 1), 0)
    sub_id = lax.broadcasted_iota(I32, (8, 1), 0)

    def gate_col(gates, branch, hh):
        lo = branch * NSA_HEADS + hh
        return jnp.where(g == 0, gates[:, lo:lo + 1], gates[:, lo + NSA_HPG:lo + NSA_HPG + 1])

    def select(a):
        a0 = a * ta
        t_lane = a0 + lax.broadcasted_iota(I32, (1, ta), 1)
        gates = gate_ref[a0:a0 + ta, :]

        mask_t = t_lane >= c_end_sub
        p_sum = None
        for hh in range(NSA_HPG):
            q_h = q_ref[a0:a0 + ta, hh * HEAD_PAD:(hh + 1) * HEAD_PAD]
            s_t = jnp.where(mask_t, _dot_nt(kc_ref[...], q_h), NEG_BIG)
            e_t = jnp.where(mask_t, jnp.exp(s_t - jnp.max(s_t, axis=0, keepdims=True)), 0.0)
            l_t = jnp.sum(e_t, axis=0, keepdims=True)
            p_t = e_t / jnp.where(l_t > 0.0, l_t, 1.0)
            o_t = _dot(vct_ref[...], p_t.astype(BF16))
            g_c = gate_col(gates, 0, hh)
            for blk in range(ta // 128):
                bs = slice(blk * 128, (blk + 1) * 128)
                oc_ref[a0 + blk * 128:a0 + (blk + 1) * 128, hh * HEAD_PAD:(hh + 1) * HEAD_PAD] = g_c[bs] * o_t[:, bs].T
            p_sum = p_t if p_sum is None else p_sum + p_t
        imp = _dot_exact_lhs(ovl_ref[...], p_sum)
        tb = t_lane >> 6
        valid = blk_id <= tb
        forced = (blk_id == 0) | (blk_id == tb) | (blk_id == tb - 1)
        score = jnp.where(forced & valid, FORCED_SCORE, imp)
        score = jnp.where(valid, score, -1.0)
        groups = [score[8 * b:8 * b + 8, :] for b in range(N_SEL_BLOCKS // 8)]
        n_above = [jnp.zeros((8, ta), F32) for _ in groups]
        for j in range(N_SEL_BLOCKS):
            cj = score[j:j + 1, :]
            for b, grp in enumerate(groups):
                if 8 * b > j:
                    n_above[b] = n_above[b] + jnp.where(cj >= grp, 1.0, 0.0)
                elif 8 * b + 7 < j:
                    n_above[b] = n_above[b] + jnp.where(cj > grp, 1.0, 0.0)
                else:
                    tie = jnp.where(sub_id + 8 * b > j, 1.0, 0.0)
                    n_above[b] = n_above[b] + jnp.where(cj > grp, 1.0, 0.0) + jnp.where(cj == grp, tie, 0.0)
        dropped = jnp.where(jnp.concatenate(n_above, axis=0) >= float(SEL_TOPN), NEG_BIG, 0.0)
        drop_t = jnp.concatenate([jnp.zeros((LANE_SEL, ta), F32), dropped,
                                  jnp.zeros((HEAD_PAD - LANE_SEL - N_SEL_BLOCKS, ta), F32)], axis=0)
        for blk in range(ta // 128):
            sel_ref[a0 + blk * 128:a0 + (blk + 1) * 128, :] = drop_t[:, blk * 128:(blk + 1) * 128].T.astype(BF16)

    for first in range(0, seq // ta, SELECTS_PER_REGION):
        @pl.when(g >= -first)
        def _(first=first):
            for a in range(first, first + SELECTS_PER_REGION):
                select(a)

    rows = NSA_HPG * tq
    row_l = lax.broadcasted_iota(I32, (rows, 1), 0) & (tq - 1)

    def attend(i):
        t0 = i * tq
        t_row = t0 + row_l
        q4 = jnp.concatenate([q_ref[t0:t0 + tq, hh * HEAD_PAD:(hh + 1) * HEAD_PAD] for hh in range(NSA_HPG)], axis=0)
        q4s = q4 + jnp.concatenate([sel_ref[t0:t0 + tq, :]] * NSA_HPG, axis=0)

        n_chunks = (t0 + tq - 1) // ck + 1
        slot = i % 2
        mx = None
        for c in range(n_chunks):
            s = _dot_nt(q4s, ks_ref[c * ck:(c + 1) * ck, :])
            if (c + 1) * ck > t0:
                s = jnp.where(t_row >= c * ck + lax.broadcasted_iota(I32, (1, ck), 1), s, NEG_BIG)
            s_ref[slot, :, c * ck:(c + 1) * ck] = s
            mx = _fold_blocks(jnp.maximum, mx, s)
        m = jnp.max(mx, axis=-1, keepdims=True)
        ls = None
        acc = None
        for c in range(n_chunks):
            p = jnp.exp(s_ref[slot, :, c * ck:(c + 1) * ck] - m)
            ls = _fold_blocks(jnp.add, ls, p)
            pv = _dot(p.astype(BF16), vs_ref[c * ck:(c + 1) * ck, :])
            acc = pv if acc is None else acc + pv
        o_s = acc / jnp.sum(ls, axis=-1, keepdims=True)

        ws = max(t0 - WINDOW, 0)
        blocks = _lane_blocks(_dot_nt(q4, kw_ref[ws:ws + span, :]))
        if t0 < WINDOW:
            blocks = [jnp.where(t_row >= ws + jb * 128 + col_l, blk, NEG_BIG) if ws + (jb + 1) * 128 > t0 else blk
                      for jb, blk in enumerate(blocks)]
        else:
            blocks[0] = jnp.where(col_l > row_l, blocks[0], NEG_BIG)
            blocks[-1] = jnp.where(col_l <= row_l, blocks[-1], NEG_BIG)
        s = jnp.concatenate(blocks, axis=1)
        m = jnp.max(_fold_blocks(jnp.maximum, None, s), axis=-1, keepdims=True)
        p = jnp.exp(s - m)
        l = jnp.sum(_fold_blocks(jnp.add, None, p), axis=-1, keepdims=True)
        o_w = _dot(p.astype(BF16), vw_ref[ws:ws + span, :]) / l

        gates = gate_ref[t0:t0 + tq, :]
        outs = []
        for hh in range(NSA_HPG):
            rs = slice(hh * tq, (hh + 1) * tq)
            o = oc_ref[t0:t0 + tq, hh * HEAD_PAD:(hh + 1) * HEAD_PAD]
            outs.append((o + gate_col(gates, 1, hh) * o_s[rs] + gate_col(gates, 2, hh) * o_w[rs]).astype(BF16))
        o_ref[t0:t0 + tq, :] = jnp.concatenate(outs, axis=1)

    for first in range(0, seq // tq, TILES_PER_REGION):
        @pl.when(g > -1 - first)
        def _(first=first):
            for i in range(first, first + TILES_PER_REGION):
                attend(i)


def _overlap_matrix():
    c = jnp.arange(N_CMP_PAD)[None, :]
    j = jnp.arange(N_SEL_BLOCKS)[:, None]
    lo = (CMP_STRIDE * c) // SEL_BLOCK
    hi = (CMP_STRIDE * c + CMP_BLOCK - 1) // SEL_BLOCK
    return ((j == lo) | (j == hi)).astype(BF16)


def _nsa(qn, ks, vs, kw, vw, kc, vc, gates):
    B, T, _ = qn.shape
    gw = NSA_HPG * HEAD_PAD
    q_spec = pl.BlockSpec((None, T, gw), lambda b, g: (b, 0, g))
    kv_spec = pl.BlockSpec((None, T, HEAD_PAD), lambda b, g: (b, 0, g))
    c_spec = pl.BlockSpec((None, None, N_CMP_PAD, HEAD_PAD), lambda b, g: (b, g, 0, 0))
    return pl.pallas_call(
        _nsa_kernel,
        grid=(B, NSA_GROUPS),
        in_specs=[q_spec, kv_spec, kv_spec, kv_spec, kv_spec, c_spec, c_spec,
                  pl.BlockSpec((None, T, 128), lambda b, g: (b, 0, 0)),
                  pl.BlockSpec((N_SEL_BLOCKS, N_CMP_PAD), lambda b, g: (0, 0))],
        out_specs=q_spec,
        out_shape=jax.ShapeDtypeStruct((B, T, NSA_HEADS * HEAD_PAD), BF16),
        scratch_shapes=[pltpu.VMEM((T, HEAD_PAD), BF16), pltpu.VMEM((T, gw), F32),
                        pltpu.VMEM((2, NSA_HPG * TQ_NSA, T), F32)],
        compiler_params=pltpu.CompilerParams(
            dimension_semantics=("arbitrary", "arbitrary"), vmem_limit_bytes=VMEM_LIMIT),
        name="nsa",
    )(qn, ks, vs, kw, vw, kc, vc.swapaxes(2, 3), gates, _overlap_matrix())


def _fox_kernel(q_ref, k_ref, v_ref, o_ref):
    seq = q_ref.shape[0]
    ck = CK_ATTN
    row = lax.broadcasted_iota(I32, (ck, 1), 0)
    col = lax.broadcasted_iota(I32, (1, ck), 1)
    for qi in range(seq // ck):
        q = q_ref[qi * ck:(qi + 1) * ck, :]
        s_chunks = []
        mx = None
        for c in range(qi + 1):
            s = _dot_nt(q, k_ref[c * ck:(c + 1) * ck, :])
            if c == qi:
                s = jnp.where(row >= col, s, NEG_BIG)
            s_chunks.append(s)
            mx = _fold_blocks(jnp.maximum, mx, s)
        m = jnp.max(mx, axis=-1, keepdims=True)
        ls = None
        acc = None
        for c in range(qi + 1):
            p = jnp.exp(s_chunks[c] - m)
            ls = _fold_blocks(jnp.add, ls, p)
            pv = _dot(p.astype(BF16), v_ref[c * ck:(c + 1) * ck, :])
            acc = pv if acc is None else acc + pv
        o_ref[qi * ck:(qi + 1) * ck, :] = (acc / jnp.sum(ls, axis=-1, keepdims=True)).astype(BF16)


def _fox(qf, kf, vf):
    B, T, _ = qf.shape
    spec = pl.BlockSpec((None, T, HEAD_PAD), lambda b, h: (b, 0, h))
    v_spec = pl.BlockSpec((None, T, HEAD_PAD), lambda b, h: (b, 0, h // 2))
    return pl.pallas_call(
        _fox_kernel,
        grid=(B, FOX_HEADS),
        in_specs=[spec, spec, v_spec],
        out_specs=spec,
        out_shape=jax.ShapeDtypeStruct((B, T, FOX_HEADS * HEAD_PAD), BF16),
        compiler_params=pltpu.CompilerParams(
            dimension_semantics=("arbitrary", "arbitrary"), vmem_limit_bytes=VMEM_LIMIT),
        name="fox",
    )(qf, kf, vf)


def _merge_out_kernel(x_ref, g_ref, on_ref, of_ref, wm_ref, bm_ref, wun_ref, wuf_ref, wo_ref,
                      gm_ref, wrh_ref, wrm_ref, br_ref, o_ref, code_ref, wgt_ref, cnt_ref, carry_ref):
    x = x_ref[...]
    h = (_rms(x) * g_ref[...]).astype(BF16)
    merge = jax.nn.sigmoid(_dot(h, wm_ref[...]) + bm_ref[...])
    y = merge[:, :D_MODEL] * _dot(on_ref[...], wun_ref[...]) + merge[:, D_MODEL:] * _dot(of_ref[...], wuf_ref[...])
    x1 = x + _dot(y.astype(BF16), wo_ref[...])
    o_ref[...] = x1
    _route_tile(x1, gm_ref, wrh_ref, wrm_ref, br_ref, code_ref, wgt_ref, cnt_ref, carry_ref)


def _merge_out(x2, g_attn, o_nsa, o_fox, w_merge, b_merge, w_up_nsa, w_up_fox, w_out, g_mlp, w_router, b_router):
    N, D = x2.shape
    tm = TM_OUT
    assert N <= RANK_RADIX
    wr = jnp.pad(w_router, ((0, 0), (0, 128 - N_EXPERTS)))
    wr_hi = wr.astype(BF16)
    wr_mid = (wr - wr_hi.astype(F32)).astype(BF16)
    br = jnp.pad(b_router, (0, 128 - N_EXPERTS)).reshape(1, 128)

    def pad_rows(w, n_heads):
        return _pad_heads(w.T, n_heads).T.astype(BF16)

    def fox_rows(w):
        w = w.reshape(FOX_HEADS, HEAD_DIM, D)
        z = jnp.zeros_like(w)
        even = jnp.concatenate([w, z], axis=1)
        odd = jnp.concatenate([z, w], axis=1)
        parity = (jnp.arange(FOX_HEADS) % 2 == 0)[:, None, None]
        return jnp.where(parity, even, odd).reshape(FOX_HEADS * HEAD_PAD, D).astype(BF16)

    def tok(width):
        return pl.BlockSpec((tm, width), lambda i: (i, 0))

    def full(shape):
        return pl.BlockSpec(shape, lambda i: (0,) * len(shape))

    hw = NSA_HEADS * HEAD_PAD
    return pl.pallas_call(
        _merge_out_kernel,
        grid=(N // tm,),
        in_specs=[tok(D), full((1, D)), tok(hw), tok(hw), full((D, 2 * D)), full((1, 2 * D)),
                  full((hw, D)), full((hw, D)), full((D, D)),
                  full((1, D)), full((D, 128)), full((D, 128)), full((1, 128))],
        out_specs=[tok(D), pl.BlockSpec((8, tm), lambda i: (0, i)), tok(128), full((1, 128))],
        out_shape=[jax.ShapeDtypeStruct((N, D), F32), jax.ShapeDtypeStruct((8, N), I32),
                   jax.ShapeDtypeStruct((N, 128), F32), jax.ShapeDtypeStruct((1, 128), I32)],
        scratch_shapes=[pltpu.VMEM((1, 128), F32)],
        compiler_params=pltpu.CompilerParams(dimension_semantics=("arbitrary",), vmem_limit_bytes=VMEM_LIMIT),
        name="merge_out",
    )(x2, g_attn.reshape(1, D), o_nsa, o_fox, w_merge.astype(BF16), b_merge.reshape(1, 2 * D),
      pad_rows(w_up_nsa, NSA_HEADS), fox_rows(w_up_fox), w_out.astype(BF16),
      g_mlp.reshape(1, D), wr_hi, wr_mid, br)


def _route_tile(x, g_ref, wh_ref, wm_ref, br_ref, code_ref, wgt_ref, cnt_ref, carry_ref):
    tm = x.shape[0]

    @pl.when(pl.program_id(0) == 0)
    def _():
        carry_ref[...] = jnp.zeros_like(carry_ref)

    h = _rms(x) * g_ref[...]
    h_hi = h.astype(BF16)
    h_mid = (h - h_hi.astype(F32)).astype(BF16)
    logits = _dot(h_hi, wh_ref[...]) + _dot(h_hi, wm_ref[...]) + _dot(h_mid, wh_ref[...]) + br_ref[...]
    lane = lax.broadcasted_iota(I32, (1, 128), 1)
    lane_f = lane.astype(F32)
    work = jnp.where(lane < N_EXPERTS, logits, -jnp.inf)
    vals, hots = [], []
    idx_out = jnp.zeros((tm, 128), F32)
    for k in range(TOP_K):
        m = jnp.max(work, axis=-1, keepdims=True)
        idx = jnp.min(jnp.where(work == m, lane_f, 128.0), axis=-1, keepdims=True)
        hot = lane_f == idx
        work = jnp.where(hot, -jnp.inf, work)
        vals.append(m)
        hots.append(hot)
        idx_out = idx_out + jnp.where(lane == k, idx, 0.0)
    e = [jnp.exp(v - vals[0]) for v in vals]
    denom = e[0] + e[1] + e[2] + e[3]
    wgt = jnp.zeros((tm, 128), F32)
    for k in range(TOP_K):
        wgt = wgt + jnp.where(lane == k, e[k] / denom, 0.0)

    multi = (hots[0] | hots[1] | hots[2] | hots[3]).astype(F32)
    r = lax.broadcasted_iota(I32, (tm, tm), 0)
    c = lax.broadcasted_iota(I32, (tm, tm), 1)
    before = _dot((r > c).astype(BF16), multi.astype(BF16)) + carry_ref[...]
    rank = jnp.zeros((tm, 128), F32)
    for k in range(TOP_K):
        rk = jnp.sum(jnp.where(hots[k], before, 0.0), axis=-1, keepdims=True)
        rank = rank + jnp.where(lane == k, rk, 0.0)
    total = carry_ref[...] + jnp.sum(multi, axis=0, keepdims=True)
    carry_ref[...] = total
    code = idx_out * float(RANK_RADIX) + rank
    for blk in range(tm // 128):
        code_ref[:, blk * 128:(blk + 1) * 128] = code[blk * 128:(blk + 1) * 128, :].T[0:8, :].astype(I32)
    wgt_ref[...] = wgt
    cnt_ref[...] = total.astype(I32)


def _row_copy(src_ref, src_row, dst_ref, dst_row, sem):
    return pltpu.make_async_copy(src_ref.at[pl.ds(src_row, 1), :], dst_ref.at[pl.ds(dst_row, 1), :], sem)


def _rows_copy(src_ref, dst_ref, n_rows, sem):
    return pltpu.make_async_copy(src_ref.at[pl.ds(0, n_rows), :], dst_ref.at[pl.ds(0, n_rows), :], sem)


def _dispatch_kernel(pad_lo_ref, pad_hi_ref, pos_ref, x_ref, xs_ref, zero_ref, sem, zsem):
    tm = x_ref.shape[0]

    def issue(j, carry):
        for u in range(ROWS_PER_ISSUE):
            r = j * ROWS_PER_ISSUE + u
            for k in range(TOP_K):
                _row_copy(x_ref, r, xs_ref, pos_ref[k * tm + r], sem).start(priority=k % 2)
        return carry

    lax.fori_loop(0, tm // ROWS_PER_ISSUE, issue, 0)

    @pl.when(pl.program_id(0) == pl.num_programs(0) - 1)
    def _():
        zero_ref[...] = jnp.zeros_like(zero_ref)
        for e in range(N_EXPERTS):
            def pad(p, carry):
                _row_copy(zero_ref, 0, xs_ref, p, zsem).start()
                return carry
            lax.fori_loop(pad_lo_ref[e], pad_hi_ref[e], pad, 0)
        for e in range(N_EXPERTS):
            def pad_wait(p, carry):
                _row_copy(zero_ref, 0, xs_ref, 0, zsem).wait()
                return carry
            lax.fori_loop(pad_lo_ref[e], pad_hi_ref[e], pad_wait, 0)

    for k in range(TOP_K):
        _rows_copy(x_ref, xs_ref, tm, sem).wait()


def _dispatch(x1, pos_flat, pad_lo, pad_hi, n_rows):
    N, D = x1.shape
    tm = TM_DISPATCH
    return pl.pallas_call(
        _dispatch_kernel,
        grid_spec=pltpu.PrefetchScalarGridSpec(
            num_scalar_prefetch=2,
            grid=(N // tm,),
            in_specs=[pl.BlockSpec((tm * TOP_K,), lambda i, lo, hi: (i,), memory_space=pltpu.SMEM),
                      pl.BlockSpec((tm, D), lambda i, lo, hi: (i, 0))],
            out_specs=pl.BlockSpec(memory_space=pl.ANY),
            scratch_shapes=[pltpu.VMEM((8, D), F32), pltpu.SemaphoreType.DMA, pltpu.SemaphoreType.DMA]),
        out_shape=jax.ShapeDtypeStruct((n_rows, D), F32),
        compiler_params=pltpu.CompilerParams(dimension_semantics=("arbitrary",), vmem_limit_bytes=VMEM_LIMIT),
        name="dispatch",
    )(pad_lo, pad_hi, pos_flat, x1)


def _experts_kernel(te_ref, nv_ref, xs_ref, g_ref, wg_ref, bg_ref, wu_ref, bu_ref, wd_ref, bd_ref, ys_ref,
                    wg_s, wu_s, wd_s):
    i = pl.program_id(0)

    @pl.when((i == 0) | (te_ref[i] != te_ref[jnp.maximum(i - 1, 0)]))
    def _():
        wg_s[...] = wg_ref[...].astype(BF16)
        wu_s[...] = wu_ref[...].astype(BF16)
        wd_s[...] = wd_ref[...].astype(BF16)

    @pl.when(i < nv_ref[0])
    def _():
        h = (_rms(xs_ref[...]) * g_ref[...]).astype(BF16)
        y = bd_ref[...]
        half = wg_s.shape[1] // 2
        for lo in (0, half):
            cols = slice(lo, lo + half)
            gate = jnp.minimum(_dot(h, wg_s[:, cols]) + bg_ref[:, cols], SWIGLU_LIMIT)
            up = jnp.clip(_dot(h, wu_s[:, cols]) + bu_ref[:, cols], -SWIGLU_LIMIT, SWIGLU_LIMIT)
            act = (up + 1.0) * gate * jax.nn.sigmoid(SWIGLU_ALPHA * gate)
            y = y + _dot(act.astype(BF16), wd_s[cols, :])
        ys_ref[...] = y

    @pl.when(i >= nv_ref[0])
    def _():
        ys_ref[...] = jnp.zeros_like(ys_ref)


def _experts(xs, tile_expert, n_valid, g_mlp, w_gate, b_gate, w_up, b_up, w_down, b_down):
    P, D = xs.shape
    tm = TM_EXPERT
    E, _, F = w_gate.shape
    w_spec = pl.BlockSpec((None, D, F), lambda i, te, nv: (te[i], 0, 0))
    wd_spec = pl.BlockSpec((None, F, D), lambda i, te, nv: (te[i], 0, 0))
    b_spec = pl.BlockSpec((None, 1, F), lambda i, te, nv: (te[i], 0, 0))
    bd_spec = pl.BlockSpec((None, 1, D), lambda i, te, nv: (te[i], 0, 0))
    x_spec = pl.BlockSpec((tm, D), lambda i, te, nv: (jnp.minimum(i, nv[0] - 1), 0))
    return pl.pallas_call(
        _experts_kernel,
        grid_spec=pltpu.PrefetchScalarGridSpec(
            num_scalar_prefetch=2,
            grid=(P // tm,),
            in_specs=[x_spec, pl.BlockSpec((1, D), lambda i, te, nv: (0, 0)),
                      w_spec, b_spec, w_spec, b_spec, wd_spec, bd_spec],
            out_specs=pl.BlockSpec((tm, D), lambda i, te, nv: (i, 0)),
            scratch_shapes=[pltpu.VMEM((D, F), BF16), pltpu.VMEM((D, F), BF16), pltpu.VMEM((F, D), BF16)]),
        out_shape=jax.ShapeDtypeStruct((P, D), F32),
        compiler_params=pltpu.CompilerParams(dimension_semantics=("arbitrary",), vmem_limit_bytes=VMEM_LIMIT),
        name="experts",
    )(tile_expert, n_valid, xs, g_mlp.reshape(1, D),
      w_gate, b_gate.reshape(E, 1, F), w_up, b_up.reshape(E, 1, F), w_down, b_down.reshape(E, 1, D))


def _combine_kernel(pos_ref, x_ref, wgt_ref, ys_ref, o_ref, buf_ref, sem):
    tm = x_ref.shape[0]

    def issue(j, carry):
        for u in range(ROWS_PER_ISSUE):
            r = j * ROWS_PER_ISSUE + u
            for k in range(TOP_K):
                _row_copy(ys_ref, pos_ref[k * tm + r], buf_ref.at[k], r, sem).start(priority=k % 2)
        return carry

    lax.fori_loop(0, tm // ROWS_PER_ISSUE, issue, 0)
    for k in range(TOP_K):
        _rows_copy(ys_ref, buf_ref.at[k], tm, sem).wait()
    wgt = wgt_ref[...]
    acc = x_ref[...]
    for k in range(TOP_K):
        acc = acc + wgt[:, k:k + 1] * buf_ref[k]
    o_ref[...] = acc


def _combine(x1, wgt, pos_flat, ys):
    N, D = x1.shape
    tm = TM_COMBINE
    return pl.pallas_call(
        _combine_kernel,
        grid=(N // tm,),
        in_specs=[pl.BlockSpec((tm * TOP_K,), lambda i: (i,), memory_space=pltpu.SMEM),
                  pl.BlockSpec((tm, D), lambda i: (i, 0)),
                  pl.BlockSpec((tm, 128), lambda i: (i, 0)),
                  pl.BlockSpec(memory_space=pl.ANY)],
        out_specs=pl.BlockSpec((tm, D), lambda i: (i, 0)),
        out_shape=jax.ShapeDtypeStruct((N, D), F32),
        scratch_shapes=[pltpu.VMEM((TOP_K, tm, D), F32), pltpu.SemaphoreType.DMA],
        compiler_params=pltpu.CompilerParams(dimension_semantics=("arbitrary",), vmem_limit_bytes=VMEM_LIMIT),
        name="combine",
    )(pos_flat, x1, wgt, ys)


def _moe(x1, code, wgt, cnt, g_mlp, w_gate, b_gate, w_up, b_up, w_down, b_down):
    N, D = x1.shape
    tm = TM_EXPERT
    n_tiles = -(-(N * TOP_K + N_EXPERTS * (tm - 1)) // tm)
    counts = cnt[0, :N_EXPERTS]
    padded = ((counts + tm - 1) // tm) * tm
    ends = jnp.cumsum(padded)
    starts = ends - padded
    code = code[:TOP_K]
    expert = code // RANK_RADIX
    start_of = jnp.sum(jnp.where(expert[..., None] == jnp.arange(N_EXPERTS, dtype=I32), starts.astype(I32), 0), axis=-1)
    pos = (start_of + code % RANK_RADIX).astype(I32)

    def tiled(tile):
        return pos.reshape(TOP_K, N // tile, tile).transpose(1, 0, 2).reshape(-1)

    tile_start = jnp.arange(n_tiles, dtype=I32) * tm
    tile_expert = jnp.minimum(jnp.sum(ends[None, :] <= tile_start[:, None], axis=1), N_EXPERTS - 1).astype(I32)
    n_valid = (ends[-1] // tm).astype(I32).reshape(1)
    xs = _dispatch(x1, tiled(TM_DISPATCH), (starts + counts).astype(I32), ends.astype(I32), n_tiles * tm)
    ys = _experts(xs, tile_expert, n_valid, g_mlp, w_gate, b_gate, w_up, b_up, w_down, b_down)
    return _combine(x1, wgt, tiled(TM_COMBINE), ys)


def _layer(x, g_attn, w_in, qn_nsa, kn_cmp, kn_sel, kn_win,
           pos_cmp_k, w_ck1, b_ck1, w_ck2, b_ck2, pos_cmp_v, w_cv1, b_cv1, w_cv2, b_cv2,
           b_nsa_gate, qn_fox, kn_fox, b_forget, w_up_nsa, w_up_fox, b_merge, w_out, g_mlp,
           w_router, b_router, w_e_gate, b_e_gate, w_e_up, b_e_up, w_e_down, b_e_down):
    B, T, D = x.shape
    qn, ks, vs, kw, vw, kc_raw, vc_raw, qf, kf, vf, gates = _in_proj(
        x, g_attn, w_in, qn_nsa, kn_sel, kn_win, b_nsa_gate, qn_fox, kn_fox, b_forget)
    kc, vc = _compress(kc_raw, vc_raw, pos_cmp_k, w_ck1, b_ck1, w_ck2, b_ck2,
                       pos_cmp_v, w_cv1, b_cv1, w_cv2, b_cv2, kn_cmp)
    o_nsa = _nsa(qn, ks, vs, kw, vw, kc, vc, gates)
    o_fox = _fox(qf, kf, vf)
    hw = NSA_HEADS * HEAD_PAD
    w_merge = w_in[:, w_in.shape[1] - 2 * D:]
    x1, code, wgt, cnt = _merge_out(x.reshape(B * T, D), g_attn, o_nsa.reshape(B * T, hw), o_fox.reshape(B * T, hw),
                                    w_merge, b_merge, w_up_nsa, w_up_fox, w_out, g_mlp, w_router, b_router)
    out = _moe(x1, code, wgt, cnt, g_mlp, w_e_gate, b_e_gate, w_e_up, b_e_up, w_e_down, b_e_down)
    return out.reshape(B, T, D)


def kernel(x, g_attn, w_in, qn_nsa, kn_cmp, kn_sel, kn_win, pos_cmp_k, w_ck1, b_ck1, w_ck2, b_ck2, pos_cmp_v, w_cv1, b_cv1, w_cv2, b_cv2, b_nsa_gate, qn_fox, kn_fox, b_forget, w_up_nsa, w_up_fox, b_merge, w_out, g_mlp, w_router, b_router, w_e_gate, b_e_gate, w_e_up, b_e_up, w_e_down, b_e_down):
    params = (g_attn, w_in, qn_nsa, kn_cmp, kn_sel, kn_win, pos_cmp_k, w_ck1, b_ck1, w_ck2, b_ck2,
              pos_cmp_v, w_cv1, b_cv1, w_cv2, b_cv2, b_nsa_gate, qn_fox, kn_fox, b_forget,
              w_up_nsa, w_up_fox, b_merge, w_out, g_mlp, w_router, b_router,
              w_e_gate, b_e_gate, w_e_up, b_e_up, w_e_down, b_e_down)
    for layer in range(g_attn.shape[0]):
        x = _layer(x, *[p[layer] for p in params])
    return x
```
